```python
import math
import jax
import jax.numpy as jnp
from jax import lax
import numpy as np

D_MODEL = 2048
BATCH = 1
SEQ = 8192
DEPTH = 4
DEC_BATCH = 16
DEC_SEQ = 32
PAST_LEN = 4096

CHUNK = 64
N_BRANCH = 4
D_MIX = D_MODEL // 4
RET_HEADS = 4
RET_DK = D_MIX // RET_HEADS
RET_DV = D_MIX // RET_HEADS
RET_EXP_LO = 5.0
RET_EXP_HI = 12.0
ROPE_BASE = 10000.0
HGRN_HEADS = 4
HGRN_DK = D_MIX // HGRN_HEADS
HGRN_DV = D_MIX // HGRN_HEADS
RWKV_N = 64
RWKV_HEADS = D_MIX // RWKV_N
RWKV_W_LORA = 64
RWKV_A_LORA = 64
RWKV_G_LORA = 128
RWKV_GN_EPS = 64e-5
GDN_HEADS = 4
GDN_DK = D_MIX // GDN_HEADS
GDN_DV = D_MIX // GDN_HEADS
GDN_CONV = 4
RET_COLS = 4 * D_MIX
HGRN_COLS = 4 * D_MIX
RWKV_COLS = 3 * D_MIX + RWKV_W_LORA + RWKV_A_LORA + RWKV_G_LORA
GDN_COLS = 4 * D_MIX + 2 * GDN_HEADS
GATE_COLS = N_BRANCH * D_MODEL
N_IN = RET_COLS + HGRN_COLS + RWKV_COLS + GDN_COLS + GATE_COLS
N_GROUPS = 4
EXPERTS_PER_GROUP = 8
N_EXPERTS = N_GROUPS * EXPERTS_PER_GROUP
TOPK = 2
D_EXPERT = D_MODEL // 4
MOE_BLOCK = 128
NORM_EPS = 1e-6
GN_EPS = 1e-6

kernel_name = "hybrid_streaming_encoder_step"


def _rms(x, g):
    xf = x.astype(jnp.float32)
    y = xf * lax.rsqrt(jnp.mean(xf * xf, axis=-1, keepdims=True) + NORM_EPS)
    return (y * g.astype(jnp.float32)).astype(x.dtype)


def _head_rms(x, g):
    return x * lax.rsqrt(jnp.mean(x * x, axis=-1, keepdims=True) + NORM_EPS) * g


def _group_norm(x, eps):
    mu = jnp.mean(x, axis=-1, keepdims=True)
    xc = x - mu
    return xc * lax.rsqrt(jnp.mean(xc * xc, axis=-1, keepdims=True) + eps)


def _l2n(x):
    return x * lax.rsqrt(jnp.sum(x * x, axis=-1, keepdims=True) + 1e-6)


def _split_heads(x, n):
    B, L, W = x.shape
    return x.reshape(B, L, n, W // n).transpose(0, 2, 1, 3)


def _merge_heads(x):
    B, H, L, d = x.shape
    return x.transpose(0, 2, 1, 3).reshape(B, L, H * d)


def _rotary(x, pos):
    half = x.shape[-1] // 2
    inv = ROPE_BASE ** (-jnp.arange(half, dtype=jnp.float32) / half)
    ang = pos[:, None] * inv[None, :]
    cos, sin = jnp.cos(ang), jnp.sin(ang)
    x1, x2 = x[..., :half], x[..., half:]
    return jnp.concatenate([x1 * cos - x2 * sin, x1 * sin + x2 * cos], axis=-1)


def _to_blocks(x, blk):
    B, H, L = x.shape[:3]
    x = x.reshape((B, H, L // blk, blk) + x.shape[3:])
    return jnp.moveaxis(x, 2, 0)


def _from_blocks(y):
    n, B, H, blk = y.shape[:4]
    y = jnp.moveaxis(y, 0, 2)
    return y.reshape((B, H, n * blk) + y.shape[4:])


def _retention(q, k, v, log_gamma, S0, blk):
    t = jnp.arange(blk, dtype=jnp.float32)
    rel = t[:, None] - t[None, :]
    causal = rel >= 0
    lg = log_gamma[:, None, None]
    d_intra = jnp.where(causal, jnp.exp(lg * jnp.where(causal, rel, 0.0)), 0.0)
    d_q = jnp.exp(log_gamma[:, None] * (t + 1.0))[..., None]
    d_k = jnp.exp(log_gamma[:, None] * (blk - 1.0 - t))[..., None]
    d_s = jnp.exp(log_gamma * blk)[:, None, None]

    def step(S, xs):
        qc, kc, vc = xs
        att = jnp.einsum("bhtd,bhsd->bhts", qc, kc) * d_intra
        o = jnp.einsum("bhts,bhsv->bhtv", att, vc) + jnp.einsum("bhtd,bhdv->bhtv", qc * d_q, S)
        S = S * d_s + jnp.einsum("bhsd,bhsv->bhdv", kc * d_k, vc)
        return S, o

    S, o = lax.scan(step, S0, (_to_blocks(q, blk), _to_blocks(k, blk), _to_blocks(v, blk)))
    return _from_blocks(o), S


def _gla(q, k, v, g, S0, blk):
    causal = jnp.tril(jnp.ones((blk, blk), bool))[..., None]

    def step(S, xs):
        qc, kc, vc, gc = xs
        G = jnp.cumsum(gc, axis=2)
        rel = G[:, :, :, None, :] - G[:, :, None, :, :]
        dec = jnp.where(causal, jnp.exp(jnp.where(causal, rel, 0.0)), 0.0)
        att = jnp.sum(qc[:, :, :, None, :] * kc[:, :, None, :, :] * dec, axis=-1)
        o = jnp.einsum("bhts,bhsv->bhtv", att, vc) + jnp.einsum("bhtd,bhdv->bhtv", qc * jnp.exp(G), S)
        gl = G[:, :, -1:, :]
        S = jnp.exp(gl[:, :, 0, :, None]) * S + jnp.einsum("bhsd,bhsv->bhdv", kc * jnp.exp(gl - G), vc)
        return S, o

    S, o = lax.scan(step, S0, (_to_blocks(q, blk), _to_blocks(k, blk), _to_blocks(v, blk), _to_blocks(g, blk)))
    return _from_blocks(o), S


def _gated_delta(q, k, v, g, beta, S0, blk):
    causal = jnp.tril(jnp.ones((blk, blk), bool))
    strict = jnp.tril(jnp.ones((blk, blk), bool), -1)
    eye = jnp.eye(blk, dtype=jnp.float32)

    def step(S, xs):
        qc, kc, vc, gc, bc = xs
        G = jnp.cumsum(gc, axis=-1)
        rel = G[..., :, None] - G[..., None, :]
        dec = jnp.where(causal, jnp.exp(jnp.where(causal, rel, 0.0)), 0.0)
        a_mat = eye + jnp.where(strict, bc[..., :, None] * jnp.einsum("bhtd,bhsd->bhts", kc, kc) * dec, 0.0)
        rhs = bc[..., None] * (vc - jnp.exp(G)[..., None] * jnp.einsum("bhtd,bhdv->bhtv", kc, S))
        u = lax.linalg.triangular_solve(a_mat, rhs, left_side=True, lower=True, unit_diagonal=True)
        o = (jnp.exp(G)[..., None] * jnp.einsum("bhtd,bhdv->bhtv", qc, S)
             + jnp.einsum("bhts,bhsv->bhtv", jnp.einsum("bhtd,bhsd->bhts", qc, kc) * dec, u))
        gl = G[..., -1:]
        S = jnp.exp(gl)[..., None] * S + jnp.einsum("bhsd,bhsv->bhdv", kc * jnp.exp(gl - G)[..., None], u)
        return S, o

    xs = (_to_blocks(q, blk), _to_blocks(k, blk), _to_blocks(v, blk), _to_blocks(g, blk), _to_blocks(beta, blk))
    S, o = lax.scan(step, S0, xs)
    return _from_blocks(o), S


def _rwkv7(zc, s_shift, s_state, P, l):
    B, L, _ = zc.shape
    prev = jnp.concatenate([s_shift[:, None, :].astype(zc.dtype), zc[:, :-1]], axis=1)
    xs = zc + (prev - zc) * P["rwkv_mu"][l]
    o3 = 3 * D_MIX
    cuts = [D_MIX, 2 * D_MIX, o3, o3 + RWKV_W_LORA, o3 + RWKV_W_LORA + RWKV_A_LORA]
    r, k, v, w_lo, a_lo, g_lo = jnp.split(xs, cuts, axis=-1)
    w = -jax.nn.softplus(-(P["rwkv_w0"][l] + jnp.tanh(w_lo) @ P["rwkv_w2"][l])) - 0.5
    decay = jnp.exp(-jnp.exp(w))
    a = jax.nn.sigmoid(P["rwkv_a0"][l] + a_lo @ P["rwkv_a2"][l])
    g = jax.nn.sigmoid(g_lo) @ P["rwkv_g2"][l]
    hs = (B, L, RWKV_HEADS, RWKV_N)
    kk = _l2n((k * P["rwkv_k_k"][l]).reshape(hs))
    k = k * (1.0 + (a - 1.0) * P["rwkv_k_a"][l])
    rh, kh, vh = r.reshape(hs), k.reshape(hs), v.reshape(hs)
    bh = kk * a.reshape(hs)

    def step(S, inp):
        r_t, w_t, k_t, v_t, kk_t, b_t = inp
        S = (S * w_t[:, :, None, :]
             - jnp.einsum("bhvk,bhk->bhv", S, kk_t)[..., None] * b_t[:, :, None, :]
             + v_t[..., None] * k_t[:, :, None, :])
        return S, jnp.einsum("bhvk,bhk->bhv", S, r_t)

    seq = tuple(jnp.moveaxis(t, 1, 0) for t in (rh, decay.reshape(hs), kh, vh, kk, bh))
    S, y = lax.scan(step, s_state, seq)
    y = jnp.moveaxis(y, 0, 1)
    y = (_group_norm(y, RWKV_GN_EPS) * P["rwkv_ln_w"][l].reshape(RWKV_HEADS, RWKV_N)
         + P["rwkv_ln_b"][l].reshape(RWKV_HEADS, RWKV_N))
    y = y + jnp.sum(rh * kh * P["rwkv_r_k"][l], axis=-1, keepdims=True) * vh
    return y.reshape(B, L, D_MIX) * g, S, zc[:, -1]


def _mixers(h, pos0, blk, s_ret, s_hgrn, s_rwkv, s_shift, s_gdn, s_conv, P, l):
    B, L, _ = h.shape
    dt = h.dtype
    f32 = jnp.float32
    z = h @ P["w_in"][l]
    o1 = RET_COLS
    o2 = o1 + HGRN_COLS
    o3 = o2 + RWKV_COLS
    o4 = o3 + GDN_COLS
    z_ret, z_hgrn, z_rwkv, z_gdn, z_gate = jnp.split(z, [o1, o2, o3, o4], axis=-1)

    q, k, v, g = jnp.split(z_ret.astype(f32), 4, axis=-1)
    pos = jnp.arange(L, dtype=f32) + pos0
    q = _rotary(_split_heads(q, RET_HEADS), pos)
    k = _rotary(_split_heads(k, RET_HEADS), pos) * RET_DK ** -0.5
    log_gamma = jnp.log1p(-jnp.exp2(-jnp.linspace(RET_EXP_LO, RET_EXP_HI, RET_HEADS, dtype=f32)))
    o, s_ret_new = _retention(q, k, _split_heads(v, RET_HEADS), log_gamma, s_ret.astype(f32), blk)
    y_ret = _merge_heads(_group_norm(o, GN_EPS)) * jax.nn.silu(g)

    q, f, i, g = jnp.split(z_hgrn.astype(f32), 4, axis=-1)
    lb = P["hgrn_lb"][l]
    logf = jnp.logaddexp(jnp.log(lb), jnp.log1p(-lb) + jax.nn.log_sigmoid(f))
    kf = (1.0 - lb) * jax.nn.sigmoid(-f)
    o, s_hgrn_new = _gla(_split_heads(q, HGRN_HEADS), _split_heads(kf, HGRN_HEADS),
                         _split_heads(i, HGRN_HEADS), _split_heads(logf, HGRN_HEADS),
                         s_hgrn.astype(f32), blk)
    y_hgrn = _merge_heads(_head_rms(o, P["hgrn_norm_g"][l].astype(f32))) * jax.nn.silu(g)

    y_rwkv, s_rwkv_new, s_shift_new = _rwkv7(z_rwkv.astype(f32), s_shift, s_rwkv.astype(f32), P, l)

    qkv = z_gdn[..., :3 * D_MIX]
    zg = z_gdn[..., 3 * D_MIX:4 * D_MIX].astype(f32)
    a_in = z_gdn[..., 4 * D_MIX:4 * D_MIX + GDN_HEADS].astype(f32)
    b_in = z_gdn[..., 4 * D_MIX + GDN_HEADS:].astype(f32)
    xp = jnp.concatenate([s_conv.astype(dt), qkv], axis=1)
    cw = P["gdn_conv_w"][l]
    conv = sum(xp[:, j:j + L] * cw[j] for j in range(GDN_CONV))
    s_conv_new = xp[:, L:]
    q, k, v = jnp.split(jax.nn.silu(conv.astype(f32)), 3, axis=-1)
    q = _l2n(_split_heads(q, GDN_HEADS)) * GDN_DK ** -0.5
    k = _l2n(_split_heads(k, GDN_HEADS))
    gdec = -jnp.exp(P["gdn_A_log"][l].astype(f32)) * jax.nn.softplus(a_in + P["gdn_dt_bias"][l].astype(f32))
    beta = jax.nn.sigmoid(b_in)
    o, s_gdn_new = _gated_delta(q, k, _split_heads(v, GDN_HEADS), gdec.transpose(0, 2, 1),
                                beta.transpose(0, 2, 1), s_gdn.astype(f32), blk)
    y_gdn = _merge_heads(_head_rms(o, P["gdn_norm_g"][l].astype(f32))) * jax.nn.silu(zg)

    gates = jax.nn.sigmoid(z_gate.astype(f32)).astype(dt).reshape(B, L, N_BRANCH, D_MODEL)
    wb = P["w_branch"][l]
    merged = sum(gates[:, :, n] * (y.astype(dt) @ wb[n])
                 for n, y in enumerate((y_ret, y_hgrn, y_rwkv, y_gdn)))
    out = merged @ P["w_out"][l]
    return out, (s_ret_new, s_hgrn_new, s_rwkv_new, s_shift_new, s_gdn_new, s_conv_new)


def _moe(h, P, l):
    B, L, D = h.shape
    T = B * L
    f32 = jnp.float32
    xt = h.reshape(T, D)
    lg = (xt @ P["router_g"][l]).astype(f32) + P["router_g_b"][l].astype(f32)
    pg = jax.nn.softmax(lg, axis=-1)
    gsel = jnp.argmax(lg, axis=-1).astype(jnp.int32)
    gw = jnp.take_along_axis(pg, gsel[:, None], axis=-1)
    le = ((xt @ P["router_e"][l]).astype(f32) + P["router_e_b"][l].astype(f32)).reshape(T, N_GROUPS, EXPERTS_PER_GROUP)
    le = jnp.take_along_axis(le, gsel[:, None, None], axis=1)[:, 0]
    top_v, top_i = lax.top_k(jax.nn.softmax(le, axis=-1), TOPK)
    wts = top_v / jnp.sum(top_v, axis=-1, keepdims=True) * gw
    eid = (gsel[:, None] * EXPERTS_PER_GROUP + top_i).reshape(-1).astype(jnp.int32)
    A = T * TOPK
    order = jnp.argsort(eid)
    se = eid[order]
    counts = jax.ops.segment_sum(jnp.ones((A,), jnp.int32), eid, num_segments=N_EXPERTS)
    padded = (counts + MOE_BLOCK - 1) // MOE_BLOCK * MOE_BLOCK
    pend = jnp.cumsum(padded)
    pstart = pend - padded
    cstart = jnp.cumsum(counts) - counts
    dest_sorted = pstart[se] + jnp.arange(A, dtype=jnp.int32) - cstart[se]
    dest = jnp.zeros((A,), jnp.int32).at[order].set(dest_sorted)
    n_blocks = -(-A // MOE_BLOCK) + N_EXPERTS
    rows = n_blocks * MOE_BLOCK
    tok = jnp.full((rows,), T, jnp.int32).at[dest].set(jnp.arange(A, dtype=jnp.int32) // TOPK)
    xpad = jnp.concatenate([xt, jnp.zeros((1, D), xt.dtype)], axis=0)
    xb = xpad[tok].reshape(n_blocks, MOE_BLOCK, D)
    block_e = jnp.minimum(jnp.searchsorted(pend, jnp.arange(n_blocks, dtype=jnp.int32) * MOE_BLOCK, side="right"),
                          N_EXPERTS - 1)
    wg, wu, wd = P["moe_w_gate"][l], P["moe_w_up"][l], P["moe_w_down"][l]

    def expert_block(args):
        xblk, e = args
        return (jax.nn.silu(xblk @ wg[e]) * (xblk @ wu[e])) @ wd[e]

    yb = lax.map(expert_block, (xb, block_e)).reshape(rows, D)
    y = jnp.sum(yb[dest].reshape(T, TOPK, D) * wts[..., None].astype(yb.dtype), axis=1)
    return y.reshape(B, L, D)


def _trunk(x, c, pos0, st_ret, st_hgrn, st_rwkv, st_shift, st_gdn, st_conv, P):
    B, L, _ = x.shape
    blk = min(CHUNK, L)
    cmod = jax.nn.silu(c)
    new = ([], [], [], [], [], [])
    for l in range(DEPTH):
        mod = (cmod @ P["ada_w"][l] + P["ada_b"][l])[:, None, :]
        sh1, sc1, gt1, sh2, sc2, gt2 = jnp.split(mod, 6, axis=-1)
        h = _rms(x, P["norm1_g"][l]) * (1.0 + sc1) + sh1
        m, st = _mixers(h, pos0, blk, st_ret[l], st_hgrn[l], st_rwkv[l], st_shift[l], st_gdn[l], st_conv[l], P, l)
        x = x + gt1 * m
        h = _rms(x, P["norm2_g"][l]) * (1.0 + sc2) + sh2
        x = x + gt2 * _moe(h, P, l)
        for lst, s in zip(new, st):
            lst.append(s)
    y = _rms(x, P["final_norm_g"])
    return y, tuple(jnp.stack(lst).astype(x.dtype) for lst in new)


def setup_inputs(seed: int = 0) -> dict:
    key = jax.random.key(seed)
    ks = iter(jax.random.split(key, 64))
    f32 = jnp.float32

    def nrm(shape, scale):
        return jax.random.normal(next(ks), shape, f32) * scale

    def uni(shape, lo, hi):
        return jax.random.uniform(next(ks), shape, f32, lo, hi)

    D = D_MODEL
    dt = jnp.exp(uni((DEPTH, GDN_HEADS), math.log(1e-3), math.log(1e-1)))
    return {
        "x_prompt": nrm((BATCH, SEQ, D), 1.0),
        "x_sample": nrm((DEC_BATCH, DEC_SEQ, D), 1.0),
        "c_prompt": nrm((BATCH, D), 1.0),
        "c_sample": nrm((DEC_BATCH, D), 1.0),
        "state_ret": nrm((DEPTH, DEC_BATCH, RET_HEADS, RET_DK, RET_DV), 0.5),
        "state_hgrn": nrm((DEPTH, DEC_BATCH, HGRN_HEADS, HGRN_DK, HGRN_DV), 0.5),
        "state_rwkv": nrm((DEPTH, DEC_BATCH, RWKV_HEADS, RWKV_N, RWKV_N), 0.3),
        "state_rwkv_shift": nrm((DEPTH, DEC_BATCH, RWKV_COLS), 1.0),
        "state_gdn": nrm((DEPTH, DEC_BATCH, GDN_HEADS, GDN_DK, GDN_DV), 0.3),
        "state_gdn_conv": nrm((DEPTH, DEC_BATCH, GDN_CONV - 1, 3 * D_MIX), 1.0),
        "ada_w": nrm((DEPTH, D, 6 * D), 0.5 * D ** -0.5),
        "ada_b": nrm((DEPTH, 6 * D), 0.01),
        "norm1_g": 1.0 + nrm((DEPTH, D), 0.02),
        "norm2_g": 1.0 + nrm((DEPTH, D), 0.02),
        "w_in": nrm((DEPTH, D, N_IN), D ** -0.5),
        "hgrn_lb_logits": nrm((DEPTH, D_MIX), 1.0),
        "hgrn_norm_g": 1.0 + nrm((DEPTH, HGRN_DV), 0.02),
        "rwkv_mu": uni((DEPTH, RWKV_COLS), 0.0, 1.0),
        "rwkv_w0": uni((DEPTH, D_MIX), -6.0, -1.0),
        "rwkv_w2": nrm((DEPTH, RWKV_W_LORA, D_MIX), 0.1),
        "rwkv_a0": nrm((DEPTH, D_MIX), 0.1),
        "rwkv_a2": nrm((DEPTH, RWKV_A_LORA, D_MIX), RWKV_A_LORA ** -0.5),
        "rwkv_g2": nrm((DEPTH, RWKV_G_LORA, D_MIX), RWKV_G_LORA ** -0.5),
        "rwkv_k_k": 0.85 + nrm((DEPTH, D_MIX), 0.02),
        "rwkv_k_a": 1.0 + nrm((DEPTH, D_MIX), 0.02),
        "rwkv_r_k": nrm((DEPTH, RWKV_HEADS, RWKV_N), 0.1),
        "rwkv_ln_w": 1.0 + nrm((DEPTH, D_MIX), 0.02),
        "rwkv_ln_b": nrm((DEPTH, D_MIX), 0.01),
        "gdn_conv_w": nrm((DEPTH, GDN_CONV, 3 * D_MIX), GDN_CONV ** -0.5),
        "gdn_A_log": jnp.log(uni((DEPTH, GDN_HEADS), 1.0, 16.0)),
        "gdn_dt_bias": dt + jnp.log(-jnp.expm1(-dt)),
        "gdn_norm_g": 1.0 + nrm((DEPTH, GDN_DV), 0.02),
        "w_branch": nrm((DEPTH, N_BRANCH, D_MIX, D), D_MIX ** -0.5),
        "w_out": nrm((DEPTH, D, D), D ** -0.5),
        "router_g": nrm((DEPTH, D, N_GROUPS), D ** -0.5),
        "router_g_b": nrm((DEPTH, N_GROUPS), 0.01),
        "router_e": nrm((DEPTH, D, N_EXPERTS), D ** -0.5),
        "router_e_b": nrm((DEPTH, N_EXPERTS), 0.01),
        "moe_w_gate": nrm((DEPTH, N_EXPERTS, D, D_EXPERT), D ** -0.5),
        "moe_w_up": nrm((DEPTH, N_EXPERTS, D, D_EXPERT), D ** -0.5),
        "moe_w_down": nrm((DEPTH, N_EXPERTS, D_EXPERT, D), D_EXPERT ** -0.5),
        "final_norm_g": 1.0 + nrm((D,), 0.02),
    }


def reference(x_prompt, x_sample, c_prompt, c_sample, state_ret, state_hgrn, state_rwkv, state_rwkv_shift,
              state_gdn, state_gdn_conv, ada_w, ada_b, norm1_g, norm2_g, w_in, hgrn_lb_logits, hgrn_norm_g,
              rwkv_mu, rwkv_w0, rwkv_w2, rwkv_a0, rwkv_a2, rwkv_g2, rwkv_k_k, rwkv_k_a, rwkv_r_k, rwkv_ln_w,
              rwkv_ln_b, gdn_conv_w, gdn_A_log, gdn_dt_bias, gdn_norm_g, w_branch, w_out, router_g, router_g_b,
              router_e, router_e_b, moe_w_gate, moe_w_up, moe_w_down, final_norm_g):
    f32 = jnp.float32
    lb_cum = jnp.cumsum(jax.nn.softmax(hgrn_lb_logits.astype(f32), axis=0), axis=0)
    P = dict(ada_w=ada_w, ada_b=ada_b, norm1_g=norm1_g, norm2_g=norm2_g, w_in=w_in,
             hgrn_lb=lb_cum - lb_cum[:1], hgrn_norm_g=hgrn_norm_g,
             rwkv_mu=rwkv_mu, rwkv_w0=rwkv_w0, rwkv_w2=rwkv_w2, rwkv_a0=rwkv_a0, rwkv_a2=rwkv_a2,
             rwkv_g2=rwkv_g2, rwkv_k_k=rwkv_k_k, rwkv_k_a=rwkv_k_a, rwkv_r_k=rwkv_r_k,
             rwkv_ln_w=rwkv_ln_w, rwkv_ln_b=rwkv_ln_b,
             gdn_conv_w=gdn_conv_w, gdn_A_log=gdn_A_log, gdn_dt_bias=gdn_dt_bias, gdn_norm_g=gdn_norm_g,
             w_branch=w_branch, w_out=w_out, router_g=router_g, router_g_b=router_g_b,
             router_e=router_e, router_e_b=router_e_b, moe_w_gate=moe_w_gate, moe_w_up=moe_w_up,
             moe_w_down=moe_w_down, final_norm_g=final_norm_g)
    Bp = x_prompt.shape[0]

    def zeros(*s):
        return jnp.zeros((DEPTH, Bp) + s, f32)

    y_prompt, (ret_p, hgrn_p, rwkv_p, shift_p, gdn_p, conv_p) = _trunk(
        x_prompt, c_prompt, 0,
        zeros(RET_HEADS, RET_DK, RET_DV), zeros(HGRN_HEADS, HGRN_DK, HGRN_DV),
        zeros(RWKV_HEADS, RWKV_N, RWKV_N), zeros(RWKV_COLS),
        zeros(GDN_HEADS, GDN_DK, GDN_DV), zeros(GDN_CONV - 1, 3 * D_MIX), P)
    y_sample, (ret_s, hgrn_s, rwkv_s, shift_s, gdn_s, conv_s) = _trunk(
        x_sample, c_sample, PAST_LEN, state_ret, state_hgrn, state_rwkv, state_rwkv_shift,
        state_gdn, state_gdn_conv, P)
    return (y_prompt, y_sample, ret_p, hgrn_p, rwkv_p, shift_p, gdn_p, conv_p,
            ret_s, hgrn_s, rwkv_s, shift_s, gdn_s, conv_s)
```

```python
import functools
import math

import numpy as np
import jax
import jax.numpy as jnp
from jax import lax
from jax.experimental import pallas as pl
from jax.experimental.pallas import tpu as pltpu

F32 = jnp.float32
BF16 = jnp.bfloat16

D_MODEL = 2048
D_MIX = 512
HEAD_DIM = 128
N_HEADS = 4
RWKV_N = 64
RWKV_HEADS = 8
RWKV_COLS = 1792
GDN_CONV = 4
N_BRANCH = 4
N_GROUPS = 4
EXPERTS_PER_GROUP = 8
N_EXPERTS = 32
TOPK = 2
D_EXPERT = 512
PAST_LEN = 4096
ROPE_BASE = 10000.0
NORM_EPS = 1e-6
GN_EPS = 1e-6
RWKV_GN_EPS = 64e-5
RET_EXP_LO, RET_EXP_HI = 5.0, 12.0

Z_RET, Z_HGRN, Z_RWKV, Z_GDN, Z_AB = 0, 2048, 4096, 6144, 8192
N_MAIN = 8448
MIX_W = 2048
GROUP = 32

VMEM_LIMIT = 56 * 1024 * 1024


def _cparams(sem):
    return pltpu.CompilerParams(dimension_semantics=sem, vmem_limit_bytes=VMEM_LIMIT)


def _mm(a, b):
    return jnp.dot(a.astype(BF16), b.astype(BF16), preferred_element_type=F32)


def _mm_nt(a, b):
    return lax.dot_general(a.astype(BF16), b.astype(BF16), (((1,), (1,)), ((), ())),
                           preferred_element_type=F32)


def _mm_tn(a, b):
    return lax.dot_general(a.astype(BF16), b.astype(BF16), (((0,), (0,)), ((), ())),
                           preferred_element_type=F32)


def _silu(x):
    return x * jax.nn.sigmoid(x)


def _softplus(x):
    return jnp.maximum(x, 0.0) + jnp.log1p(jnp.exp(-jnp.abs(x)))


def _cumsum_rows(x):
    n = x.shape[0]
    row = lax.broadcasted_iota(jnp.int32, x.shape, 0)
    s = 1
    while s < n:
        x = x + jnp.where(row >= s, pltpu.roll(x, s, 0), 0.0)
        s *= 2
    return x


def _row_from_col(col, eye):
    return jnp.sum(jnp.where(eye, col, 0.0), axis=0, keepdims=True)


def _tri_inv(n_mat, c):
    r = lax.broadcasted_iota(jnp.int32, (c, c), 0)
    col = lax.broadcasted_iota(jnp.int32, (c, c), 1)
    x = jnp.where(r == col, 1.0, 0.0) - jnp.where((r >> 1) == (col >> 1), n_mat, 0.0)
    m, sh = 2, 1
    while m < c:
        off = jnp.where(((r >> (sh + 1)) == (col >> (sh + 1))) & ((r >> sh) != (col >> sh)), n_mat, 0.0)
        x = x - _mm(x, _mm(off, x))
        m, sh = m * 2, sh + 1
    return x


def _ada_kernel(c_ref, w_ref, b_ref, o_ref):
    cm = _silu(c_ref[...])
    o_ref[0] = _mm(cm, w_ref[0]) + b_ref[0]


def _ada(c_all, ada_w, ada_b):
    depth, d, n = ada_w.shape
    rows = c_all.shape[0]
    tn = 1024
    return pl.pallas_call(
        _ada_kernel,
        grid=(depth, n // tn),
        in_specs=[pl.BlockSpec((rows, d), lambda l, j: (0, 0)),
                  pl.BlockSpec((1, d, tn), lambda l, j: (l, 0, j)),
                  pl.BlockSpec((1, 1, tn), lambda l, j: (l, 0, j))],
        out_specs=pl.BlockSpec((1, rows, tn), lambda l, j: (l, 0, j)),
        out_shape=jax.ShapeDtypeStruct((depth, rows, n), F32),
        compiler_params=_cparams(("arbitrary", "arbitrary")),
        name="ada",
    )(c_all, ada_w, ada_b.reshape(depth, 1, n))


def _group_block(tm, n_prompt_tiles):
    gpt = tm // GROUP
    first_sample = 16 // gpt

    def idx(i):
        return jnp.where(i < n_prompt_tiles, 0, first_sample + i - n_prompt_tiles)
    return gpt, idx


def _modulate(y, sc, sh, gpt):
    tm, d = y.shape
    y3 = y.reshape(gpt, GROUP, d)
    return (y3 * (1.0 + sc[:, None, :]) + sh[:, None, :]).reshape(tm, d)


def _rms_rows(x, g):
    return x * lax.rsqrt(jnp.mean(x * x, axis=-1, keepdims=True) + NORM_EPS) * g


def _norm_kernel(*refs, gpt, with_moe, modulated):
    it = iter(refs)
    x_ref = next(it)
    if with_moe:
        m_ref, gt_ref = next(it), next(it)
    g_ref = next(it)
    if modulated:
        sc_ref, sh_ref = next(it), next(it)
    x = x_ref[...]
    tm, d = x.shape
    if with_moe:
        m = m_ref[...]
        moe = m[:, :d] + m[:, d:]
        gt = gt_ref[0]
        x = x + (moe.reshape(gpt, GROUP, d) * gt[:, None, :]).reshape(tm, d)
    y = _rms_rows(x, g_ref[...])
    if modulated:
        xo_ref, h_ref = next(it), next(it)
        xo_ref[...] = x
        h_ref[...] = _modulate(y, sc_ref[0], sh_ref[0], gpt).astype(BF16)
    else:
        y_ref = next(it)
        y_ref[...] = y


def _norm(x, moe_out, modg, l, g, n_prompt, *, final):
    t, d = x.shape
    tm = 256
    gpt, gidx = _group_block(tm, n_prompt // tm)
    with_moe = moe_out is not None
    row = pl.BlockSpec((tm, d), lambda i: (i, 0))

    def mod_spec(col, layer):
        return pl.BlockSpec((1, gpt, d), lambda i: (layer, gidx(i), col))
    args, specs = [x], [row]
    if with_moe:
        lm = l if final else l - 1
        args += [moe_out, modg]
        specs += [pl.BlockSpec((tm, 2 * d), lambda i: (i, 0)), mod_spec(5, lm)]
    args.append(g.reshape(1, d))
    specs.append(pl.BlockSpec((1, d), lambda i: (0, 0)))
    if not final:
        args += [modg, modg]
        specs += [mod_spec(1, l), mod_spec(0, l)]
        out_shape = [jax.ShapeDtypeStruct((t, d), F32), jax.ShapeDtypeStruct((t, d), BF16)]
        out_specs = [row, row]
    else:
        out_shape = jax.ShapeDtypeStruct((t, d), F32)
        out_specs = row
    return pl.pallas_call(
        functools.partial(_norm_kernel, gpt=gpt, with_moe=with_moe, modulated=not final),
        grid=(t // tm,), in_specs=specs, out_specs=out_specs, out_shape=out_shape,
        compiler_params=_cparams(("arbitrary",)), name="norm",
    )(*args)


def _proj_kernel(a_ref, w_ref, o_ref, *, sigmoid):
    acc = jnp.dot(a_ref[...], w_ref[...], preferred_element_type=F32)
    if sigmoid:
        acc = jax.nn.sigmoid(acc)
    o_ref[...] = acc.astype(o_ref.dtype)


def _proj(a, w, *, tn, sigmoid, out_dtype):
    t, k = a.shape
    n = w.shape[1]
    tm = 512
    return pl.pallas_call(
        functools.partial(_proj_kernel, sigmoid=sigmoid),
        grid=(n // tn, t // tm),
        in_specs=[pl.BlockSpec((tm, k), lambda j, i: (i, 0)),
                  pl.BlockSpec((k, tn), lambda j, i: (0, j))],
        out_specs=pl.BlockSpec((tm, tn), lambda j, i: (i, j)),
        out_shape=jax.ShapeDtypeStruct((t, n), out_dtype),
        compiler_params=_cparams(("arbitrary", "arbitrary")), name="proj",
    )(a, w)


def _ret_tables(c):
    e = np.linspace(RET_EXP_LO, RET_EXP_HI, N_HEADS)
    lg = np.log1p(-np.exp2(-e))
    t = np.arange(c, dtype=np.float64)
    rel = t[:, None] - t[None, :]
    d_intra = np.where(rel >= 0, np.exp(lg[:, None, None] * np.where(rel >= 0, rel, 0.0)), 0.0)
    d_q = np.exp(lg[:, None] * (t + 1.0))[:, :, None] * np.ones((1, 1, HEAD_DIM))
    d_k = np.exp(lg[:, None] * (c - 1.0 - t))[:, :, None] * np.ones((1, 1, HEAD_DIM))
    d_s = np.exp(lg * c)[:, None, None] * np.ones((1, 8, HEAD_DIM))
    return tuple(jnp.asarray(a, F32) for a in (d_intra, d_q, d_k, d_s))


def _ret_kernel(*refs, c, nc, has_state):
    it = iter(refs)
    z_ref, cos_ref, sin_ref, di_ref, dq_ref, dk_ref, ds_ref = (next(it) for _ in range(7))
    s0_ref = next(it) if has_state else None
    y_ref, so_ref, s_scr = next(it), next(it), next(it)
    n = pl.program_id(1)

    @pl.when(n == 0)
    def _():
        s_scr[...] = s0_ref[0] if has_state else jnp.zeros(s_scr.shape, F32)

    cos, sin = cos_ref[...], sin_ref[...]
    half = HEAD_DIM // 2
    for h in range(N_HEADS):
        lo = h * HEAD_DIM
        q = z_ref[:, lo:lo + HEAD_DIM]
        k = z_ref[:, D_MIX + lo:D_MIX + lo + HEAD_DIM]
        v = z_ref[:, 2 * D_MIX + lo:2 * D_MIX + lo + HEAD_DIM]
        g = z_ref[:, 3 * D_MIX + lo:3 * D_MIX + lo + HEAD_DIM]
        q = q * cos + pltpu.roll(q, half, 1) * sin
        k = (k * cos + pltpu.roll(k, half, 1) * sin) * HEAD_DIM ** -0.5
        s = s_scr[h]
        att = _mm_nt(q, k) * di_ref[h]
        o = _mm(att, v) + _mm(q * dq_ref[h], s)
        s_scr[h] = s * ds_ref[h, 0:1, :] + _mm_tn(k * dk_ref[h], v)
        xc = o - jnp.mean(o, axis=-1, keepdims=True)
        on = xc * lax.rsqrt(jnp.mean(xc * xc, axis=-1, keepdims=True) + GN_EPS)
        y_ref[:, lo:lo + HEAD_DIM] = (on * _silu(g)).astype(y_ref.dtype)

    @pl.when(n == nc - 1)
    def _():
        so_ref[0] = s_scr[...]


def _rope_tables(length, pos0):
    half = HEAD_DIM // 2
    inv = ROPE_BASE ** (-jnp.arange(half, dtype=F32) / half)
    pos = jnp.arange(length, dtype=F32) + pos0
    ang = pos[:, None] * inv[None, :]
    cos, sin = jnp.cos(ang), jnp.sin(ang)
    return jnp.concatenate([cos, cos], axis=1), jnp.concatenate([-sin, sin], axis=1)


def _row_block(row0, c, nc):
    off = row0 // c
    return lambda b, n: off + b * nc + n


def _retention(z, row0, bsz, length, c, pos0, state):
    nc = length // c
    rb = _row_block(row0, c, nc)
    cos, sin = _rope_tables(length, pos0)
    di, dq, dk, ds = _ret_tables(c)
    has_state = state is not None
    const3 = lambda b, n: (0, 0, 0)
    args = [z, cos, sin, di, dq, dk, ds]
    specs = [pl.BlockSpec((c, MIX_W), lambda b, n: (rb(b, n), Z_RET // MIX_W)),
             pl.BlockSpec((c, HEAD_DIM), lambda b, n: (n, 0)),
             pl.BlockSpec((c, HEAD_DIM), lambda b, n: (n, 0)),
             pl.BlockSpec(di.shape, const3), pl.BlockSpec(dq.shape, const3),
             pl.BlockSpec(dk.shape, const3), pl.BlockSpec(ds.shape, const3)]
    sshape = (N_HEADS, HEAD_DIM, HEAD_DIM)
    if has_state:
        args.append(state)
        specs.append(pl.BlockSpec((1,) + sshape, lambda b, n: (b, 0, 0, 0)))
    return pl.pallas_call(
        functools.partial(_ret_kernel, c=c, nc=nc, has_state=has_state),
        grid=(bsz, nc), in_specs=specs,
        out_specs=[pl.BlockSpec((c, D_MIX), lambda b, n: (b * nc + n, 0)),
                   pl.BlockSpec((1,) + sshape, lambda b, n: (b, 0, 0, 0))],
        out_shape=[jax.ShapeDtypeStruct((bsz * length, D_MIX), BF16),
                   jax.ShapeDtypeStruct((bsz,) + sshape, F32)],
        scratch_shapes=[pltpu.VMEM(sshape, F32)],
        compiler_params=_cparams(("arbitrary", "arbitrary")), name="retention",
    )(*args)


SUB = 16


def _gla_head(q, k, v, g, st, c):
    big = _cumsum_rows(g)
    gl = big[c - 1:c, :]
    o = _mm_nt(q * jnp.exp(big), st)
    st_new = st * jnp.exp(gl) + _mm_tn(v, k * jnp.exp(gl - big))

    row = lax.broadcasted_iota(jnp.int32, (c, HEAD_DIM), 0)
    r2 = lax.broadcasted_iota(jnp.int32, (c, c), 0)
    c2 = lax.broadcasted_iota(jnp.int32, (c, c), 1)
    att = jnp.zeros((c, c), F32)
    m, sh = SUB, 4
    while m < c:
        anchor = jnp.concatenate(
            [jnp.broadcast_to(big[p * 2 * m + m - 1:p * 2 * m + m, :], (2 * m, HEAD_DIM))
             for p in range(c // (2 * m))], axis=0)
        right = ((row >> sh) & 1) == 1
        qt = jnp.where(right, q * jnp.exp(jnp.where(right, big - anchor, 0.0)), 0.0)
        kt = jnp.where(right, 0.0, k * jnp.exp(jnp.where(right, 0.0, anchor - big)))
        att = att + jnp.where((r2 >> (sh + 1)) == (c2 >> (sh + 1)), _mm_nt(qt, kt), 0.0)
        m, sh = m * 2, sh + 1
    o = o + _mm(att, v)

    f = jnp.exp(g)
    rsub = lax.broadcasted_iota(jnp.int32, (SUB, HEAD_DIM), 0)
    diag = []
    for blk in range(c // SUB):
        r0 = blk * SUB
        vb = v[r0:r0 + SUB, :]
        w = jnp.zeros((SUB, HEAD_DIM), F32)
        ob = jnp.zeros((SUB, HEAD_DIM), F32)
        for tl in range(SUB):
            t = r0 + tl
            if tl:
                w = w * f[t:t + 1, :]
            w = jnp.where(rsub == tl, k[t:t + 1, :], w)
            a_col = jnp.sum(w * q[t:t + 1, :], axis=1, keepdims=True)
            o_t = jnp.sum(a_col * vb, axis=0, keepdims=True)
            ob = jnp.where(rsub == tl, o_t, ob)
        diag.append(ob)
    return o + jnp.concatenate(diag, axis=0), st_new


def _hgrn_kernel(*refs, c, nc, has_state):
    it = iter(refs)
    z_ref, llb_ref, l1m_ref, oml_ref, gn_ref = (next(it) for _ in range(5))
    s0_ref = next(it) if has_state else None
    y_ref, so_ref, s_scr = next(it), next(it), next(it)
    n = pl.program_id(1)

    @pl.when(n == 0)
    def _():
        for h in range(N_HEADS):
            s_scr[h] = s0_ref[0, h].T if has_state else jnp.zeros((HEAD_DIM, HEAD_DIM), F32)

    for h in range(N_HEADS):
        lo = h * HEAD_DIM
        q = z_ref[:, lo:lo + HEAD_DIM]
        f = z_ref[:, D_MIX + lo:D_MIX + lo + HEAD_DIM]
        iv = z_ref[:, 2 * D_MIX + lo:2 * D_MIX + lo + HEAD_DIM]
        g = z_ref[:, 3 * D_MIX + lo:3 * D_MIX + lo + HEAD_DIM]
        ls = jnp.minimum(f, 0.0) - jnp.log1p(jnp.exp(-jnp.abs(f)))
        a = llb_ref[:, lo:lo + HEAD_DIM]
        b = l1m_ref[:, lo:lo + HEAD_DIM] + ls
        logf = jnp.maximum(a, b) + jnp.log1p(jnp.exp(-jnp.abs(a - b)))
        kf = oml_ref[:, lo:lo + HEAD_DIM] * jax.nn.sigmoid(-f)
        o, st = _gla_head(q, kf, iv, logf, s_scr[h], c)
        s_scr[h] = st
        y = _rms_rows(o, gn_ref[...]) * _silu(g)
        y_ref[:, lo:lo + HEAD_DIM] = y.astype(y_ref.dtype)

    @pl.when(n == nc - 1)
    def _():
        for h in range(N_HEADS):
            so_ref[0, h] = s_scr[h].T


def _hgrn(z, row0, bsz, length, c, lb, gnorm, state):
    nc = length // c
    rb = _row_block(row0, c, nc)
    has_state = state is not None
    vec = pl.BlockSpec((1, D_MIX), lambda b, n: (0, 0))
    lb = lb.reshape(1, D_MIX)
    args = [z, jnp.log(lb), jnp.log1p(-lb), 1.0 - lb, gnorm.reshape(1, HEAD_DIM)]
    specs = [pl.BlockSpec((c, MIX_W), lambda b, n: (rb(b, n), Z_HGRN // MIX_W)), vec, vec, vec,
             pl.BlockSpec((1, HEAD_DIM), lambda b, n: (0, 0))]
    sshape = (N_HEADS, HEAD_DIM, HEAD_DIM)
    if has_state:
        args.append(state)
        specs.append(pl.BlockSpec((1,) + sshape, lambda b, n: (b, 0, 0, 0)))
    return pl.pallas_call(
        functools.partial(_hgrn_kernel, c=c, nc=nc, has_state=has_state),
        grid=(bsz, nc), in_specs=specs,
        out_specs=[pl.BlockSpec((c, D_MIX), lambda b, n: (b * nc + n, 0)),
                   pl.BlockSpec((1,) + sshape, lambda b, n: (b, 0, 0, 0))],
        out_shape=[jax.ShapeDtypeStruct((bsz * length, D_MIX), BF16),
                   jax.ShapeDtypeStruct((bsz,) + sshape, F32)],
        scratch_shapes=[pltpu.VMEM(sshape, F32)],
        compiler_params=_cparams(("arbitrary", "arbitrary")), name="hgrn2",
    )(*args)


def _rwkv_kernel(*refs, c, nc, has_state):
    it = iter(refs)
    (z_ref, mu_ref, w0_ref, w2_ref, a0_ref, a2_ref, g2_ref, kk_ref, ka_ref, rk_ref,
     lnw_ref, lnb_ref) = (next(it) for _ in range(12))
    if has_state:
        sh0_ref, s0_ref = next(it), next(it)
    y_ref, so_ref, s_scr, prev_scr = next(it), next(it), next(it), next(it)
    n = pl.program_id(1)
    nh, hd = RWKV_HEADS, RWKV_N

    @pl.when(n == 0)
    def _():
        s_scr[...] = s0_ref[0] if has_state else jnp.zeros(s_scr.shape, F32)
        prev_scr[...] = (jnp.broadcast_to(sh0_ref[0], prev_scr.shape) if has_state
                         else jnp.zeros(prev_scr.shape, F32))

    z = z_ref[...]
    row = lax.broadcasted_iota(jnp.int32, z.shape, 0)
    prev = jnp.where(row == 0, prev_scr[0:1, :], pltpu.roll(z, 1, 0))
    prev_scr[...] = jnp.broadcast_to(z[c - 1:c, :], prev_scr.shape)
    xs = z + (prev - z) * mu_ref[...]
    o3 = 3 * D_MIX
    r, k, v = xs[:, :D_MIX], xs[:, D_MIX:2 * D_MIX], xs[:, 2 * D_MIX:o3]
    w_lo, a_lo, g_lo = xs[:, o3:o3 + 64], xs[:, o3 + 64:o3 + 128], xs[:, o3 + 128:o3 + 256]
    w = -_softplus(-(w0_ref[...] + _mm(jnp.tanh(w_lo), w2_ref[...]))) - 0.5
    ld = -jnp.exp(w)
    a = jax.nn.sigmoid(a0_ref[...] + _mm(a_lo, a2_ref[...]))
    g = _mm(jax.nn.sigmoid(g_lo), g2_ref[...])
    kkv = k * kk_ref[...]
    k = k * (1.0 + (a - 1.0) * ka_ref[...])
    lc = _cumsum_rows(ld)
    e_in = jnp.exp(lc)
    e_in_neg = jnp.exp(-lc)
    e_ex = jnp.exp(lc - ld)
    bonus = r * k * rk_ref[...]

    r2 = lax.broadcasted_iota(jnp.int32, (c, c), 0)
    c2 = lax.broadcasted_iota(jnp.int32, (c, c), 1)
    strict = r2 > c2
    incl = r2 >= c2
    for h in range(nh):
        sl = slice(h * hd, (h + 1) * hd)
        kkh = kkv[:, sl]
        kap = kkh * lax.rsqrt(jnp.sum(kkh * kkh, axis=-1, keepdims=True) + 1e-6)
        bh = kap * a[:, sl]
        kap_h = kap * e_ex[:, sl]
        b_h = bh * e_in_neg[:, sl]
        k_h = k[:, sl] * e_in_neg[:, sl]
        r_h = r[:, sl] * e_in[:, sl]
        vh = v[:, sl]
        s0 = s_scr[h]
        m_b = jnp.where(strict, _mm_nt(kap_h, b_h), 0.0)
        m_k = jnp.where(strict, _mm_nt(kap_h, k_h), 0.0)
        r_b = jnp.where(incl, _mm_nt(r_h, b_h), 0.0)
        r_k = jnp.where(incl, _mm_nt(r_h, k_h), 0.0)
        t_inv = _tri_inv(m_b, c)
        u = -_mm(t_inv, _mm_nt(kap_h, s0) + _mm(m_k, vh))
        y = _mm_nt(r_h, s0) + _mm(r_b, u) + _mm(r_k, vh)
        s_new = (s0 + _mm_tn(u, b_h) + _mm_tn(vh, k_h)) * e_in[c - 1:c, sl]
        s_scr[h] = s_new
        yc = y - jnp.mean(y, axis=-1, keepdims=True)
        yn = yc * lax.rsqrt(jnp.mean(yc * yc, axis=-1, keepdims=True) + RWKV_GN_EPS)
        yn = yn * lnw_ref[:, sl] + lnb_ref[:, sl]
        yn = yn + jnp.sum(bonus[:, sl], axis=-1, keepdims=True) * vh
        y_ref[:, sl] = (yn * g[:, sl]).astype(y_ref.dtype)

    @pl.when(n == nc - 1)
    def _():
        so_ref[0] = s_scr[...]


def _rwkv(z, row0, bsz, length, c, p, shift, state):
    nc = length // c
    rb = _row_block(row0, c, nc)
    has_state = state is not None
    pad = MIX_W - RWKV_COLS
    c2 = lambda b, n: (0, 0)
    args = [z, jnp.pad(p["mu"], (0, pad)).reshape(1, MIX_W),
            p["w0"].reshape(1, D_MIX), p["w2"], p["a0"].reshape(1, D_MIX), p["a2"], p["g2"],
            p["k_k"].reshape(1, D_MIX), p["k_a"].reshape(1, D_MIX), p["r_k"].reshape(1, D_MIX),
            p["ln_w"].reshape(1, D_MIX), p["ln_b"].reshape(1, D_MIX)]
    specs = [pl.BlockSpec((c, MIX_W), lambda b, n: (rb(b, n), Z_RWKV // MIX_W))]
    specs += [pl.BlockSpec(a.shape, c2) for a in args[1:]]
    sshape = (RWKV_HEADS, RWKV_N, RWKV_N)
    if has_state:
        args += [jnp.pad(shift, ((0, 0), (0, pad))).reshape(bsz, 1, MIX_W), state]
        specs += [pl.BlockSpec((1, 1, MIX_W), lambda b, n: (b, 0, 0)),
                  pl.BlockSpec((1,) + sshape, lambda b, n: (b, 0, 0, 0))]
    return pl.pallas_call(
        functools.partial(_rwkv_kernel, c=c, nc=nc, has_state=has_state),
        grid=(bsz, nc), in_specs=specs,
        out_specs=[pl.BlockSpec((c, D_MIX), lambda b, n: (b * nc + n, 0)),
                   pl.BlockSpec((1,) + sshape, lambda b, n: (b, 0, 0, 0))],
        out_shape=[jax.ShapeDtypeStruct((bsz * length, D_MIX), BF16),
                   jax.ShapeDtypeStruct((bsz,) + sshape, F32)],
        scratch_shapes=[pltpu.VMEM(sshape, F32), pltpu.VMEM((8, MIX_W), F32)],
        compiler_params=_cparams(("arbitrary", "arbitrary")), name="rwkv7",
    )(*args)


CONV_PAD = 8


def _gdn_kernel(*refs, c, nc, has_state):
    it = iter(refs)
    z_ref, ab_ref, cw_ref, alog_ref, dtb_ref, gn_ref = (next(it) for _ in range(6))
    if has_state:
        cv0_ref, s0_ref = next(it), next(it)
    y_ref, so_ref, s_scr, xp_scr = next(it), next(it), next(it), next(it)
    n = pl.program_id(1)
    qkv_w = 3 * D_MIX
    tail = GDN_CONV - 1

    @pl.when(n == 0)
    def _():
        s_scr[...] = s0_ref[0] if has_state else jnp.zeros(s_scr.shape, F32)
        xp_scr[0:CONV_PAD, :] = jnp.zeros((CONV_PAD, qkv_w), F32)
        if has_state:
            xp_scr[CONV_PAD - tail:CONV_PAD, :] = cv0_ref[0]

    xp_scr[CONV_PAD:CONV_PAD + c, :] = z_ref[:, :qkv_w]
    conv = xp_scr[CONV_PAD - tail:CONV_PAD - tail + c, :] * cw_ref[0:1, :]
    for j in range(1, GDN_CONV):
        conv = conv + xp_scr[CONV_PAD - tail + j:CONV_PAD - tail + j + c, :] * cw_ref[j:j + 1, :]
    xp_scr[CONV_PAD - tail:CONV_PAD, :] = xp_scr[CONV_PAD + c - tail:CONV_PAD + c, :]
    act = _silu(conv)

    ab = ab_ref[...]
    gdec = -jnp.exp(alog_ref[...]) * _softplus(ab + dtb_ref[...])
    gcum = _cumsum_rows(gdec)
    beta = jax.nn.sigmoid(ab)

    r2 = lax.broadcasted_iota(jnp.int32, (c, c), 0)
    c2 = lax.broadcasted_iota(jnp.int32, (c, c), 1)
    eye, causal, strict = r2 == c2, r2 >= c2, r2 > c2
    for h in range(N_HEADS):
        lo = h * HEAD_DIM
        q = act[:, lo:lo + HEAD_DIM]
        k = act[:, D_MIX + lo:D_MIX + lo + HEAD_DIM]
        v = act[:, 2 * D_MIX + lo:2 * D_MIX + lo + HEAD_DIM]
        zg = z_ref[:, qkv_w + lo:qkv_w + lo + HEAD_DIM]
        q = q * lax.rsqrt(jnp.sum(q * q, axis=-1, keepdims=True) + 1e-6) * HEAD_DIM ** -0.5
        k = k * lax.rsqrt(jnp.sum(k * k, axis=-1, keepdims=True) + 1e-6)
        gcol = gcum[:, h:h + 1]
        bcol = beta[:, N_HEADS + h:N_HEADS + h + 1]
        rel = gcol - _row_from_col(gcol, eye)
        dec = jnp.where(causal, jnp.exp(jnp.where(causal, rel, 0.0)), 0.0)
        n_mat = jnp.where(strict, bcol * _mm_nt(k, k) * dec, 0.0)
        t_inv = _tri_inv(n_mat, c)
        s = s_scr[h]
        eg = jnp.exp(gcol)
        u = _mm(t_inv, bcol * (v - eg * _mm(k, s)))
        o = eg * _mm(q, s) + _mm(_mm_nt(q, k) * dec, u)
        gl = gcol[c - 1:c, :]
        s_scr[h] = jnp.exp(gl) * s + _mm_tn(k * jnp.exp(gl - gcol), u)
        y = _rms_rows(o, gn_ref[...]) * _silu(zg)
        y_ref[:, lo:lo + HEAD_DIM] = y.astype(y_ref.dtype)

    @pl.when(n == nc - 1)
    def _():
        so_ref[0] = s_scr[...]


def _gdn(z, row0, bsz, length, c, p, conv_state, state):
    nc = length // c
    rb = _row_block(row0, c, nc)
    has_state = state is not None
    c2 = lambda b, n: (0, 0)
    lane_pad = lambda a: jnp.pad(a, (0, HEAD_DIM - a.shape[0])).reshape(1, HEAD_DIM)
    args = [z, z, p["conv_w"], lane_pad(p["A_log"]), lane_pad(p["dt_bias"]), p["norm_g"].reshape(1, HEAD_DIM)]
    specs = [pl.BlockSpec((c, MIX_W), lambda b, n: (rb(b, n), Z_GDN // MIX_W)),
             pl.BlockSpec((c, HEAD_DIM), lambda b, n: (rb(b, n), Z_AB // HEAD_DIM))]
    specs += [pl.BlockSpec(a.shape, c2) for a in args[2:]]
    sshape = (N_HEADS, HEAD_DIM, HEAD_DIM)
    if has_state:
        args += [conv_state, state]
        specs += [pl.BlockSpec((1, GDN_CONV - 1, 3 * D_MIX), lambda b, n: (b, 0, 0)),
                  pl.BlockSpec((1,) + sshape, lambda b, n: (b, 0, 0, 0))]
    return pl.pallas_call(
        functools.partial(_gdn_kernel, c=c, nc=nc, has_state=has_state),
        grid=(bsz, nc), in_specs=specs,
        out_specs=[pl.BlockSpec((c, D_MIX), lambda b, n: (b * nc + n, 0)),
                   pl.BlockSpec((1,) + sshape, lambda b, n: (b, 0, 0, 0))],
        out_shape=[jax.ShapeDtypeStruct((bsz * length, D_MIX), BF16),
                   jax.ShapeDtypeStruct((bsz,) + sshape, F32)],
        scratch_shapes=[pltpu.VMEM(sshape, F32), pltpu.VMEM((CONV_PAD + c, 3 * D_MIX), F32)],
        compiler_params=_cparams(("arbitrary", "arbitrary")), name="gdn",
    )(*args)


def _merge_kernel(y0_ref, y1_ref, y2_ref, y3_ref, gate_ref, x_ref, gt_ref, sc_ref, sh_ref, g_ref,
                  wb_ref, wo_ref, rw_ref, rb_ref, xo_ref, h_ref, lg_ref, *, gpt):
    tm, d = x_ref.shape
    merged = jnp.zeros((tm, d), F32)
    for nb, y_ref in enumerate((y0_ref, y1_ref, y2_ref, y3_ref)):
        br = jnp.dot(y_ref[...], wb_ref[nb], preferred_element_type=F32)
        merged = merged + gate_ref[:, nb * d:(nb + 1) * d].astype(F32) * br
    m = jnp.dot(merged.astype(BF16), wo_ref[...], preferred_element_type=F32)
    gt = gt_ref[0]
    x = x_ref[...] + (m.reshape(gpt, GROUP, d) * gt[:, None, :]).reshape(tm, d)
    xo_ref[...] = x
    h = _modulate(_rms_rows(x, g_ref[...]), sc_ref[0], sh_ref[0], gpt)
    h_ref[...] = h
    lg_ref[...] = jnp.dot(h, rw_ref[...], precision=lax.Precision.HIGHEST,
                          preferred_element_type=F32) + rb_ref[...]


def _merge(ys, gates, x, modg, l, g2, wb, wo, rw, rb, n_prompt):
    t, d = x.shape
    tm = 256
    gpt, gidx = _group_block(tm, n_prompt // tm)
    row = lambda w: pl.BlockSpec((tm, w), lambda i: (i, 0))

    def mod_spec(col):
        return pl.BlockSpec((1, gpt, d), lambda i: (l, gidx(i), col))

    def const(shape):
        return pl.BlockSpec(shape, lambda i: (0,) * len(shape), pipeline_mode=pl.Buffered(1))
    nr = rw.shape[1]
    return pl.pallas_call(
        functools.partial(_merge_kernel, gpt=gpt),
        grid=(t // tm,),
        in_specs=[row(D_MIX)] * 4 + [row(N_BRANCH * d), row(d), mod_spec(2), mod_spec(4), mod_spec(3),
                                     const((1, d)), const(wb.shape), const(wo.shape), const(rw.shape),
                                     const((1, nr))],
        out_specs=[row(d), row(d), row(nr)],
        out_shape=[jax.ShapeDtypeStruct((t, d), F32), jax.ShapeDtypeStruct((t, d), F32),
                   jax.ShapeDtypeStruct((t, nr), F32)],
        compiler_params=_cparams(("arbitrary",)), name="merge",
    )(*ys, gates, x, modg, modg, modg, g2.reshape(1, d), wb, wo, rw, rb)


MOE_BLK = 256


def _moe_kernel(be_ref, tok_ref, slot_ref, nused_ref, h_hbm, w_ref, wg_ref, wu_ref, wd_ref, o_hbm,
                xbuf, ybuf, gsem, ssem):
    i = pl.program_id(0)
    base = i * MOE_BLK

    def gather(r):
        return pltpu.make_async_copy(h_hbm.at[pl.ds(tok_ref[base + r], 1)], xbuf.at[pl.ds(r, 1)], gsem)

    def scatter(r):
        return pltpu.make_async_copy(ybuf.at[pl.ds(r, 1)], o_hbm.at[pl.ds(slot_ref[base + r], 1)], ssem)

    @pl.when(i < nused_ref[0])
    def _():
        def g_start(r, carry):
            gather(r).start()
            return carry

        def g_wait(r, carry):
            gather(r).wait()
            return carry
        lax.fori_loop(0, MOE_BLK, g_start, 0)
        lax.fori_loop(0, MOE_BLK, g_wait, 0)
        x = xbuf[...].astype(BF16)
        hid = _silu(jnp.dot(x, wg_ref[0], preferred_element_type=F32)) * jnp.dot(
            x, wu_ref[0], preferred_element_type=F32)
        ybuf[...] = jnp.dot(hid.astype(BF16), wd_ref[0], preferred_element_type=F32) * w_ref[...]

        def s_start(r, carry):
            @pl.when(slot_ref[base + r] >= 0)
            def _():
                scatter(r).start()
            return carry

        def s_wait(r, carry):
            @pl.when(slot_ref[base + r] >= 0)
            def _():
                scatter(r).wait()
            return carry
        lax.fori_loop(0, MOE_BLK, s_start, 0)
        lax.fori_loop(0, MOE_BLK, s_wait, 0)


def _route(logits, n_tok):
    lg = logits[:, :N_GROUPS]
    pg = jax.nn.softmax(lg, axis=-1)
    gsel = jnp.argmax(lg, axis=-1).astype(jnp.int32)
    gw = jnp.take_along_axis(pg, gsel[:, None], axis=-1)
    le = logits[:, N_GROUPS:N_GROUPS + N_EXPERTS].reshape(n_tok, N_GROUPS, EXPERTS_PER_GROUP)
    le = jnp.take_along_axis(le, gsel[:, None, None], axis=1)[:, 0]
    top_v, top_i = lax.top_k(jax.nn.softmax(le, axis=-1), TOPK)
    wts = top_v / jnp.sum(top_v, axis=-1, keepdims=True) * gw
    eid = (gsel[:, None] * EXPERTS_PER_GROUP + top_i).reshape(-1).astype(jnp.int32)
    a = n_tok * TOPK
    order = jnp.argsort(eid)
    se = eid[order]
    counts = jax.ops.segment_sum(jnp.ones((a,), jnp.int32), eid, num_segments=N_EXPERTS)
    padded = (counts + MOE_BLK - 1) // MOE_BLK * MOE_BLK
    pend = jnp.cumsum(padded)
    pstart = pend - padded
    cstart = jnp.cumsum(counts) - counts
    dest_sorted = pstart[se] + jnp.arange(a, dtype=jnp.int32) - cstart[se]
    n_blocks = -(-a // MOE_BLK) + N_EXPERTS
    rows = n_blocks * MOE_BLK
    tok = jnp.zeros((rows,), jnp.int32).at[dest_sorted].set(order // TOPK)
    slot = jnp.full((rows,), -1, jnp.int32).at[dest_sorted].set(order)
    wrow = jnp.zeros((rows,), F32).at[dest_sorted].set(wts.reshape(-1)[order])
    block_e = jnp.minimum(
        jnp.searchsorted(pend, jnp.arange(n_blocks, dtype=jnp.int32) * MOE_BLK, side="right"),
        N_EXPERTS - 1).astype(jnp.int32)
    nused = (pend[-1:] // MOE_BLK).astype(jnp.int32)
    return block_e, tok, slot, nused, wrow.reshape(rows, 1), n_blocks


def _moe(h, logits, wg, wu, wd):
    t, d = h.shape
    block_e, tok, slot, nused, wrow, n_blocks = _route(logits, t)
    grid_spec = pltpu.PrefetchScalarGridSpec(
        num_scalar_prefetch=4, grid=(n_blocks,),
        in_specs=[pl.BlockSpec(memory_space=pl.ANY),
                  pl.BlockSpec((MOE_BLK, 1), lambda i, be, *_: (i, 0)),
                  pl.BlockSpec((1, d, D_EXPERT), lambda i, be, *_: (be[i], 0, 0)),
                  pl.BlockSpec((1, d, D_EXPERT), lambda i, be, *_: (be[i], 0, 0)),
                  pl.BlockSpec((1, D_EXPERT, d), lambda i, be, *_: (be[i], 0, 0))],
        out_specs=pl.BlockSpec(memory_space=pl.ANY),
        scratch_shapes=[pltpu.VMEM((MOE_BLK, d), F32), pltpu.VMEM((MOE_BLK, d), F32),
                        pltpu.SemaphoreType.DMA, pltpu.SemaphoreType.DMA])
    out = pl.pallas_call(
        _moe_kernel, grid_spec=grid_spec,
        out_shape=jax.ShapeDtypeStruct((t * TOPK, d), F32),
        compiler_params=_cparams(("arbitrary",)), name="moe",
    )(block_e, tok, slot, nused, h, wrow, wg, wu, wd)
    return out.reshape(t, TOPK * d)


def _pack_w_in(w_in):
    depth, d, _ = w_in.shape
    o_rwkv, o_gdn = 4096, 4096 + RWKV_COLS
    o_ab = o_gdn + 4 * D_MIX
    o_gate = o_ab + 2 * N_HEADS
    zeros = lambda n: jnp.zeros((depth, d, n), BF16)
    wb = w_in.astype(BF16)
    w_main = jnp.concatenate(
        [wb[:, :, :o_gdn], zeros(Z_GDN - o_gdn), wb[:, :, o_gdn:o_ab], wb[:, :, o_ab:o_gate],
         zeros(N_MAIN - Z_AB - 2 * N_HEADS)], axis=2)
    return w_main, wb[:, :, o_gate:]


def _mixers(z, l, groups, params):
    outs = [[] for _ in range(N_BRANCH)]
    states = []
    for (row0, bsz, length, c, pos0, st) in groups:
        s_ret, s_hgrn, s_rwkv, s_shift, s_gdn, s_conv = st if st is not None else (None,) * 6
        y_a, n_ret = _retention(z, row0, bsz, length, c, pos0, s_ret)
        y_b, n_hgrn = _hgrn(z, row0, bsz, length, c, params["hgrn_lb"][l], params["hgrn_norm_g"][l], s_hgrn)
        y_c, n_rwkv = _rwkv(z, row0, bsz, length, c, {k: v[l] for k, v in params["rwkv"].items()},
                            s_shift, s_rwkv)
        y_d, n_gdn = _gdn(z, row0, bsz, length, c, {k: v[l] for k, v in params["gdn"].items()},
                          s_conv, s_gdn)
        for lst, y in zip(outs, (y_a, y_b, y_c, y_d)):
            lst.append(y)
        z3 = z[row0:row0 + bsz * length].reshape(bsz, length, -1)
        n_shift = z3[:, -1, Z_RWKV:Z_RWKV + RWKV_COLS]
        n_conv = z3[:, length - (GDN_CONV - 1):, Z_GDN:Z_GDN + 3 * D_MIX]
        states.append((n_ret, n_hgrn, n_rwkv, n_shift, n_gdn, n_conv))
    return [jnp.concatenate(lst, axis=0) for lst in outs], states


def kernel(x_prompt, x_sample, c_prompt, c_sample, state_ret, state_hgrn, state_rwkv, state_rwkv_shift,
           state_gdn, state_gdn_conv, ada_w, ada_b, norm1_g, norm2_g, w_in, hgrn_lb_logits, hgrn_norm_g,
           rwkv_mu, rwkv_w0, rwkv_w2, rwkv_a0, rwkv_a2, rwkv_g2, rwkv_k_k, rwkv_k_a, rwkv_r_k, rwkv_ln_w,
           rwkv_ln_b, gdn_conv_w, gdn_A_log, gdn_dt_bias, gdn_norm_g, w_branch, w_out, router_g, router_g_b,
           router_e, router_e_b, moe_w_gate, moe_w_up, moe_w_down, final_norm_g):
    depth = ada_w.shape[0]
    bp, lp, d = x_prompt.shape
    bs, ls, _ = x_sample.shape
    n_prompt = bp * lp
    assert bp == 1 and ls == GROUP and bs == 16

    lb_cum = jnp.cumsum(jax.nn.softmax(hgrn_lb_logits.astype(F32), axis=0), axis=0)
    params = dict(
        hgrn_lb=lb_cum - lb_cum[:1], hgrn_norm_g=hgrn_norm_g,
        rwkv=dict(mu=rwkv_mu, w0=rwkv_w0, w2=rwkv_w2, a0=rwkv_a0, a2=rwkv_a2, g2=rwkv_g2, k_k=rwkv_k_k,
                  k_a=rwkv_k_a, r_k=rwkv_r_k.reshape(depth, D_MIX), ln_w=rwkv_ln_w, ln_b=rwkv_ln_b),
        gdn=dict(conv_w=gdn_conv_w, A_log=gdn_A_log, dt_bias=gdn_dt_bias, norm_g=gdn_norm_g))

    c_all = jnp.concatenate([c_prompt, c_sample, jnp.zeros((24 - bp - bs, d), F32)], axis=0)
    mod = _ada(c_all, ada_w, ada_b)
    modg = jnp.concatenate([jnp.broadcast_to(mod[:, :1], (depth, 16, 6 * d)), mod[:, 1:1 + bs]], axis=1)

    w_main, w_gate = _pack_w_in(w_in)
    wb_bf, wo_bf = w_branch.astype(BF16), w_out.astype(BF16)
    wg_bf, wu_bf, wd_bf = moe_w_gate.astype(BF16), moe_w_up.astype(BF16), moe_w_down.astype(BF16)
    n_r = 128
    r_w = jnp.concatenate([router_g, router_e, jnp.zeros((depth, d, n_r - N_GROUPS - N_EXPERTS), F32)], axis=2)
    r_b = jnp.concatenate([router_g_b, router_e_b, jnp.zeros((depth, n_r - N_GROUPS - N_EXPERTS), F32)],
                          axis=1).reshape(depth, 1, n_r)

    x = jnp.concatenate([x_prompt.reshape(n_prompt, d), x_sample.reshape(bs * ls, d)], axis=0)
    moe_out = None
    new_p, new_s = [], []
    for l in range(depth):
        x, h = _norm(x, moe_out, modg, l, norm1_g[l], n_prompt, final=False)
        z = _proj(h, w_main[l], tn=768, sigmoid=False, out_dtype=F32)
        gates = _proj(h, w_gate[l], tn=1024, sigmoid=True, out_dtype=BF16)
        groups = [(0, bp, lp, 64, 0.0, None),
                  (n_prompt, bs, ls, ls, float(PAST_LEN),
                   (state_ret[l], state_hgrn[l], state_rwkv[l], state_rwkv_shift[l], state_gdn[l],
                    state_gdn_conv[l]))]
        ys, (st_p, st_s) = _mixers(z, l, groups, params)
        new_p.append(st_p)
        new_s.append(st_s)
        x, h2, logits = _merge(ys, gates, x, modg, l, norm2_g[l], wb_bf[l], wo_bf[l], r_w[l], r_b[l], n_prompt)
        moe_out = _moe(h2, logits, wg_bf[l], wu_bf[l], wd_bf[l])
    y = _norm(x, moe_out, modg, depth - 1, final_norm_g, n_prompt, final=True)

    def stack(lst, i):
        return jnp.stack([s[i] for s in lst]).astype(F32)
    return ((y[:n_prompt].reshape(bp, lp, d), y[n_prompt:].reshape(bs, ls, d))
            + tuple(stack(new_p, i) for i in range(6)) + tuple(stack(new_s, i) for i in range(6)))
```

```python
import functools
import math

import numpy as np
import jax
import jax.numpy as jnp
from jax import lax
from jax.experimental import pallas as pl
from jax.experimental.pallas import tpu as pltpu

F32 = jnp.float32
BF16 = jnp.bfloat16

D_MODEL = 2048
D_MIX = 512
HEAD_DIM = 128
N_HEADS = 4
RWKV_N = 64
RWKV_HEADS = 8
RWKV_COLS = 1792
GDN_CONV = 4
N_BRANCH = 4
N_GROUPS = 4
EXPERTS_PER_GROUP = 8
N_EXPERTS = 32
TOPK = 2
D_EXPERT = 512
PAST_LEN = 4096
ROPE_BASE = 10000.0
NORM_EPS = 1e-6
GN_EPS = 1e-6
RWKV_GN_EPS = 64e-5
RET_EXP_LO, RET_EXP_HI = 5.0, 12.0

Z_RET, Z_HGRN, Z_RWKV, Z_GDN, Z_AB = 0, 2048, 4096, 6144, 8192
N_MAIN = 8448
MIX_W = 2048
GROUP = 32

VMEM_LIMIT = 56 * 1024 * 1024


def _cparams(sem):
    return pltpu.CompilerParams(dimension_semantics=sem, vmem_limit_bytes=VMEM_LIMIT)


def _mm(a, b):
    return jnp.dot(a.astype(BF16), b.astype(BF16), preferred_element_type=F32)


def _mm_nt(a, b):
    return lax.dot_general(a.astype(BF16), b.astype(BF16), (((1,), (1,)), ((), ())),
                           preferred_element_type=F32)


def _mm_tn(a, b):
    return lax.dot_general(a.astype(BF16), b.astype(BF16), (((0,), (0,)), ((), ())),
                           preferred_element_type=F32)


def _silu(x):
    return x * jax.nn.sigmoid(x)


def _softplus(x):
    return jnp.maximum(x, 0.0) + jnp.log1p(jnp.exp(-jnp.abs(x)))


def _cumsum_rows(x):
    n = x.shape[0]
    row = lax.broadcasted_iota(jnp.int32, x.shape, 0)
    s = 1
    while s < n:
        x = x + jnp.where(row >= s, pltpu.roll(x, s, 0), 0.0)
        s *= 2
    return x


def _row_from_col(col, eye):
    return jnp.sum(jnp.where(eye, col, 0.0), axis=0, keepdims=True)


def _tri_inv(n_mats, c):
    r = lax.broadcasted_iota(jnp.int32, (c, c), 0)
    col = lax.broadcasted_iota(jnp.int32, (c, c), 1)
    eye = jnp.where(r == col, 1.0, 0.0)
    pair = (r >> 1) == (col >> 1)
    xs = [eye - jnp.where(pair, n, 0.0) for n in n_mats]
    m, sh = 2, 1
    while m < c:
        lvl = ((r >> (sh + 1)) == (col >> (sh + 1))) & ((r >> sh) != (col >> sh))
        ts = [_mm(jnp.where(lvl, n, 0.0), x) for n, x in zip(n_mats, xs)]
        xs = [x - _mm(x, t) for x, t in zip(xs, ts)]
        m, sh = m * 2, sh + 1
    return xs


def _ada_kernel(c_ref, w_ref, b_ref, o_ref):
    cm = _silu(c_ref[...])
    o_ref[0] = _mm(cm, w_ref[0]) + b_ref[0]


def _ada(c_all, ada_w, ada_b):
    depth, d, n = ada_w.shape
    rows = c_all.shape[0]
    tn = 1024
    return pl.pallas_call(
        _ada_kernel,
        grid=(depth, n // tn),
        in_specs=[pl.BlockSpec((rows, d), lambda l, j: (0, 0)),
                  pl.BlockSpec((1, d, tn), lambda l, j: (l, 0, j)),
                  pl.BlockSpec((1, 1, tn), lambda l, j: (l, 0, j))],
        out_specs=pl.BlockSpec((1, rows, tn), lambda l, j: (l, 0, j)),
        out_shape=jax.ShapeDtypeStruct((depth, rows, n), F32),
        compiler_params=_cparams(("arbitrary", "arbitrary")),
        name="ada",
    )(c_all, ada_w, ada_b.reshape(depth, 1, n))


def _group_block(tm, n_prompt_tiles):
    gpt = tm // GROUP
    first_sample = 16 // gpt

    def idx(i):
        return jnp.where(i < n_prompt_tiles, 0, first_sample + i - n_prompt_tiles)
    return gpt, idx


def _modulate(y, sc, sh, gpt):
    tm, d = y.shape
    y3 = y.reshape(gpt, GROUP, d)
    return (y3 * (1.0 + sc[:, None, :]) + sh[:, None, :]).reshape(tm, d)


def _rms_rows(x, g):
    return x * lax.rsqrt(jnp.mean(x * x, axis=-1, keepdims=True) + NORM_EPS) * g


def _norm_kernel(*refs, gpt, with_moe, modulated):
    it = iter(refs)
    x_ref = next(it)
    if with_moe:
        m0_ref, m1_ref, gt_ref = next(it), next(it), next(it)
    g_ref = next(it)
    if modulated:
        sc_ref, sh_ref = next(it), next(it)
    x = x_ref[...]
    tm, d = x.shape
    if with_moe:
        moe = m0_ref[...] + m1_ref[...]
        gt = gt_ref[0]
        x = x + (moe.reshape(gpt, GROUP, d) * gt[:, None, :]).reshape(tm, d)
    y = _rms_rows(x, g_ref[...])
    if modulated:
        xo_ref, h_ref = next(it), next(it)
        xo_ref[...] = x
        h_ref[...] = _modulate(y, sc_ref[0], sh_ref[0], gpt).astype(BF16)
    else:
        y_ref = next(it)
        y_ref[...] = y


def _norm(x, moe_out, modg, l, g, n_prompt, *, final):
    t, d = x.shape
    tm = 256
    gpt, gidx = _group_block(tm, n_prompt // tm)
    with_moe = moe_out is not None
    row = pl.BlockSpec((tm, d), lambda i: (i, 0))

    def mod_spec(col, layer):
        return pl.BlockSpec((1, gpt, d), lambda i: (layer, gidx(i), col))
    args, specs = [x], [row]
    if with_moe:
        lm = l if final else l - 1
        args += [moe_out, moe_out, modg]
        specs += [row, pl.BlockSpec((tm, d), lambda i: (t // tm + i, 0)), mod_spec(5, lm)]
    args.append(g.reshape(1, d))
    specs.append(pl.BlockSpec((1, d), lambda i: (0, 0)))
    if not final:
        args += [modg, modg]
        specs += [mod_spec(1, l), mod_spec(0, l)]
        out_shape = [jax.ShapeDtypeStruct((t, d), F32), jax.ShapeDtypeStruct((t, d), BF16)]
        out_specs = [row, row]
    else:
        out_shape = jax.ShapeDtypeStruct((t, d), F32)
        out_specs = row
    return pl.pallas_call(
        functools.partial(_norm_kernel, gpt=gpt, with_moe=with_moe, modulated=not final),
        grid=(t // tm,), in_specs=specs, out_specs=out_specs, out_shape=out_shape,
        compiler_params=_cparams(("arbitrary",)), name="norm",
    )(*args)


def _proj_kernel(a_ref, w_ref, o_ref, *, sigmoid):
    acc = jnp.dot(a_ref[...], w_ref[0], preferred_element_type=F32)
    if sigmoid:
        acc = jax.nn.sigmoid(acc)
    o_ref[...] = acc.astype(o_ref.dtype)


def _proj(a, w, l, *, tn, sigmoid, out_dtype):
    t, k = a.shape
    n = w.shape[2]
    tm = 512
    return pl.pallas_call(
        functools.partial(_proj_kernel, sigmoid=sigmoid),
        grid=(n // tn, t // tm),
        in_specs=[pl.BlockSpec((tm, k), lambda j, i: (i, 0)),
                  pl.BlockSpec((1, k, tn), lambda j, i: (l, 0, j))],
        out_specs=pl.BlockSpec((tm, tn), lambda j, i: (i, j)),
        out_shape=jax.ShapeDtypeStruct((t, n), out_dtype),
        compiler_params=_cparams(("arbitrary", "arbitrary")), name="proj",
    )(a, w)


def _ret_tables(c):
    e = np.linspace(RET_EXP_LO, RET_EXP_HI, N_HEADS)
    lg = np.log1p(-np.exp2(-e))
    t = np.arange(c, dtype=np.float64)
    rel = t[:, None] - t[None, :]
    d_intra = np.where(rel >= 0, np.exp(lg[:, None, None] * np.where(rel >= 0, rel, 0.0)), 0.0)
    d_q = np.exp(lg[:, None] * (t + 1.0))[:, :, None] * np.ones((1, 1, HEAD_DIM))
    d_k = np.exp(lg[:, None] * (c - 1.0 - t))[:, :, None] * np.ones((1, 1, HEAD_DIM))
    d_s = np.exp(lg * c)[:, None, None] * np.ones((1, 8, HEAD_DIM))
    return tuple(jnp.asarray(a, F32) for a in (d_intra, d_q, d_k, d_s))


def _ret_kernel(*refs, c, nc, has_state):
    it = iter(refs)
    z_ref, cos_ref, sin_ref, di_ref, dq_ref, dk_ref, ds_ref = (next(it) for _ in range(7))
    s0_ref = next(it) if has_state else None
    y_ref, so_ref, s_scr = next(it), next(it), next(it)
    n = pl.program_id(1)

    @pl.when(n == 0)
    def _():
        s_scr[...] = s0_ref[0] if has_state else jnp.zeros(s_scr.shape, F32)

    cos, sin = cos_ref[...], sin_ref[...]
    half = HEAD_DIM // 2
    for h in range(N_HEADS):
        lo = h * HEAD_DIM
        q = z_ref[:, lo:lo + HEAD_DIM]
        k = z_ref[:, D_MIX + lo:D_MIX + lo + HEAD_DIM]
        v = z_ref[:, 2 * D_MIX + lo:2 * D_MIX + lo + HEAD_DIM]
        g = z_ref[:, 3 * D_MIX + lo:3 * D_MIX + lo + HEAD_DIM]
        q = q * cos + pltpu.roll(q, half, 1) * sin
        k = (k * cos + pltpu.roll(k, half, 1) * sin) * HEAD_DIM ** -0.5
        s = s_scr[h]
        att = _mm_nt(q, k) * di_ref[h]
        o = _mm(att, v) + _mm(q * dq_ref[h], s)
        s_scr[h] = s * ds_ref[h, 0:1, :] + _mm_tn(k * dk_ref[h], v)
        xc = o - jnp.mean(o, axis=-1, keepdims=True)
        on = xc * lax.rsqrt(jnp.mean(xc * xc, axis=-1, keepdims=True) + GN_EPS)
        y_ref[:, lo:lo + HEAD_DIM] = (on * _silu(g)).astype(y_ref.dtype)

    @pl.when(n == nc - 1)
    def _():
        so_ref[0] = s_scr[...]


def _rope_tables(length, pos0):
    half = HEAD_DIM // 2
    inv = ROPE_BASE ** (-jnp.arange(half, dtype=F32) / half)
    pos = jnp.arange(length, dtype=F32) + pos0
    ang = pos[:, None] * inv[None, :]
    cos, sin = jnp.cos(ang), jnp.sin(ang)
    return jnp.concatenate([cos, cos], axis=1), jnp.concatenate([-sin, sin], axis=1)


def _row_block(row0, c, nc):
    off = row0 // c
    return lambda b, n: off + b * nc + n


def _retention(z, row0, bsz, length, c, pos0, state):
    nc = length // c
    rb = _row_block(row0, c, nc)
    cos, sin = _rope_tables(length, pos0)
    di, dq, dk, ds = _ret_tables(c)
    has_state = state is not None
    const3 = lambda b, n: (0, 0, 0)
    args = [z, cos, sin, di, dq, dk, ds]
    specs = [pl.BlockSpec((c, MIX_W), lambda b, n: (rb(b, n), Z_RET // MIX_W)),
             pl.BlockSpec((c, HEAD_DIM), lambda b, n: (n, 0)),
             pl.BlockSpec((c, HEAD_DIM), lambda b, n: (n, 0)),
             pl.BlockSpec(di.shape, const3), pl.BlockSpec(dq.shape, const3),
             pl.BlockSpec(dk.shape, const3), pl.BlockSpec(ds.shape, const3)]
    sshape = (N_HEADS, HEAD_DIM, HEAD_DIM)
    if has_state:
        args.append(state)
        specs.append(pl.BlockSpec((1,) + sshape, lambda b, n: (b, 0, 0, 0)))
    return pl.pallas_call(
        functools.partial(_ret_kernel, c=c, nc=nc, has_state=has_state),
        grid=(bsz, nc), in_specs=specs,
        out_specs=[pl.BlockSpec((c, D_MIX), lambda b, n: (b * nc + n, 0)),
                   pl.BlockSpec((1,) + sshape, lambda b, n: (b, 0, 0, 0))],
        out_shape=[jax.ShapeDtypeStruct((bsz * length, D_MIX), BF16),
                   jax.ShapeDtypeStruct((bsz,) + sshape, F32)],
        scratch_shapes=[pltpu.VMEM(sshape, F32)],
        compiler_params=_cparams(("arbitrary", "arbitrary")), name="retention",
    )(*args)


SUB = 16


def _gla_head(q, k, v, g, st, c):
    big = _cumsum_rows(g)
    gl = big[c - 1:c, :]
    o = _mm_nt(q * jnp.exp(big), st)
    st_new = st * jnp.exp(gl) + _mm_tn(v, k * jnp.exp(gl - big))

    row = lax.broadcasted_iota(jnp.int32, (c, HEAD_DIM), 0)
    r2 = lax.broadcasted_iota(jnp.int32, (c, c), 0)
    c2 = lax.broadcasted_iota(jnp.int32, (c, c), 1)
    att = jnp.zeros((c, c), F32)
    m, sh = SUB, 4
    while m < c:
        anchor = jnp.concatenate(
            [jnp.broadcast_to(big[p * 2 * m + m - 1:p * 2 * m + m, :], (2 * m, HEAD_DIM))
             for p in range(c // (2 * m))], axis=0)
        right = ((row >> sh) & 1) == 1
        qt = jnp.where(right, q * jnp.exp(jnp.where(right, big - anchor, 0.0)), 0.0)
        kt = jnp.where(right, 0.0, k * jnp.exp(jnp.where(right, 0.0, anchor - big)))
        att = att + jnp.where((r2 >> (sh + 1)) == (c2 >> (sh + 1)), _mm_nt(qt, kt), 0.0)
        m, sh = m * 2, sh + 1
    o = o + _mm(att, v)

    f = jnp.exp(g)
    rsub = lax.broadcasted_iota(jnp.int32, (SUB, HEAD_DIM), 0)
    diag = []
    for blk in range(c // SUB):
        r0 = blk * SUB
        vb = v[r0:r0 + SUB, :]
        w = jnp.zeros((SUB, HEAD_DIM), F32)
        ob = jnp.zeros((SUB, HEAD_DIM), F32)
        for tl in range(SUB):
            t = r0 + tl
            if tl:
                w = w * f[t:t + 1, :]
            w = jnp.where(rsub == tl, k[t:t + 1, :], w)
            a_col = jnp.sum(w * q[t:t + 1, :], axis=1, keepdims=True)
            o_t = jnp.sum(a_col * vb, axis=0, keepdims=True)
            ob = jnp.where(rsub == tl, o_t, ob)
        diag.append(ob)
    return o + jnp.concatenate(diag, axis=0), st_new


def _hgrn_kernel(*refs, c, nc, has_state):
    it = iter(refs)
    z_ref, llb_ref, l1m_ref, oml_ref, gn_ref = (next(it) for _ in range(5))
    s0_ref = next(it) if has_state else None
    y_ref, so_ref, s_scr = next(it), next(it), next(it)
    n = pl.program_id(1)

    @pl.when(n == 0)
    def _():
        for h in range(N_HEADS):
            s_scr[h] = s0_ref[0, h].T if has_state else jnp.zeros((HEAD_DIM, HEAD_DIM), F32)

    for h in range(N_HEADS):
        lo = h * HEAD_DIM
        q = z_ref[:, lo:lo + HEAD_DIM]
        f = z_ref[:, D_MIX + lo:D_MIX + lo + HEAD_DIM]
        iv = z_ref[:, 2 * D_MIX + lo:2 * D_MIX + lo + HEAD_DIM]
        g = z_ref[:, 3 * D_MIX + lo:3 * D_MIX + lo + HEAD_DIM]
        ls = jnp.minimum(f, 0.0) - jnp.log1p(jnp.exp(-jnp.abs(f)))
        a = llb_ref[:, lo:lo + HEAD_DIM]
        b = l1m_ref[:, lo:lo + HEAD_DIM] + ls
        logf = jnp.maximum(a, b) + jnp.log1p(jnp.exp(-jnp.abs(a - b)))
        kf = oml_ref[:, lo:lo + HEAD_DIM] * jax.nn.sigmoid(-f)
        o, st = _gla_head(q, kf, iv, logf, s_scr[h], c)
        s_scr[h] = st
        y = _rms_rows(o, gn_ref[...]) * _silu(g)
        y_ref[:, lo:lo + HEAD_DIM] = y.astype(y_ref.dtype)

    @pl.when(n == nc - 1)
    def _():
        for h in range(N_HEADS):
            so_ref[0, h] = s_scr[h].T


def _hgrn(z, row0, bsz, length, c, lb, gnorm, state):
    nc = length // c
    rb = _row_block(row0, c, nc)
    has_state = state is not None
    vec = pl.BlockSpec((1, D_MIX), lambda b, n: (0, 0))
    lb = lb.reshape(1, D_MIX)
    args = [z, jnp.log(lb), jnp.log1p(-lb), 1.0 - lb, gnorm.reshape(1, HEAD_DIM)]
    specs = [pl.BlockSpec((c, MIX_W), lambda b, n: (rb(b, n), Z_HGRN // MIX_W)), vec, vec, vec,
             pl.BlockSpec((1, HEAD_DIM), lambda b, n: (0, 0))]
    sshape = (N_HEADS, HEAD_DIM, HEAD_DIM)
    if has_state:
        args.append(state)
        specs.append(pl.BlockSpec((1,) + sshape, lambda b, n: (b, 0, 0, 0)))
    return pl.pallas_call(
        functools.partial(_hgrn_kernel, c=c, nc=nc, has_state=has_state),
        grid=(bsz, nc), in_specs=specs,
        out_specs=[pl.BlockSpec((c, D_MIX), lambda b, n: (b * nc + n, 0)),
                   pl.BlockSpec((1,) + sshape, lambda b, n: (b, 0, 0, 0))],
        out_shape=[jax.ShapeDtypeStruct((bsz * length, D_MIX), BF16),
                   jax.ShapeDtypeStruct((bsz,) + sshape, F32)],
        scratch_shapes=[pltpu.VMEM(sshape, F32)],
        compiler_params=_cparams(("arbitrary", "arbitrary")), name="hgrn2",
    )(*args)


def _rwkv_kernel(*refs, c, nc, has_state):
    it = iter(refs)
    (z_ref, mu_ref, w0_ref, w2_ref, a0_ref, a2_ref, g2_ref, kk_ref, ka_ref, rk_ref,
     lnw_ref, lnb_ref) = (next(it) for _ in range(12))
    if has_state:
        sh0_ref, s0_ref = next(it), next(it)
    y_ref, so_ref, s_scr, prev_scr = next(it), next(it), next(it), next(it)
    n = pl.program_id(1)
    nh, hd = RWKV_HEADS, RWKV_N

    @pl.when(n == 0)
    def _():
        s_scr[...] = s0_ref[0] if has_state else jnp.zeros(s_scr.shape, F32)
        prev_scr[...] = (jnp.broadcast_to(sh0_ref[0], prev_scr.shape) if has_state
                         else jnp.zeros(prev_scr.shape, F32))

    z = z_ref[...]
    row = lax.broadcasted_iota(jnp.int32, z.shape, 0)
    prev = jnp.where(row == 0, prev_scr[0:1, :], pltpu.roll(z, 1, 0))
    prev_scr[...] = jnp.broadcast_to(z[c - 1:c, :], prev_scr.shape)
    xs = z + (prev - z) * mu_ref[...]
    o3 = 3 * D_MIX
    r, k, v = xs[:, :D_MIX], xs[:, D_MIX:2 * D_MIX], xs[:, 2 * D_MIX:o3]
    w_lo, a_lo, g_lo = xs[:, o3:o3 + 64], xs[:, o3 + 64:o3 + 128], xs[:, o3 + 128:o3 + 256]
    w = -_softplus(-(w0_ref[...] + _mm(jnp.tanh(w_lo), w2_ref[...]))) - 0.5
    ld = -jnp.exp(w)
    a = jax.nn.sigmoid(a0_ref[...] + _mm(a_lo, a2_ref[...]))
    g = _mm(jax.nn.sigmoid(g_lo), g2_ref[...])
    kkv = k * kk_ref[...]
    k = k * (1.0 + (a - 1.0) * ka_ref[...])
    lc = _cumsum_rows(ld)
    e_in = jnp.exp(lc)
    e_in_neg = jnp.exp(-lc)
    e_ex = jnp.exp(lc - ld)
    bonus = r * k * rk_ref[...]

    r2 = lax.broadcasted_iota(jnp.int32, (2 * c, 2 * c), 0)
    c2 = lax.broadcasted_iota(jnp.int32, (2 * c, 2 * c), 1)
    tq = jnp.where(r2 >= c, r2 - c, r2)
    ts = jnp.where(c2 >= c, c2 - c, c2)
    keep = (tq > ts) | ((r2 >= c) & (tq == ts))
    heads = range(nh)
    sls = [slice(h * hd, (h + 1) * hd) for h in heads]
    lhs, rhs, vs, s0s = [], [], [], []
    for h in heads:
        sl = sls[h]
        kkh = kkv[:, sl]
        kap = kkh * lax.rsqrt(jnp.sum(kkh * kkh, axis=-1, keepdims=True) + 1e-6)
        lhs.append(jnp.concatenate([kap * e_ex[:, sl], r[:, sl] * e_in[:, sl]], axis=0))
        rhs.append(jnp.concatenate([kap * a[:, sl] * e_in_neg[:, sl], k[:, sl] * e_in_neg[:, sl]], axis=0))
        vs.append(v[:, sl])
        s0s.append(s_scr[h])
    pm = [jnp.where(keep, _mm_nt(lhs[h], rhs[h]), 0.0) for h in heads]
    qs = [_mm_nt(lhs[h], s0s[h]) for h in heads]
    zv = jnp.zeros((c, hd), F32)
    mkv = [_mm(pm[h][:c, :], jnp.concatenate([zv, vs[h]], axis=0)) for h in heads]
    t_inv = _tri_inv([pm[h][:c, :c] for h in heads], c)
    us = [-_mm(t_inv[h], qs[h][:c, :] + mkv[h]) for h in heads]
    uv = [jnp.concatenate([us[h], vs[h]], axis=0) for h in heads]
    ys = [qs[h][c:, :] + _mm(pm[h][c:, :], uv[h]) for h in heads]
    sn = [(s0s[h] + _mm_tn(uv[h], rhs[h])) * e_in[c - 1:c, sls[h]] for h in heads]
    for h in heads:
        sl = sls[h]
        s_scr[h] = sn[h]
        y = ys[h]
        yc = y - jnp.mean(y, axis=-1, keepdims=True)
        yn = yc * lax.rsqrt(jnp.mean(yc * yc, axis=-1, keepdims=True) + RWKV_GN_EPS)
        yn = yn * lnw_ref[:, sl] + lnb_ref[:, sl]
        yn = yn + jnp.sum(bonus[:, sl], axis=-1, keepdims=True) * vs[h]
        y_ref[:, sl] = (yn * g[:, sl]).astype(y_ref.dtype)

    @pl.when(n == nc - 1)
    def _():
        so_ref[0] = s_scr[...]


def _rwkv(z, row0, bsz, length, c, p, shift, state):
    nc = length // c
    rb = _row_block(row0, c, nc)
    has_state = state is not None
    pad = MIX_W - RWKV_COLS
    c2 = lambda b, n: (0, 0)
    args = [z, jnp.pad(p["mu"], (0, pad)).reshape(1, MIX_W),
            p["w0"].reshape(1, D_MIX), p["w2"], p["a0"].reshape(1, D_MIX), p["a2"], p["g2"],
            p["k_k"].reshape(1, D_MIX), p["k_a"].reshape(1, D_MIX), p["r_k"].reshape(1, D_MIX),
            p["ln_w"].reshape(1, D_MIX), p["ln_b"].reshape(1, D_MIX)]
    specs = [pl.BlockSpec((c, MIX_W), lambda b, n: (rb(b, n), Z_RWKV // MIX_W))]
    specs += [pl.BlockSpec(a.shape, c2) for a in args[1:]]
    sshape = (RWKV_HEADS, RWKV_N, RWKV_N)
    if has_state:
        args += [jnp.pad(shift, ((0, 0), (0, pad))).reshape(bsz, 1, MIX_W), state]
        specs += [pl.BlockSpec((1, 1, MIX_W), lambda b, n: (b, 0, 0)),
                  pl.BlockSpec((1,) + sshape, lambda b, n: (b, 0, 0, 0))]
    return pl.pallas_call(
        functools.partial(_rwkv_kernel, c=c, nc=nc, has_state=has_state),
        grid=(bsz, nc), in_specs=specs,
        out_specs=[pl.BlockSpec((c, D_MIX), lambda b, n: (b * nc + n, 0)),
                   pl.BlockSpec((1,) + sshape, lambda b, n: (b, 0, 0, 0))],
        out_shape=[jax.ShapeDtypeStruct((bsz * length, D_MIX), BF16),
                   jax.ShapeDtypeStruct((bsz,) + sshape, F32)],
        scratch_shapes=[pltpu.VMEM(sshape, F32), pltpu.VMEM((8, MIX_W), F32)],
        compiler_params=_cparams(("arbitrary", "arbitrary")), name="rwkv7",
    )(*args)


CONV_PAD = 8


def _gdn_kernel(*refs, c, nc, has_state):
    it = iter(refs)
    z_ref, ab_ref, cw_ref, alog_ref, dtb_ref, gn_ref = (next(it) for _ in range(6))
    if has_state:
        cv0_ref, s0_ref = next(it), next(it)
    y_ref, so_ref, s_scr, xp_scr = next(it), next(it), next(it), next(it)
    n = pl.program_id(1)
    qkv_w = 3 * D_MIX
    tail = GDN_CONV - 1

    @pl.when(n == 0)
    def _():
        s_scr[...] = s0_ref[0] if has_state else jnp.zeros(s_scr.shape, F32)
        xp_scr[0:CONV_PAD, :] = jnp.zeros((CONV_PAD, qkv_w), F32)
        if has_state:
            xp_scr[CONV_PAD - tail:CONV_PAD, :] = cv0_ref[0]

    xp_scr[CONV_PAD:CONV_PAD + c, :] = z_ref[:, :qkv_w]
    conv = xp_scr[CONV_PAD - tail:CONV_PAD - tail + c, :] * cw_ref[0:1, :]
    for j in range(1, GDN_CONV):
        conv = conv + xp_scr[CONV_PAD - tail + j:CONV_PAD - tail + j + c, :] * cw_ref[j:j + 1, :]
    xp_scr[CONV_PAD - tail:CONV_PAD, :] = xp_scr[CONV_PAD + c - tail:CONV_PAD + c, :]
    act = _silu(conv)

    ab = ab_ref[...]
    gdec = -jnp.exp(alog_ref[...]) * _softplus(ab + dtb_ref[...])
    gcum = _cumsum_rows(gdec)
    beta = jax.nn.sigmoid(ab)

    r2 = lax.broadcasted_iota(jnp.int32, (c, c), 0)
    c2 = lax.broadcasted_iota(jnp.int32, (c, c), 1)
    eye, causal, strict = r2 == c2, r2 >= c2, r2 > c2
    heads = range(N_HEADS)
    kq, ks, vs, ss, gcols, bcols, decs = [], [], [], [], [], [], []
    for h in heads:
        lo = h * HEAD_DIM
        q = act[:, lo:lo + HEAD_DIM]
        k = act[:, D_MIX + lo:D_MIX + lo + HEAD_DIM]
        q = q * lax.rsqrt(jnp.sum(q * q, axis=-1, keepdims=True) + 1e-6) * HEAD_DIM ** -0.5
        k = k * lax.rsqrt(jnp.sum(k * k, axis=-1, keepdims=True) + 1e-6)
        gcol = gcum[:, h:h + 1]
        rel = gcol - _row_from_col(gcol, eye)
        kq.append(jnp.concatenate([k, q], axis=0))
        ks.append(k)
        vs.append(act[:, 2 * D_MIX + lo:2 * D_MIX + lo + HEAD_DIM])
        ss.append(s_scr[h])
        gcols.append(gcol)
        bcols.append(beta[:, N_HEADS + h:N_HEADS + h + 1])
        decs.append(jnp.where(causal, jnp.exp(jnp.where(causal, rel, 0.0)), 0.0))
    kk_qk = [_mm_nt(kq[h], ks[h]) for h in heads]
    ks_qs = [_mm(kq[h], ss[h]) for h in heads]
    t_inv = _tri_inv([jnp.where(strict, bcols[h] * kk_qk[h][:c] * decs[h], 0.0) for h in heads], c)
    egs = [jnp.exp(gcols[h]) for h in heads]
    us = [_mm(t_inv[h], bcols[h] * (vs[h] - egs[h] * ks_qs[h][:c])) for h in heads]
    os_ = [egs[h] * ks_qs[h][c:] + _mm(kk_qk[h][c:] * decs[h], us[h]) for h in heads]
    for h in heads:
        lo = h * HEAD_DIM
        gl = gcols[h][c - 1:c, :]
        s_scr[h] = jnp.exp(gl) * ss[h] + _mm_tn(ks[h] * jnp.exp(gl - gcols[h]), us[h])
        zg = z_ref[:, qkv_w + lo:qkv_w + lo + HEAD_DIM]
        y = _rms_rows(os_[h], gn_ref[...]) * _silu(zg)
        y_ref[:, lo:lo + HEAD_DIM] = y.astype(y_ref.dtype)

    @pl.when(n == nc - 1)
    def _():
        so_ref[0] = s_scr[...]


def _gdn(z, row0, bsz, length, c, p, conv_state, state):
    nc = length // c
    rb = _row_block(row0, c, nc)
    has_state = state is not None
    c2 = lambda b, n: (0, 0)
    lane_pad = lambda a: jnp.pad(a, (0, HEAD_DIM - a.shape[0])).reshape(1, HEAD_DIM)
    args = [z, z, p["conv_w"], lane_pad(p["A_log"]), lane_pad(p["dt_bias"]), p["norm_g"].reshape(1, HEAD_DIM)]
    specs = [pl.BlockSpec((c, MIX_W), lambda b, n: (rb(b, n), Z_GDN // MIX_W)),
             pl.BlockSpec((c, HEAD_DIM), lambda b, n: (rb(b, n), Z_AB // HEAD_DIM))]
    specs += [pl.BlockSpec(a.shape, c2) for a in args[2:]]
    sshape = (N_HEADS, HEAD_DIM, HEAD_DIM)
    if has_state:
        args += [conv_state, state]
        specs += [pl.BlockSpec((1, GDN_CONV - 1, 3 * D_MIX), lambda b, n: (b, 0, 0)),
                  pl.BlockSpec((1,) + sshape, lambda b, n: (b, 0, 0, 0))]
    return pl.pallas_call(
        functools.partial(_gdn_kernel, c=c, nc=nc, has_state=has_state),
        grid=(bsz, nc), in_specs=specs,
        out_specs=[pl.BlockSpec((c, D_MIX), lambda b, n: (b * nc + n, 0)),
                   pl.BlockSpec((1,) + sshape, lambda b, n: (b, 0, 0, 0))],
        out_shape=[jax.ShapeDtypeStruct((bsz * length, D_MIX), BF16),
                   jax.ShapeDtypeStruct((bsz,) + sshape, F32)],
        scratch_shapes=[pltpu.VMEM(sshape, F32), pltpu.VMEM((CONV_PAD + c, 3 * D_MIX), F32)],
        compiler_params=_cparams(("arbitrary", "arbitrary")), name="gdn",
    )(*args)


def _merge_kernel(y0_ref, y1_ref, y2_ref, y3_ref, gate_ref, x_ref, gt_ref, sc_ref, sh_ref, g_ref,
                  wb_ref, wo_ref, rw_ref, rb_ref, xo_ref, h_ref, lg_ref, *, gpt):
    tm, d = x_ref.shape
    merged = jnp.zeros((tm, d), F32)
    for nb, y_ref in enumerate((y0_ref, y1_ref, y2_ref, y3_ref)):
        br = jnp.dot(y_ref[...], wb_ref[0, nb], preferred_element_type=F32)
        merged = merged + gate_ref[:, nb * d:(nb + 1) * d].astype(F32) * br
    m = jnp.dot(merged.astype(BF16), wo_ref[0], preferred_element_type=F32)
    gt = gt_ref[0]
    x = x_ref[...] + (m.reshape(gpt, GROUP, d) * gt[:, None, :]).reshape(tm, d)
    xo_ref[...] = x
    h = _modulate(_rms_rows(x, g_ref[...]), sc_ref[0], sh_ref[0], gpt)
    h_ref[...] = h
    lg_ref[...] = jnp.dot(h, rw_ref[0], precision=lax.Precision.HIGHEST,
                          preferred_element_type=F32) + rb_ref[0]


def _merge(ys, gates, x, modg, l, g2, wb, wo, rw, rb, n_prompt):
    t, d = x.shape
    tm = 256
    gpt, gidx = _group_block(tm, n_prompt // tm)
    row = lambda w: pl.BlockSpec((tm, w), lambda i: (i, 0))

    def mod_spec(col):
        return pl.BlockSpec((1, gpt, d), lambda i: (l, gidx(i), col))

    def const(shape):
        return pl.BlockSpec(shape, lambda i: (0,) * len(shape), pipeline_mode=pl.Buffered(1))

    def layer(arr):
        return pl.BlockSpec((1,) + arr.shape[1:], lambda i: (l,) + (0,) * (arr.ndim - 1),
                            pipeline_mode=pl.Buffered(1))
    nr = rw.shape[2]
    return pl.pallas_call(
        functools.partial(_merge_kernel, gpt=gpt),
        grid=(t // tm,),
        in_specs=[row(D_MIX)] * 4 + [row(N_BRANCH * d), row(d), mod_spec(2), mod_spec(4), mod_spec(3),
                                     const((1, d)), layer(wb), layer(wo), layer(rw), layer(rb)],
        out_specs=[row(d), row(d), row(nr)],
        out_shape=[jax.ShapeDtypeStruct((t, d), F32), jax.ShapeDtypeStruct((t, d), F32),
                   jax.ShapeDtypeStruct((t, nr), F32)],
        compiler_params=_cparams(("arbitrary",)), name="merge",
    )(*ys, gates, x, modg, modg, modg, g2.reshape(1, d), wb, wo, rw, rb)


MOE_BLK = 256
MOE_UNROLL = 8


def _moe_kernel(be_ref, tok_ref, slot_ref, nv_ref, nused_ref, h_hbm, w_ref, wg_ref, wu_ref, wd_ref, o_hbm,
                xbuf, ybuf, wg_bf, wu_bf, wd_bf, gsem, ssem):
    i = pl.program_id(0)
    nused = nused_ref[0]
    p = i % 2

    def gather(blk, buf, r):
        return pltpu.make_async_copy(h_hbm.at[pl.ds(tok_ref[blk * MOE_BLK + r], 1)],
                                     xbuf.at[buf, pl.ds(r, 1)], gsem.at[buf])

    def scatter(blk, buf, r):
        return pltpu.make_async_copy(ybuf.at[buf, pl.ds(r, 1)],
                                     o_hbm.at[pl.ds(slot_ref[blk * MOE_BLK + r], 1)], ssem.at[buf])

    def for_rows(n_rows, fn):
        def group(j, carry):
            for u in range(MOE_UNROLL):
                fn(j * MOE_UNROLL + u)
            return carry

        def single(r, carry):
            fn(r)
            return carry
        full = n_rows // MOE_UNROLL
        lax.fori_loop(0, full, group, 0)
        if not isinstance(n_rows, int):
            lax.fori_loop(full * MOE_UNROLL, n_rows, single, 0)

    @pl.when(i < nused)
    def _():
        @pl.when(i == 0)
        def _():
            for_rows(MOE_BLK, lambda r: gather(i, p, r).start())
        for_rows(MOE_BLK, lambda r: gather(i, p, r).wait())

        @pl.when(i + 1 < nused)
        def _():
            for_rows(MOE_BLK, lambda r: gather(i + 1, 1 - p, r).start())

        @pl.when(i > 0)
        def _():
            for_rows(nv_ref[i - 1], lambda r: scatter(i - 1, 1 - p, r).wait())

        @pl.when((i == 0) | (be_ref[i] != be_ref[jnp.maximum(i - 1, 0)]))
        def _():
            wg_bf[...] = wg_ref[0, 0].astype(BF16)
            wu_bf[...] = wu_ref[0, 0].astype(BF16)
            wd_bf[...] = wd_ref[0, 0].astype(BF16)

        x = xbuf[p].astype(BF16)
        hid = _silu(jnp.dot(x, wg_bf[...], preferred_element_type=F32)) * jnp.dot(
            x, wu_bf[...], preferred_element_type=F32)
        ybuf[p] = jnp.dot(hid.astype(BF16), wd_bf[...], preferred_element_type=F32) * w_ref[...]
        for_rows(nv_ref[i], lambda r: scatter(i, p, r).start())

        @pl.when(i == nused - 1)
        def _():
            for_rows(nv_ref[i], lambda r: scatter(i, p, r).wait())


def _route(logits, n_tok):
    lg = logits[:, :N_GROUPS]
    pg = jax.nn.softmax(lg, axis=-1)
    gsel = jnp.argmax(lg, axis=-1).astype(jnp.int32)
    gw = jnp.take_along_axis(pg, gsel[:, None], axis=-1)
    le = logits[:, N_GROUPS:N_GROUPS + N_EXPERTS].reshape(n_tok, N_GROUPS, EXPERTS_PER_GROUP)
    le = jnp.take_along_axis(le, gsel[:, None, None], axis=1)[:, 0]
    top_v, top_i = lax.top_k(jax.nn.softmax(le, axis=-1), TOPK)
    wts = top_v / jnp.sum(top_v, axis=-1, keepdims=True) * gw
    eid = (gsel[:, None] * EXPERTS_PER_GROUP + top_i).reshape(-1).astype(jnp.int32)
    a = n_tok * TOPK
    order = jnp.argsort(eid).astype(jnp.int32)
    counts = jnp.sum((eid[:, None] == jnp.arange(N_EXPERTS, dtype=jnp.int32)[None, :]).astype(jnp.int32), axis=0)
    padded = (counts + MOE_BLK - 1) // MOE_BLK * MOE_BLK
    pend = jnp.cumsum(padded)
    pstart = pend - padded
    cstart = jnp.cumsum(counts) - counts
    n_blocks = -(-a // MOE_BLK) + N_EXPERTS
    rows = n_blocks * MOE_BLK
    blk_row0 = jnp.arange(n_blocks, dtype=jnp.int32) * MOE_BLK
    block_e = jnp.minimum(jnp.sum((pend[None, :] <= blk_row0[:, None]).astype(jnp.int32), axis=1),
                          N_EXPERTS - 1).astype(jnp.int32)
    nused = (pend[-1:] // MOE_BLK).astype(jnp.int32)
    nvalid = jnp.clip((pstart + counts)[block_e] - blk_row0, 0, MOE_BLK).astype(jnp.int32)
    e_row = jnp.repeat(block_e, MOE_BLK)
    j_row = jnp.arange(rows, dtype=jnp.int32) - pstart[e_row]
    valid = j_row < counts[e_row]
    asg = order[jnp.clip(cstart[e_row] + j_row, 0, a - 1)]
    tok = jnp.where(valid, asg // TOPK, 0)
    slot = jnp.where(valid, (asg % TOPK) * n_tok + asg // TOPK, 0)
    wrow = jnp.where(valid, wts.reshape(-1)[asg], 0.0)
    return block_e, tok, slot, nvalid, nused, wrow.reshape(rows, 1), n_blocks


def _moe(h, logits, wg, wu, wd, l):
    t, d = h.shape
    block_e, tok, slot, nvalid, nused, wrow, n_blocks = _route(logits, t)
    grid_spec = pltpu.PrefetchScalarGridSpec(
        num_scalar_prefetch=5, grid=(n_blocks,),
        in_specs=[pl.BlockSpec(memory_space=pl.ANY),
                  pl.BlockSpec((MOE_BLK, 1), lambda i, be, *_: (i, 0)),
                  pl.BlockSpec((1, 1, d, D_EXPERT), lambda i, be, *_: (l, be[i], 0, 0)),
                  pl.BlockSpec((1, 1, d, D_EXPERT), lambda i, be, *_: (l, be[i], 0, 0)),
                  pl.BlockSpec((1, 1, D_EXPERT, d), lambda i, be, *_: (l, be[i], 0, 0))],
        out_specs=pl.BlockSpec(memory_space=pl.ANY),
        scratch_shapes=[pltpu.VMEM((2, MOE_BLK, d), F32), pltpu.VMEM((2, MOE_BLK, d), F32),
                        pltpu.VMEM((d, D_EXPERT), BF16), pltpu.VMEM((d, D_EXPERT), BF16),
                        pltpu.VMEM((D_EXPERT, d), BF16),
                        pltpu.SemaphoreType.DMA((2,)), pltpu.SemaphoreType.DMA((2,))])
    out = pl.pallas_call(
        _moe_kernel, grid_spec=grid_spec,
        out_shape=jax.ShapeDtypeStruct((t * TOPK, d), F32),
        compiler_params=_cparams(("arbitrary",)), name="moe",
    )(block_e, tok, slot, nvalid, nused, h, wrow, wg, wu, wd)
    return out


def _pack_w_in(w_in):
    depth, d, _ = w_in.shape
    o_rwkv, o_gdn = 4096, 4096 + RWKV_COLS
    o_ab = o_gdn + 4 * D_MIX
    o_gate = o_ab + 2 * N_HEADS
    zeros = lambda n: jnp.zeros((depth, d, n), BF16)
    wb = w_in.astype(BF16)
    w_main = jnp.concatenate(
        [wb[:, :, :o_gdn], zeros(Z_GDN - o_gdn), wb[:, :, o_gdn:o_ab], wb[:, :, o_ab:o_gate],
         zeros(N_MAIN - Z_AB - 2 * N_HEADS)], axis=2)
    return w_main, wb[:, :, o_gate:]


def _mixers(z, l, groups, params):
    outs = [[] for _ in range(N_BRANCH)]
    states = []
    for (row0, bsz, length, c, pos0, st) in groups:
        s_ret, s_hgrn, s_rwkv, s_shift, s_gdn, s_conv = st if st is not None else (None,) * 6
        y_a, n_ret = _retention(z, row0, bsz, length, c, pos0, s_ret)
        y_b, n_hgrn = _hgrn(z, row0, bsz, length, c, params["hgrn_lb"][l], params["hgrn_norm_g"][l], s_hgrn)
        y_c, n_rwkv = _rwkv(z, row0, bsz, length, c, {k: v[l] for k, v in params["rwkv"].items()},
                            s_shift, s_rwkv)
        y_d, n_gdn = _gdn(z, row0, bsz, length, c, {k: v[l] for k, v in params["gdn"].items()},
                          s_conv, s_gdn)
        for lst, y in zip(outs, (y_a, y_b, y_c, y_d)):
            lst.append(y)
        def seq_row(j, col0, width):
            return lax.slice(z, (row0 + j, col0), (row0 + (bsz - 1) * length + j + 1, col0 + width),
                             (length, 1))
        n_shift = seq_row(length - 1, Z_RWKV, RWKV_COLS)
        n_conv = jnp.stack([seq_row(length - (GDN_CONV - 1) + j, Z_GDN, 3 * D_MIX)
                            for j in range(GDN_CONV - 1)], axis=1)
        states.append((n_ret, n_hgrn, n_rwkv, n_shift, n_gdn, n_conv))
    return [jnp.concatenate(lst, axis=0) for lst in outs], states


def kernel(x_prompt, x_sample, c_prompt, c_sample, state_ret, state_hgrn, state_rwkv, state_rwkv_shift,
           state_gdn, state_gdn_conv, ada_w, ada_b, norm1_g, norm2_g, w_in, hgrn_lb_logits, hgrn_norm_g,
           rwkv_mu, rwkv_w0, rwkv_w2, rwkv_a0, rwkv_a2, rwkv_g2, rwkv_k_k, rwkv_k_a, rwkv_r_k, rwkv_ln_w,
           rwkv_ln_b, gdn_conv_w, gdn_A_log, gdn_dt_bias, gdn_norm_g, w_branch, w_out, router_g, router_g_b,
           router_e, router_e_b, moe_w_gate, moe_w_up, moe_w_down, final_norm_g):
    depth = ada_w.shape[0]
    bp, lp, d = x_prompt.shape
    bs, ls, _ = x_sample.shape
    n_prompt = bp * lp
    assert bp == 1 and ls == GROUP and bs == 16

    lb_cum = jnp.cumsum(jax.nn.softmax(hgrn_lb_logits.astype(F32), axis=0), axis=0)
    params = dict(
        hgrn_lb=lb_cum - lb_cum[:1], hgrn_norm_g=hgrn_norm_g,
        rwkv=dict(mu=rwkv_mu, w0=rwkv_w0, w2=rwkv_w2, a0=rwkv_a0, a2=rwkv_a2, g2=rwkv_g2, k_k=rwkv_k_k,
                  k_a=rwkv_k_a, r_k=rwkv_r_k.reshape(depth, D_MIX), ln_w=rwkv_ln_w, ln_b=rwkv_ln_b),
        gdn=dict(conv_w=gdn_conv_w, A_log=gdn_A_log, dt_bias=gdn_dt_bias, norm_g=gdn_norm_g))

    c_all = jnp.concatenate([c_prompt, c_sample, jnp.zeros((24 - bp - bs, d), F32)], axis=0)
    mod = _ada(c_all, ada_w, ada_b)
    modg = jnp.concatenate([jnp.broadcast_to(mod[:, :1], (depth, 16, 6 * d)), mod[:, 1:1 + bs]], axis=1)

    w_main, w_gate = _pack_w_in(w_in)
    wb_bf, wo_bf = w_branch.astype(BF16), w_out.astype(BF16)
    n_r = 128
    r_w = jnp.concatenate([router_g, router_e, jnp.zeros((depth, d, n_r - N_GROUPS - N_EXPERTS), F32)], axis=2)
    r_b = jnp.concatenate([router_g_b, router_e_b, jnp.zeros((depth, n_r - N_GROUPS - N_EXPERTS), F32)],
                          axis=1).reshape(depth, 1, n_r)

    x = jnp.concatenate([x_prompt.reshape(n_prompt, d), x_sample.reshape(bs * ls, d)], axis=0)
    moe_out = None
    new_p, new_s = [], []
    for l in range(depth):
        x, h = _norm(x, moe_out, modg, l, norm1_g[l], n_prompt, final=False)
        z = _proj(h, w_main, l, tn=768, sigmoid=False, out_dtype=F32)
        gates = _proj(h, w_gate, l, tn=1024, sigmoid=True, out_dtype=BF16)
        groups = [(0, bp, lp, 64, 0.0, None),
                  (n_prompt, bs, ls, ls, float(PAST_LEN),
                   (state_ret[l], state_hgrn[l], state_rwkv[l], state_rwkv_shift[l], state_gdn[l],
                    state_gdn_conv[l]))]
        ys, (st_p, st_s) = _mixers(z, l, groups, params)
        new_p.append(st_p)
        new_s.append(st_s)
        x, h2, logits = _merge(ys, gates, x, modg, l, norm2_g[l], wb_bf, wo_bf, r_w, r_b, n_prompt)
        moe_out = _moe(h2, logits, moe_w_gate, moe_w_up, moe_w_down, l)
    y = _norm(x, moe_out, modg, depth - 1, final_norm_g, n_prompt, final=True)

    def stack(lst, i):
        return jnp.stack([s[i] for s in lst]).astype(F32)
    return ((y[:n_prompt].reshape(bp, lp, d), y[n_prompt:].reshape(bs, ls, d))
            + tuple(stack(new_p, i) for i in range(6)) + tuple(stack(new_s, i) for i in range(6)))
```

```python
import functools
import math

import numpy as np
import jax
import jax.numpy as jnp
from jax import lax
from jax.experimental import pallas as pl
from jax.experimental.pallas import tpu as pltpu

F32 = jnp.float32
BF16 = jnp.bfloat16

D_MODEL = 2048
D_MIX = 512
HEAD_DIM = 128
N_HEADS = 4
RWKV_N = 64
RWKV_HEADS = 8
RWKV_COLS = 1792
GDN_CONV = 4
N_BRANCH = 4
N_GROUPS = 4
EXPERTS_PER_GROUP = 8
N_EXPERTS = 32
TOPK = 2
D_EXPERT = 512
PAST_LEN = 4096
ROPE_BASE = 10000.0
NORM_EPS = 1e-6
GN_EPS = 1e-6
RWKV_GN_EPS = 64e-5
RET_EXP_LO, RET_EXP_HI = 5.0, 12.0

Z_RET, Z_HGRN, Z_RWKV, Z_GDN, Z_AB = 0, 2048, 4096, 6144, 8192
N_MAIN = 8448
MIX_W = 2048
GROUP = 32

VMEM_LIMIT = 56 * 1024 * 1024


def _cparams(sem):
    return pltpu.CompilerParams(dimension_semantics=sem, vmem_limit_bytes=VMEM_LIMIT)


def _mm(a, b):
    return jnp.dot(a.astype(BF16), b.astype(BF16), preferred_element_type=F32)


def _mm_nt(a, b):
    return lax.dot_general(a.astype(BF16), b.astype(BF16), (((1,), (1,)), ((), ())),
                           preferred_element_type=F32)


def _mm_tn(a, b):
    return lax.dot_general(a.astype(BF16), b.astype(BF16), (((0,), (0,)), ((), ())),
                           preferred_element_type=F32)


def _silu(x):
    return x * jax.nn.sigmoid(x)


def _softplus(x):
    return jnp.maximum(x, 0.0) + jnp.log1p(jnp.exp(-jnp.abs(x)))


def _cumsum_rows(x):
    n = x.shape[0]
    row = lax.broadcasted_iota(jnp.int32, x.shape, 0)
    s = 1
    while s < n:
        x = x + jnp.where(row >= s, pltpu.roll(x, s, 0), 0.0)
        s *= 2
    return x


def _row_from_col(col, eye):
    return jnp.sum(jnp.where(eye, col, 0.0), axis=0, keepdims=True)


def _tri_inv(n_mats, c):
    r = lax.broadcasted_iota(jnp.int32, (c, c), 0)
    col = lax.broadcasted_iota(jnp.int32, (c, c), 1)
    eye = jnp.where(r == col, 1.0, 0.0)
    pair = (r >> 1) == (col >> 1)
    xs = [eye - jnp.where(pair, n, 0.0) for n in n_mats]
    m, sh = 2, 1
    while m < c:
        lvl = ((r >> (sh + 1)) == (col >> (sh + 1))) & ((r >> sh) != (col >> sh))
        ts = [_mm(jnp.where(lvl, n, 0.0), x) for n, x in zip(n_mats, xs)]
        xs = [x - _mm(x, t) for x, t in zip(xs, ts)]
        m, sh = m * 2, sh + 1
    return xs


def _ada_kernel(c_ref, w_ref, b_ref, o_ref):
    cm = _silu(c_ref[...])
    o_ref[0] = _mm(cm, w_ref[0]) + b_ref[0]


def _ada(c_all, ada_w, ada_b):
    depth, d, n = ada_w.shape
    rows = c_all.shape[0]
    tn = 1024
    return pl.pallas_call(
        _ada_kernel,
        grid=(depth, n // tn),
        in_specs=[pl.BlockSpec((rows, d), lambda l, j: (0, 0)),
                  pl.BlockSpec((1, d, tn), lambda l, j: (l, 0, j)),
                  pl.BlockSpec((1, 1, tn), lambda l, j: (l, 0, j))],
        out_specs=pl.BlockSpec((1, rows, tn), lambda l, j: (l, 0, j)),
        out_shape=jax.ShapeDtypeStruct((depth, rows, n), F32),
        compiler_params=_cparams(("arbitrary", "arbitrary")),
        name="ada",
    )(c_all, ada_w, ada_b.reshape(depth, 1, n))


def _group_block(tm, n_prompt_tiles):
    gpt = tm // GROUP
    first_sample = 16 // gpt

    def idx(i):
        return jnp.where(i < n_prompt_tiles, 0, first_sample + i - n_prompt_tiles)
    return gpt, idx


def _modulate(y, sc, sh, gpt):
    tm, d = y.shape
    y3 = y.reshape(gpt, GROUP, d)
    return (y3 * (1.0 + sc[:, None, :]) + sh[:, None, :]).reshape(tm, d)


def _rms_rows(x, g):
    return x * lax.rsqrt(jnp.mean(x * x, axis=-1, keepdims=True) + NORM_EPS) * g


def _norm_kernel(*refs, gpt, with_moe, modulated):
    it = iter(refs)
    x_ref = next(it)
    if with_moe:
        m0_ref, m1_ref, gt_ref = next(it), next(it), next(it)
    g_ref = next(it)
    if modulated:
        sc_ref, sh_ref = next(it), next(it)
    x = x_ref[...]
    tm, d = x.shape
    if with_moe:
        moe = m0_ref[...] + m1_ref[...]
        gt = gt_ref[0]
        x = x + (moe.reshape(gpt, GROUP, d) * gt[:, None, :]).reshape(tm, d)
    y = _rms_rows(x, g_ref[...])
    if modulated:
        xo_ref, h_ref = next(it), next(it)
        xo_ref[...] = x
        h_ref[...] = _modulate(y, sc_ref[0], sh_ref[0], gpt).astype(BF16)
    else:
        y_ref = next(it)
        y_ref[...] = y


def _norm(x, moe_out, modg, l, g, n_prompt, *, final):
    t, d = x.shape
    tm = 256
    gpt, gidx = _group_block(tm, n_prompt // tm)
    with_moe = moe_out is not None
    row = pl.BlockSpec((tm, d), lambda i: (i, 0))

    def mod_spec(col, layer):
        return pl.BlockSpec((1, gpt, d), lambda i: (layer, gidx(i), col))
    args, specs = [x], [row]
    if with_moe:
        lm = l if final else l - 1
        args += [moe_out, moe_out, modg]
        specs += [row, pl.BlockSpec((tm, d), lambda i: (t // tm + i, 0)), mod_spec(5, lm)]
    args.append(g.reshape(1, d))
    specs.append(pl.BlockSpec((1, d), lambda i: (0, 0)))
    if not final:
        args += [modg, modg]
        specs += [mod_spec(1, l), mod_spec(0, l)]
        out_shape = [jax.ShapeDtypeStruct((t, d), F32), jax.ShapeDtypeStruct((t, d), BF16)]
        out_specs = [row, row]
    else:
        out_shape = jax.ShapeDtypeStruct((t, d), F32)
        out_specs = row
    return pl.pallas_call(
        functools.partial(_norm_kernel, gpt=gpt, with_moe=with_moe, modulated=not final),
        grid=(t // tm,), in_specs=specs, out_specs=out_specs, out_shape=out_shape,
        compiler_params=_cparams(("arbitrary",)), name="norm",
    )(*args)


def _proj_kernel(a_ref, w_ref, o_ref, *, sigmoid):
    acc = jnp.dot(a_ref[...], w_ref[0], preferred_element_type=F32)
    if sigmoid:
        acc = jax.nn.sigmoid(acc)
    o_ref[...] = acc.astype(o_ref.dtype)


def _proj(a, w, l, *, tn, sigmoid, out_dtype):
    t, k = a.shape
    n = w.shape[2]
    tm = t // 4
    return pl.pallas_call(
        functools.partial(_proj_kernel, sigmoid=sigmoid),
        grid=(n // tn, t // tm),
        in_specs=[pl.BlockSpec((tm, k), lambda j, i: (i, 0)),
                  pl.BlockSpec((1, k, tn), lambda j, i: (l, 0, j))],
        out_specs=pl.BlockSpec((tm, tn), lambda j, i: (i, j)),
        out_shape=jax.ShapeDtypeStruct((t, n), out_dtype),
        compiler_params=_cparams(("arbitrary", "arbitrary")), name="proj",
    )(a, w)


def _ret_tables(c):
    e = np.linspace(RET_EXP_LO, RET_EXP_HI, N_HEADS)
    lg = np.log1p(-np.exp2(-e))
    t = np.arange(c, dtype=np.float64)
    rel = t[:, None] - t[None, :]
    d_intra = np.where(rel >= 0, np.exp(lg[:, None, None] * np.where(rel >= 0, rel, 0.0)), 0.0)
    d_q = np.exp(lg[:, None] * (t + 1.0))[:, :, None] * np.ones((1, 1, HEAD_DIM))
    d_k = np.exp(lg[:, None] * (c - 1.0 - t))[:, :, None] * np.ones((1, 1, HEAD_DIM))
    d_s = np.exp(lg * c)[:, None, None] * np.ones((1, 8, HEAD_DIM))
    return tuple(jnp.asarray(a, F32) for a in (d_intra, d_q, d_k, d_s))


def _ret_kernel(*refs, c, nc, has_state):
    it = iter(refs)
    z_ref, cos_ref, sin_ref, di_ref, dq_ref, dk_ref, ds_ref = (next(it) for _ in range(7))
    s0_ref = next(it) if has_state else None
    y_ref, so_ref, s_scr = next(it), next(it), next(it)
    n = pl.program_id(1)

    @pl.when(n == 0)
    def _():
        s_scr[...] = s0_ref[0] if has_state else jnp.zeros(s_scr.shape, F32)

    cos, sin = cos_ref[...], sin_ref[...]
    half = HEAD_DIM // 2
    for h in range(N_HEADS):
        lo = h * HEAD_DIM
        q = z_ref[:, lo:lo + HEAD_DIM]
        k = z_ref[:, D_MIX + lo:D_MIX + lo + HEAD_DIM]
        v = z_ref[:, 2 * D_MIX + lo:2 * D_MIX + lo + HEAD_DIM]
        g = z_ref[:, 3 * D_MIX + lo:3 * D_MIX + lo + HEAD_DIM]
        q = q * cos + pltpu.roll(q, half, 1) * sin
        k = (k * cos + pltpu.roll(k, half, 1) * sin) * HEAD_DIM ** -0.5
        s = s_scr[h]
        att = _mm_nt(q, k) * di_ref[h]
        o = _mm(att, v) + _mm(q * dq_ref[h], s)
        s_scr[h] = s * ds_ref[h, 0:1, :] + _mm_tn(k * dk_ref[h], v)
        xc = o - jnp.mean(o, axis=-1, keepdims=True)
        on = xc * lax.rsqrt(jnp.mean(xc * xc, axis=-1, keepdims=True) + GN_EPS)
        y_ref[:, lo:lo + HEAD_DIM] = (on * _silu(g)).astype(y_ref.dtype)

    @pl.when(n == nc - 1)
    def _():
        so_ref[0] = s_scr[...]


def _rope_tables(length, pos0):
    half = HEAD_DIM // 2
    inv = ROPE_BASE ** (-jnp.arange(half, dtype=F32) / half)
    pos = jnp.arange(length, dtype=F32) + pos0
    ang = pos[:, None] * inv[None, :]
    cos, sin = jnp.cos(ang), jnp.sin(ang)
    return jnp.concatenate([cos, cos], axis=1), jnp.concatenate([-sin, sin], axis=1)


def _row_block(row0, c, nc):
    off = row0 // c
    return lambda b, n: off + b * nc + n


def _retention(z, row0, bsz, length, c, pos0, state):
    nc = length // c
    rb = _row_block(row0, c, nc)
    cos, sin = _rope_tables(length, pos0)
    di, dq, dk, ds = _ret_tables(c)
    has_state = state is not None
    const3 = lambda b, n: (0, 0, 0)
    args = [z, cos, sin, di, dq, dk, ds]
    specs = [pl.BlockSpec((c, MIX_W), lambda b, n: (rb(b, n), Z_RET // MIX_W)),
             pl.BlockSpec((c, HEAD_DIM), lambda b, n: (n, 0)),
             pl.BlockSpec((c, HEAD_DIM), lambda b, n: (n, 0)),
             pl.BlockSpec(di.shape, const3), pl.BlockSpec(dq.shape, const3),
             pl.BlockSpec(dk.shape, const3), pl.BlockSpec(ds.shape, const3)]
    sshape = (N_HEADS, HEAD_DIM, HEAD_DIM)
    if has_state:
        args.append(state)
        specs.append(pl.BlockSpec((1,) + sshape, lambda b, n: (b, 0, 0, 0)))
    return pl.pallas_call(
        functools.partial(_ret_kernel, c=c, nc=nc, has_state=has_state),
        grid=(bsz, nc), in_specs=specs,
        out_specs=[pl.BlockSpec((c, D_MIX), lambda b, n: (b * nc + n, 0)),
                   pl.BlockSpec((1,) + sshape, lambda b, n: (b, 0, 0, 0))],
        out_shape=[jax.ShapeDtypeStruct((bsz * length, D_MIX), BF16),
                   jax.ShapeDtypeStruct((bsz,) + sshape, F32)],
        scratch_shapes=[pltpu.VMEM(sshape, F32)],
        compiler_params=_cparams(("arbitrary", "arbitrary")), name="retention",
    )(*args)


SUB = 16


def _gla_head(q, k, v, g, st, c):
    big = _cumsum_rows(g)
    gl = big[c - 1:c, :]
    o = _mm_nt(q * jnp.exp(big), st)
    st_new = st * jnp.exp(gl) + _mm_tn(v, k * jnp.exp(gl - big))

    row = lax.broadcasted_iota(jnp.int32, (c, HEAD_DIM), 0)
    r2 = lax.broadcasted_iota(jnp.int32, (c, c), 0)
    c2 = lax.broadcasted_iota(jnp.int32, (c, c), 1)
    att = jnp.zeros((c, c), F32)
    m, sh = SUB, 4
    while m < c:
        anchor = jnp.concatenate(
            [jnp.broadcast_to(big[p * 2 * m + m - 1:p * 2 * m + m, :], (2 * m, HEAD_DIM))
             for p in range(c // (2 * m))], axis=0)
        right = ((row >> sh) & 1) == 1
        qt = jnp.where(right, q * jnp.exp(jnp.where(right, big - anchor, 0.0)), 0.0)
        kt = jnp.where(right, 0.0, k * jnp.exp(jnp.where(right, 0.0, anchor - big)))
        att = att + jnp.where((r2 >> (sh + 1)) == (c2 >> (sh + 1)), _mm_nt(qt, kt), 0.0)
        m, sh = m * 2, sh + 1
    o = o + _mm(att, v)

    f = jnp.exp(g)
    rsub = lax.broadcasted_iota(jnp.int32, (SUB, HEAD_DIM), 0)
    diag = []
    for blk in range(c // SUB):
        r0 = blk * SUB
        vb = v[r0:r0 + SUB, :]
        w = jnp.zeros((SUB, HEAD_DIM), F32)
        ob = jnp.zeros((SUB, HEAD_DIM), F32)
        for tl in range(SUB):
            t = r0 + tl
            if tl:
                w = w * f[t:t + 1, :]
            w = jnp.where(rsub == tl, k[t:t + 1, :], w)
            a_col = jnp.sum(w * q[t:t + 1, :], axis=1, keepdims=True)
            o_t = jnp.sum(a_col * vb, axis=0, keepdims=True)
            ob = jnp.where(rsub == tl, o_t, ob)
        diag.append(ob)
    return o + jnp.concatenate(diag, axis=0), st_new


def _hgrn_kernel(*refs, c, nc, has_state):
    it = iter(refs)
    z_ref, llb_ref, l1m_ref, oml_ref, gn_ref = (next(it) for _ in range(5))
    s0_ref = next(it) if has_state else None
    y_ref, so_ref, s_scr = next(it), next(it), next(it)
    n = pl.program_id(1)

    @pl.when(n == 0)
    def _():
        for h in range(N_HEADS):
            s_scr[h] = s0_ref[0, h].T if has_state else jnp.zeros((HEAD_DIM, HEAD_DIM), F32)

    for h in range(N_HEADS):
        lo = h * HEAD_DIM
        q = z_ref[:, lo:lo + HEAD_DIM]
        f = z_ref[:, D_MIX + lo:D_MIX + lo + HEAD_DIM]
        iv = z_ref[:, 2 * D_MIX + lo:2 * D_MIX + lo + HEAD_DIM]
        g = z_ref[:, 3 * D_MIX + lo:3 * D_MIX + lo + HEAD_DIM]
        ls = jnp.minimum(f, 0.0) - jnp.log1p(jnp.exp(-jnp.abs(f)))
        a = llb_ref[:, lo:lo + HEAD_DIM]
        b = l1m_ref[:, lo:lo + HEAD_DIM] + ls
        logf = jnp.maximum(a, b) + jnp.log1p(jnp.exp(-jnp.abs(a - b)))
        kf = oml_ref[:, lo:lo + HEAD_DIM] * jax.nn.sigmoid(-f)
        o, st = _gla_head(q, kf, iv, logf, s_scr[h], c)
        s_scr[h] = st
        y = _rms_rows(o, gn_ref[...]) * _silu(g)
        y_ref[:, lo:lo + HEAD_DIM] = y.astype(y_ref.dtype)

    @pl.when(n == nc - 1)
    def _():
        for h in range(N_HEADS):
            so_ref[0, h] = s_scr[h].T


def _hgrn(z, row0, bsz, length, c, lb, gnorm, state):
    nc = length // c
    rb = _row_block(row0, c, nc)
    has_state = state is not None
    vec = pl.BlockSpec((1, D_MIX), lambda b, n: (0, 0))
    lb = lb.reshape(1, D_MIX)
    args = [z, jnp.log(lb), jnp.log1p(-lb), 1.0 - lb, gnorm.reshape(1, HEAD_DIM)]
    specs = [pl.BlockSpec((c, MIX_W), lambda b, n: (rb(b, n), Z_HGRN // MIX_W)), vec, vec, vec,
             pl.BlockSpec((1, HEAD_DIM), lambda b, n: (0, 0))]
    sshape = (N_HEADS, HEAD_DIM, HEAD_DIM)
    if has_state:
        args.append(state)
        specs.append(pl.BlockSpec((1,) + sshape, lambda b, n: (b, 0, 0, 0)))
    return pl.pallas_call(
        functools.partial(_hgrn_kernel, c=c, nc=nc, has_state=has_state),
        grid=(bsz, nc), in_specs=specs,
        out_specs=[pl.BlockSpec((c, D_MIX), lambda b, n: (b * nc + n, 0)),
                   pl.BlockSpec((1,) + sshape, lambda b, n: (b, 0, 0, 0))],
        out_shape=[jax.ShapeDtypeStruct((bsz * length, D_MIX), BF16),
                   jax.ShapeDtypeStruct((bsz,) + sshape, F32)],
        scratch_shapes=[pltpu.VMEM(sshape, F32)],
        compiler_params=_cparams(("arbitrary", "arbitrary")), name="hgrn2",
    )(*args)


def _rwkv_kernel(*refs, c, nc, has_state):
    it = iter(refs)
    (z_ref, mu_ref, w0_ref, w2_ref, a0_ref, a2_ref, g2_ref, kk_ref, ka_ref, rk_ref,
     lnw_ref, lnb_ref) = (next(it) for _ in range(12))
    if has_state:
        sh0_ref, s0_ref = next(it), next(it)
    y_ref, so_ref, s_scr, prev_scr = next(it), next(it), next(it), next(it)
    n = pl.program_id(1)
    nh, hd = RWKV_HEADS, RWKV_N

    @pl.when(n == 0)
    def _():
        s_scr[...] = s0_ref[0] if has_state else jnp.zeros(s_scr.shape, F32)
        prev_scr[...] = (jnp.broadcast_to(sh0_ref[0], prev_scr.shape) if has_state
                         else jnp.zeros(prev_scr.shape, F32))

    z = z_ref[...]
    row = lax.broadcasted_iota(jnp.int32, z.shape, 0)
    prev = jnp.where(row == 0, prev_scr[0:1, :], pltpu.roll(z, 1, 0))
    prev_scr[...] = jnp.broadcast_to(z[c - 1:c, :], prev_scr.shape)
    xs = z + (prev - z) * mu_ref[...]
    o3 = 3 * D_MIX
    r, k, v = xs[:, :D_MIX], xs[:, D_MIX:2 * D_MIX], xs[:, 2 * D_MIX:o3]
    w_lo, a_lo, g_lo = xs[:, o3:o3 + 64], xs[:, o3 + 64:o3 + 128], xs[:, o3 + 128:o3 + 256]
    w = -_softplus(-(w0_ref[...] + _mm(jnp.tanh(w_lo), w2_ref[...]))) - 0.5
    ld = -jnp.exp(w)
    a = jax.nn.sigmoid(a0_ref[...] + _mm(a_lo, a2_ref[...]))
    g = _mm(jax.nn.sigmoid(g_lo), g2_ref[...])
    kkv = k * kk_ref[...]
    k = k * (1.0 + (a - 1.0) * ka_ref[...])
    lc = _cumsum_rows(ld)
    e_in = jnp.exp(lc)
    e_in_neg = jnp.exp(-lc)
    e_ex = jnp.exp(lc - ld)
    bonus = r * k * rk_ref[...]

    r2 = lax.broadcasted_iota(jnp.int32, (2 * c, 2 * c), 0)
    c2 = lax.broadcasted_iota(jnp.int32, (2 * c, 2 * c), 1)
    tq = jnp.where(r2 >= c, r2 - c, r2)
    ts = jnp.where(c2 >= c, c2 - c, c2)
    keep = (tq > ts) | ((r2 >= c) & (tq == ts))
    heads = range(nh)
    sls = [slice(h * hd, (h + 1) * hd) for h in heads]
    lhs, rhs, vs, s0s = [], [], [], []
    for h in heads:
        sl = sls[h]
        kkh = kkv[:, sl]
        kap = kkh * lax.rsqrt(jnp.sum(kkh * kkh, axis=-1, keepdims=True) + 1e-6)
        lhs.append(jnp.concatenate([kap * e_ex[:, sl], r[:, sl] * e_in[:, sl]], axis=0))
        rhs.append(jnp.concatenate([kap * a[:, sl] * e_in_neg[:, sl], k[:, sl] * e_in_neg[:, sl]], axis=0))
        vs.append(v[:, sl])
        s0s.append(s_scr[h])
    pm = [jnp.where(keep, _mm_nt(lhs[h], rhs[h]), 0.0) for h in heads]
    qs = [_mm_nt(lhs[h], s0s[h]) for h in heads]
    zv = jnp.zeros((c, hd), F32)
    mkv = [_mm(pm[h][:c, :], jnp.concatenate([zv, vs[h]], axis=0)) for h in heads]
    t_inv = _tri_inv([pm[h][:c, :c] for h in heads], c)
    us = [-_mm(t_inv[h], qs[h][:c, :] + mkv[h]) for h in heads]
    uv = [jnp.concatenate([us[h], vs[h]], axis=0) for h in heads]
    ys = [qs[h][c:, :] + _mm(pm[h][c:, :], uv[h]) for h in heads]
    sn = [(s0s[h] + _mm_tn(uv[h], rhs[h])) * e_in[c - 1:c, sls[h]] for h in heads]
    for h in heads:
        sl = sls[h]
        s_scr[h] = sn[h]
        y = ys[h]
        yc = y - jnp.mean(y, axis=-1, keepdims=True)
        yn = yc * lax.rsqrt(jnp.mean(yc * yc, axis=-1, keepdims=True) + RWKV_GN_EPS)
        yn = yn * lnw_ref[:, sl] + lnb_ref[:, sl]
        yn = yn + jnp.sum(bonus[:, sl], axis=-1, keepdims=True) * vs[h]
        y_ref[:, sl] = (yn * g[:, sl]).astype(y_ref.dtype)

    @pl.when(n == nc - 1)
    def _():
        so_ref[0] = s_scr[...]


def _rwkv(z, row0, bsz, length, c, p, shift, state):
    nc = length // c
    rb = _row_block(row0, c, nc)
    has_state = state is not None
    pad = MIX_W - RWKV_COLS
    c2 = lambda b, n: (0, 0)
    args = [z, jnp.pad(p["mu"], (0, pad)).reshape(1, MIX_W),
            p["w0"].reshape(1, D_MIX), p["w2"], p["a0"].reshape(1, D_MIX), p["a2"], p["g2"],
            p["k_k"].reshape(1, D_MIX), p["k_a"].reshape(1, D_MIX), p["r_k"].reshape(1, D_MIX),
            p["ln_w"].reshape(1, D_MIX), p["ln_b"].reshape(1, D_MIX)]
    specs = [pl.BlockSpec((c, MIX_W), lambda b, n: (rb(b, n), Z_RWKV // MIX_W))]
    specs += [pl.BlockSpec(a.shape, c2) for a in args[1:]]
    sshape = (RWKV_HEADS, RWKV_N, RWKV_N)
    if has_state:
        args += [jnp.pad(shift, ((0, 0), (0, pad))).reshape(bsz, 1, MIX_W), state]
        specs += [pl.BlockSpec((1, 1, MIX_W), lambda b, n: (b, 0, 0)),
                  pl.BlockSpec((1,) + sshape, lambda b, n: (b, 0, 0, 0))]
    return pl.pallas_call(
        functools.partial(_rwkv_kernel, c=c, nc=nc, has_state=has_state),
        grid=(bsz, nc), in_specs=specs,
        out_specs=[pl.BlockSpec((c, D_MIX), lambda b, n: (b * nc + n, 0)),
                   pl.BlockSpec((1,) + sshape, lambda b, n: (b, 0, 0, 0))],
        out_shape=[jax.ShapeDtypeStruct((bsz * length, D_MIX), BF16),
                   jax.ShapeDtypeStruct((bsz,) + sshape, F32)],
        scratch_shapes=[pltpu.VMEM(sshape, F32), pltpu.VMEM((8, MIX_W), F32)],
        compiler_params=_cparams(("arbitrary", "arbitrary")), name="rwkv7",
    )(*args)


CONV_PAD = 8


def _gdn_kernel(*refs, c, nc, has_state):
    it = iter(refs)
    z_ref, ab_ref, cw_ref, alog_ref, dtb_ref, gn_ref = (next(it) for _ in range(6))
    if has_state:
        cv0_ref, s0_ref = next(it), next(it)
    y_ref, so_ref, s_scr, xp_scr = next(it), next(it), next(it), next(it)
    n = pl.program_id(1)
    qkv_w = 3 * D_MIX
    tail = GDN_CONV - 1

    @pl.when(n == 0)
    def _():
        s_scr[...] = s0_ref[0] if has_state else jnp.zeros(s_scr.shape, F32)
        xp_scr[0:CONV_PAD, :] = jnp.zeros((CONV_PAD, qkv_w), F32)
        if has_state:
            xp_scr[CONV_PAD - tail:CONV_PAD, :] = cv0_ref[0]

    xp_scr[CONV_PAD:CONV_PAD + c, :] = z_ref[:, :qkv_w]
    conv = xp_scr[CONV_PAD - tail:CONV_PAD - tail + c, :] * cw_ref[0:1, :]
    for j in range(1, GDN_CONV):
        conv = conv + xp_scr[CONV_PAD - tail + j:CONV_PAD - tail + j + c, :] * cw_ref[j:j + 1, :]
    xp_scr[CONV_PAD - tail:CONV_PAD, :] = xp_scr[CONV_PAD + c - tail:CONV_PAD + c, :]
    act = _silu(conv)

    ab = ab_ref[...]
    gdec = -jnp.exp(alog_ref[...]) * _softplus(ab + dtb_ref[...])
    gcum = _cumsum_rows(gdec)
    beta = jax.nn.sigmoid(ab)

    r2 = lax.broadcasted_iota(jnp.int32, (c, c), 0)
    c2 = lax.broadcasted_iota(jnp.int32, (c, c), 1)
    eye, causal, strict = r2 == c2, r2 >= c2, r2 > c2
    heads = range(N_HEADS)
    kq, ks, vs, ss, gcols, bcols, decs = [], [], [], [], [], [], []
    for h in heads:
        lo = h * HEAD_DIM
        q = act[:, lo:lo + HEAD_DIM]
        k = act[:, D_MIX + lo:D_MIX + lo + HEAD_DIM]
        q = q * lax.rsqrt(jnp.sum(q * q, axis=-1, keepdims=True) + 1e-6) * HEAD_DIM ** -0.5
        k = k * lax.rsqrt(jnp.sum(k * k, axis=-1, keepdims=True) + 1e-6)
        gcol = gcum[:, h:h + 1]
        rel = gcol - _row_from_col(gcol, eye)
        kq.append(jnp.concatenate([k, q], axis=0))
        ks.append(k)
        vs.append(act[:, 2 * D_MIX + lo:2 * D_MIX + lo + HEAD_DIM])
        ss.append(s_scr[h])
        gcols.append(gcol)
        bcols.append(beta[:, N_HEADS + h:N_HEADS + h + 1])
        decs.append(jnp.where(causal, jnp.exp(jnp.where(causal, rel, 0.0)), 0.0))
    kk_qk = [_mm_nt(kq[h], ks[h]) for h in heads]
    ks_qs = [_mm(kq[h], ss[h]) for h in heads]
    t_inv = _tri_inv([jnp.where(strict, bcols[h] * kk_qk[h][:c] * decs[h], 0.0) for h in heads], c)
    egs = [jnp.exp(gcols[h]) for h in heads]
    us = [_mm(t_inv[h], bcols[h] * (vs[h] - egs[h] * ks_qs[h][:c])) for h in heads]
    os_ = [egs[h] * ks_qs[h][c:] + _mm(kk_qk[h][c:] * decs[h], us[h]) for h in heads]
    for h in heads:
        lo = h * HEAD_DIM
        gl = gcols[h][c - 1:c, :]
        s_scr[h] = jnp.exp(gl) * ss[h] + _mm_tn(ks[h] * jnp.exp(gl - gcols[h]), us[h])
        zg = z_ref[:, qkv_w + lo:qkv_w + lo + HEAD_DIM]
        y = _rms_rows(os_[h], gn_ref[...]) * _silu(zg)
        y_ref[:, lo:lo + HEAD_DIM] = y.astype(y_ref.dtype)

    @pl.when(n == nc - 1)
    def _():
        so_ref[0] = s_scr[...]


def _gdn(z, row0, bsz, length, c, p, conv_state, state):
    nc = length // c
    rb = _row_block(row0, c, nc)
    has_state = state is not None
    c2 = lambda b, n: (0, 0)
    lane_pad = lambda a: jnp.pad(a, (0, HEAD_DIM - a.shape[0])).reshape(1, HEAD_DIM)
    args = [z, z, p["conv_w"], lane_pad(p["A_log"]), lane_pad(p["dt_bias"]), p["norm_g"].reshape(1, HEAD_DIM)]
    specs = [pl.BlockSpec((c, MIX_W), lambda b, n: (rb(b, n), Z_GDN // MIX_W)),
             pl.BlockSpec((c, HEAD_DIM), lambda b, n: (rb(b, n), Z_AB // HEAD_DIM))]
    specs += [pl.BlockSpec(a.shape, c2) for a in args[2:]]
    sshape = (N_HEADS, HEAD_DIM, HEAD_DIM)
    if has_state:
        args += [conv_state, state]
        specs += [pl.BlockSpec((1, GDN_CONV - 1, 3 * D_MIX), lambda b, n: (b, 0, 0)),
                  pl.BlockSpec((1,) + sshape, lambda b, n: (b, 0, 0, 0))]
    return pl.pallas_call(
        functools.partial(_gdn_kernel, c=c, nc=nc, has_state=has_state),
        grid=(bsz, nc), in_specs=specs,
        out_specs=[pl.BlockSpec((c, D_MIX), lambda b, n: (b * nc + n, 0)),
                   pl.BlockSpec((1,) + sshape, lambda b, n: (b, 0, 0, 0))],
        out_shape=[jax.ShapeDtypeStruct((bsz * length, D_MIX), BF16),
                   jax.ShapeDtypeStruct((bsz,) + sshape, F32)],
        scratch_shapes=[pltpu.VMEM(sshape, F32), pltpu.VMEM((CONV_PAD + c, 3 * D_MIX), F32)],
        compiler_params=_cparams(("arbitrary", "arbitrary")), name="gdn",
    )(*args)


def _merge_kernel(y0_ref, y1_ref, y2_ref, y3_ref, gate_ref, x_ref, gt_ref, sc_ref, sh_ref, g_ref,
                  wb_ref, wo_ref, rw_ref, rb_ref, xo_ref, h_ref, lg_ref, *, gpt):
    tm, d = x_ref.shape
    merged = jnp.zeros((tm, d), F32)
    for nb, y_ref in enumerate((y0_ref, y1_ref, y2_ref, y3_ref)):
        br = jnp.dot(y_ref[...], wb_ref[0, nb], preferred_element_type=F32)
        merged = merged + gate_ref[:, nb * d:(nb + 1) * d].astype(F32) * br
    m = jnp.dot(merged.astype(BF16), wo_ref[0], preferred_element_type=F32)
    gt = gt_ref[0]
    x = x_ref[...] + (m.reshape(gpt, GROUP, d) * gt[:, None, :]).reshape(tm, d)
    xo_ref[...] = x
    h = _modulate(_rms_rows(x, g_ref[...]), sc_ref[0], sh_ref[0], gpt)
    h_ref[...] = h
    lg_ref[...] = jnp.dot(h, rw_ref[0], precision=lax.Precision.HIGHEST,
                          preferred_element_type=F32) + rb_ref[0]


def _merge(ys, gates, x, modg, l, g2, wb, wo, rw, rb, n_prompt):
    t, d = x.shape
    tm = 256
    gpt, gidx = _group_block(tm, n_prompt // tm)
    row = lambda w: pl.BlockSpec((tm, w), lambda i: (i, 0))

    def mod_spec(col):
        return pl.BlockSpec((1, gpt, d), lambda i: (l, gidx(i), col))

    def const(shape):
        return pl.BlockSpec(shape, lambda i: (0,) * len(shape), pipeline_mode=pl.Buffered(1))

    def layer(arr):
        return pl.BlockSpec((1,) + arr.shape[1:], lambda i: (l,) + (0,) * (arr.ndim - 1),
                            pipeline_mode=pl.Buffered(1))
    nr = rw.shape[2]
    return pl.pallas_call(
        functools.partial(_merge_kernel, gpt=gpt),
        grid=(t // tm,),
        in_specs=[row(D_MIX)] * 4 + [row(N_BRANCH * d), row(d), mod_spec(2), mod_spec(4), mod_spec(3),
                                     const((1, d)), layer(wb), layer(wo), layer(rw), layer(rb)],
        out_specs=[row(d), row(d), row(nr)],
        out_shape=[jax.ShapeDtypeStruct((t, d), F32), jax.ShapeDtypeStruct((t, d), F32),
                   jax.ShapeDtypeStruct((t, nr), F32)],
        compiler_params=_cparams(("arbitrary",)), name="merge",
    )(*ys, gates, x, modg, modg, modg, g2.reshape(1, d), wb, wo, rw, rb)


MOE_BLK = 256
MOE_UNROLL = 8


def _moe_kernel(be_ref, tok_ref, slot_ref, nused_ref, h_hbm, w_ref, wg_ref, wu_ref, wd_ref, o_hbm,
                x0, x1, y0, y1, wg_bf, wu_bf, wd_bf, gsem, ssem):
    i = pl.program_id(0)
    nused = nused_ref[0]

    def gather(blk, xb, q, r):
        return pltpu.make_async_copy(h_hbm.at[pl.ds(tok_ref[blk * MOE_BLK + r], 1)], xb.at[pl.ds(r, 1)],
                                     gsem.at[q])

    def scatter(blk, yb, q, r):
        return pltpu.make_async_copy(yb.at[pl.ds(r, 1)],
                                     o_hbm.at[pl.ds(slot_ref[(blk + 1) * MOE_BLK + r], 1)], ssem.at[q])

    def for_rows(fn):
        def group(j, carry):
            for u in range(MOE_UNROLL):
                fn(j * MOE_UNROLL + u)
            return carry
        lax.fori_loop(0, MOE_BLK // MOE_UNROLL, group, 0)

    def step(q, xa, ya, xb, yb):
        @pl.when(i == 0)
        def _():
            yb[...] = jnp.zeros(yb.shape, F32)
            n_real = o_hbm.shape[0] - 2 * MOE_BLK
            init = pltpu.make_async_copy(yb, o_hbm.at[pl.ds(n_real, MOE_BLK)], ssem.at[q])
            init.start()
            init.wait()
            for_rows(lambda r: gather(i, xa, q, r).start())

        @pl.when(i > 0)
        def _():
            for_rows(lambda r: scatter(i - 2, ya, q, r).wait())
        for_rows(lambda r: gather(i, xa, q, r).wait())

        @pl.when((i == 0) | (be_ref[i] != be_ref[jnp.maximum(i - 1, 0)]))
        def _():
            wg_bf[...] = wg_ref[0, 0].astype(BF16)
            wu_bf[...] = wu_ref[0, 0].astype(BF16)
            wd_bf[...] = wd_ref[0, 0].astype(BF16)

        for r in range(MOE_BLK):
            gather(i + 1, xb, 1 - q, r).start()
        for r in range(MOE_BLK):
            scatter(i - 1, yb, 1 - q, r).start()
        x = xa[...].astype(BF16)
        hid = _silu(jnp.dot(x, wg_bf[...], preferred_element_type=F32)) * jnp.dot(
            x, wu_bf[...], preferred_element_type=F32)
        ya[...] = jnp.dot(hid.astype(BF16), wd_bf[...], preferred_element_type=F32) * w_ref[...]

        @pl.when(i == nused - 1)
        def _():
            for_rows(lambda r: scatter(i, ya, q, r).start())
            for_rows(lambda r: scatter(i - 1, yb, 1 - q, r).wait())
            for_rows(lambda r: scatter(i, ya, q, r).wait())
            for_rows(lambda r: gather(i + 1, xb, 1 - q, r).wait())

    @pl.when((i < nused) & (i % 2 == 0))
    def _():
        step(0, x0, y0, x1, y1)

    @pl.when((i < nused) & (i % 2 == 1))
    def _():
        step(1, x1, y1, x0, y0)


def _route(logits, n_tok):
    lg = logits[:, :N_GROUPS]
    pg = jax.nn.softmax(lg, axis=-1)
    gsel = jnp.argmax(lg, axis=-1).astype(jnp.int32)
    gw = jnp.take_along_axis(pg, gsel[:, None], axis=-1)
    le = logits[:, N_GROUPS:N_GROUPS + N_EXPERTS].reshape(n_tok, N_GROUPS, EXPERTS_PER_GROUP)
    le = jnp.take_along_axis(le, gsel[:, None, None], axis=1)[:, 0]
    top_v, top_i = lax.top_k(jax.nn.softmax(le, axis=-1), TOPK)
    wts = top_v / jnp.sum(top_v, axis=-1, keepdims=True) * gw
    eid = (gsel[:, None] * EXPERTS_PER_GROUP + top_i).reshape(-1).astype(jnp.int32)
    a = n_tok * TOPK
    order = jnp.argsort(eid).astype(jnp.int32)
    counts = jnp.sum((eid[:, None] == jnp.arange(N_EXPERTS, dtype=jnp.int32)[None, :]).astype(jnp.int32), axis=0)
    padded = (counts + MOE_BLK - 1) // MOE_BLK * MOE_BLK
    pend = jnp.cumsum(padded)
    pstart = pend - padded
    cstart = jnp.cumsum(counts) - counts
    n_blocks = -(-a // MOE_BLK) + N_EXPERTS
    rows = n_blocks * MOE_BLK
    blk_row0 = jnp.arange(n_blocks, dtype=jnp.int32) * MOE_BLK
    block_e = jnp.minimum(jnp.sum((pend[None, :] <= blk_row0[:, None]).astype(jnp.int32), axis=1),
                          N_EXPERTS - 1).astype(jnp.int32)
    nused = (pend[-1:] // MOE_BLK).astype(jnp.int32)
    e_row = jnp.repeat(block_e, MOE_BLK)
    row_id = jnp.arange(rows, dtype=jnp.int32)
    j_row = row_id - pstart[e_row]
    valid = j_row < counts[e_row]
    asg = order[jnp.clip(cstart[e_row] + j_row, 0, a - 1)]
    tok = jnp.where(valid, asg // TOPK, 0)
    dummy = a + ((row_id // MOE_BLK) % 2) * MOE_BLK + row_id % MOE_BLK
    slot = jnp.where(valid, (asg % TOPK) * n_tok + asg // TOPK, dummy)
    wrow = jnp.where(valid, wts.reshape(-1)[asg], 0.0)
    tok = jnp.concatenate([tok, jnp.zeros((MOE_BLK,), jnp.int32)])
    slot = jnp.concatenate([a + MOE_BLK + jnp.arange(MOE_BLK, dtype=jnp.int32), slot])
    return block_e, tok, slot, nused, wrow.reshape(rows, 1), n_blocks


def _moe(h, logits, wg, wu, wd, l):
    t, d = h.shape
    block_e, tok, slot, nused, wrow, n_blocks = _route(logits, t)
    grid_spec = pltpu.PrefetchScalarGridSpec(
        num_scalar_prefetch=4, grid=(n_blocks,),
        in_specs=[pl.BlockSpec(memory_space=pl.ANY),
                  pl.BlockSpec((MOE_BLK, 1), lambda i, be, *_: (i, 0)),
                  pl.BlockSpec((1, 1, d, D_EXPERT), lambda i, be, *_: (l, be[i], 0, 0)),
                  pl.BlockSpec((1, 1, d, D_EXPERT), lambda i, be, *_: (l, be[i], 0, 0)),
                  pl.BlockSpec((1, 1, D_EXPERT, d), lambda i, be, *_: (l, be[i], 0, 0))],
        out_specs=pl.BlockSpec(memory_space=pl.ANY),
        scratch_shapes=[pltpu.VMEM((MOE_BLK, d), F32), pltpu.VMEM((MOE_BLK, d), F32),
                        pltpu.VMEM((MOE_BLK, d), F32), pltpu.VMEM((MOE_BLK, d), F32),
                        pltpu.VMEM((d, D_EXPERT), BF16), pltpu.VMEM((d, D_EXPERT), BF16),
                        pltpu.VMEM((D_EXPERT, d), BF16),
                        pltpu.SemaphoreType.DMA((2,)), pltpu.SemaphoreType.DMA((2,))])
    out = pl.pallas_call(
        _moe_kernel, grid_spec=grid_spec,
        out_shape=jax.ShapeDtypeStruct((t * TOPK + 2 * MOE_BLK, d), F32),
        compiler_params=_cparams(("arbitrary",)), name="moe",
    )(block_e, tok, slot, nused, h, wrow, wg, wu, wd)
    return out


def _pack_w_in(w_in):
    depth, d, _ = w_in.shape
    o_rwkv, o_gdn = 4096, 4096 + RWKV_COLS
    o_ab = o_gdn + 4 * D_MIX
    o_gate = o_ab + 2 * N_HEADS
    zeros = lambda n: jnp.zeros((depth, d, n), BF16)
    wb = w_in.astype(BF16)
    w_main = jnp.concatenate(
        [wb[:, :, :o_gdn], zeros(Z_GDN - o_gdn), wb[:, :, o_gdn:o_ab], wb[:, :, o_ab:o_gate],
         zeros(N_MAIN - Z_AB - 2 * N_HEADS)], axis=2)
    return w_main, wb[:, :, o_gate:]


def _mixers(z, l, groups, params):
    outs = [[] for _ in range(N_BRANCH)]
    states = []
    for (row0, bsz, length, c, pos0, st) in groups:
        s_ret, s_hgrn, s_rwkv, s_shift, s_gdn, s_conv = st if st is not None else (None,) * 6
        y_a, n_ret = _retention(z, row0, bsz, length, c, pos0, s_ret)
        y_b, n_hgrn = _hgrn(z, row0, bsz, length, c, params["hgrn_lb"][l], params["hgrn_norm_g"][l], s_hgrn)
        y_c, n_rwkv = _rwkv(z, row0, bsz, length, c, {k: v[l] for k, v in params["rwkv"].items()},
                            s_shift, s_rwkv)
        y_d, n_gdn = _gdn(z, row0, bsz, length, c, {k: v[l] for k, v in params["gdn"].items()},
                          s_conv, s_gdn)
        for lst, y in zip(outs, (y_a, y_b, y_c, y_d)):
            lst.append(y)
        def seq_row(j, col0, width):
            return lax.slice(z, (row0 + j, col0), (row0 + (bsz - 1) * length + j + 1, col0 + width),
                             (length, 1))
        n_shift = seq_row(length - 1, Z_RWKV, RWKV_COLS)
        n_conv = jnp.stack([seq_row(length - (GDN_CONV - 1) + j, Z_GDN, 3 * D_MIX)
                            for j in range(GDN_CONV - 1)], axis=1)
        states.append((n_ret, n_hgrn, n_rwkv, n_shift, n_gdn, n_conv))
    return [jnp.concatenate(lst, axis=0) for lst in outs], states


def kernel(x_prompt, x_sample, c_prompt, c_sample, state_ret, state_hgrn, state_rwkv, state_rwkv_shift,
           state_gdn, state_gdn_conv, ada_w, ada_b, norm1_g, norm2_g, w_in, hgrn_lb_logits, hgrn_norm_g,
           rwkv_mu, rwkv_w0, rwkv_w2, rwkv_a0, rwkv_a2, rwkv_g2, rwkv_k_k, rwkv_k_a, rwkv_r_k, rwkv_ln_w,
           rwkv_ln_b, gdn_conv_w, gdn_A_log, gdn_dt_bias, gdn_norm_g, w_branch, w_out, router_g, router_g_b,
           router_e, router_e_b, moe_w_gate, moe_w_up, moe_w_down, final_norm_g):
    depth = ada_w.shape[0]
    bp, lp, d = x_prompt.shape
    bs, ls, _ = x_sample.shape
    n_prompt = bp * lp
    assert bp == 1 and ls == GROUP and bs == 16

    lb_cum = jnp.cumsum(jax.nn.softmax(hgrn_lb_logits.astype(F32), axis=0), axis=0)
    params = dict(
        hgrn_lb=lb_cum - lb_cum[:1], hgrn_norm_g=hgrn_norm_g,
        rwkv=dict(mu=rwkv_mu, w0=rwkv_w0, w2=rwkv_w2, a0=rwkv_a0, a2=rwkv_a2, g2=rwkv_g2, k_k=rwkv_k_k,
                  k_a=rwkv_k_a, r_k=rwkv_r_k.reshape(depth, D_MIX), ln_w=rwkv_ln_w, ln_b=rwkv_ln_b),
        gdn=dict(conv_w=gdn_conv_w, A_log=gdn_A_log, dt_bias=gdn_dt_bias, norm_g=gdn_norm_g))

    c_all = jnp.concatenate([c_prompt, c_sample, jnp.zeros((24 - bp - bs, d), F32)], axis=0)
    mod = _ada(c_all, ada_w, ada_b)
    modg = jnp.concatenate([jnp.broadcast_to(mod[:, :1], (depth, 16, 6 * d)), mod[:, 1:1 + bs]], axis=1)

    w_main, w_gate = _pack_w_in(w_in)
    wb_bf, wo_bf = w_branch.astype(BF16), w_out.astype(BF16)
    n_r = 128
    r_w = jnp.concatenate([router_g, router_e, jnp.zeros((depth, d, n_r - N_GROUPS - N_EXPERTS), F32)], axis=2)
    r_b = jnp.concatenate([router_g_b, router_e_b, jnp.zeros((depth, n_r - N_GROUPS - N_EXPERTS), F32)],
                          axis=1).reshape(depth, 1, n_r)

    x = jnp.concatenate([x_prompt.reshape(n_prompt, d), x_sample.reshape(bs * ls, d)], axis=0)
    moe_out = None
    new_p, new_s = [], []
    for l in range(depth):
        x, h = _norm(x, moe_out, modg, l, norm1_g[l], n_prompt, final=False)
        z = _proj(h, w_main, l, tn=768, sigmoid=False, out_dtype=F32)
        gates = _proj(h, w_gate, l, tn=1024, sigmoid=True, out_dtype=BF16)
        groups = [(0, bp, lp, 64, 0.0, None),
                  (n_prompt, bs, ls, ls, float(PAST_LEN),
                   (state_ret[l], state_hgrn[l], state_rwkv[l], state_rwkv_shift[l], state_gdn[l],
                    state_gdn_conv[l]))]
        ys, (st_p, st_s) = _mixers(z, l, groups, params)
        new_p.append(st_p)
        new_s.append(st_s)
        x, h2, logits = _merge(ys, gates, x, modg, l, norm2_g[l], wb_bf, wo_bf, r_w, r_b, n_prompt)
        moe_out = _moe(h2, logits, moe_w_gate, moe_w_up, moe_w_down, l)
    y = _norm(x, moe_out, modg, depth - 1, final_norm_g, n_prompt, final=True)

    def stack(lst, i):
        return jnp.stack([s[i] for s in lst]).astype(F32)
    return ((y[:n_prompt].reshape(bp, lp, d), y[n_prompt:].reshape(bs, ls, d))
            + tuple(stack(new_p, i) for i in range(6)) + tuple(stack(new_s, i) for i in range(6)))
```

```python
import functools
import math

import numpy as np
import jax
import jax.numpy as jnp
from jax import lax
from jax.experimental import pallas as pl
from jax.experimental.pallas import tpu as pltpu

F32 = jnp.float32
BF16 = jnp.bfloat16

D_MODEL = 2048
D_MIX = 512
HEAD_DIM = 128
N_HEADS = 4
RWKV_N = 64
RWKV_HEADS = 8
RWKV_COLS = 1792
GDN_CONV = 4
N_BRANCH = 4
N_GROUPS = 4
EXPERTS_PER_GROUP = 8
N_EXPERTS = 32
TOPK = 2
D_EXPERT = 512
PAST_LEN = 4096
ROPE_BASE = 10000.0
NORM_EPS = 1e-6
GN_EPS = 1e-6
RWKV_GN_EPS = 64e-5
RET_EXP_LO, RET_EXP_HI = 5.0, 12.0

Z_RET, Z_HGRN, Z_RWKV, Z_GDN, Z_AB = 0, 2048, 4096, 6144, 8192
N_MAIN = 8448
MIX_W = 2048
GROUP = 32

VMEM_LIMIT = 56 * 1024 * 1024


def _cparams(sem):
    return pltpu.CompilerParams(dimension_semantics=sem, vmem_limit_bytes=VMEM_LIMIT)


def _mm(a, b):
    return jnp.dot(a.astype(BF16), b.astype(BF16), preferred_element_type=F32)


def _mm_nt(a, b):
    return lax.dot_general(a.astype(BF16), b.astype(BF16), (((1,), (1,)), ((), ())),
                           preferred_element_type=F32)


def _mm_tn(a, b):
    return lax.dot_general(a.astype(BF16), b.astype(BF16), (((0,), (0,)), ((), ())),
                           preferred_element_type=F32)


def _silu(x):
    return x * jax.nn.sigmoid(x)


def _softplus(x):
    return jnp.maximum(x, 0.0) + jnp.log1p(jnp.exp(-jnp.abs(x)))


LANES = 128
ROW_CHUNKS = D_MODEL // LANES


def _rows_from_chunks(ref, n_rows):
    return jnp.concatenate([ref[pl.ds(c, n_rows, stride=ROW_CHUNKS), :] for c in range(ROW_CHUNKS)], axis=1)


def _rows_to_chunks(ref, val):
    n_rows = val.shape[0]
    for c in range(ROW_CHUNKS):
        ref[pl.ds(c, n_rows, stride=ROW_CHUNKS), :] = val[:, c * LANES:(c + 1) * LANES]


def _cumsum_rows(x):
    n = x.shape[0]
    row = lax.broadcasted_iota(jnp.int32, x.shape, 0)
    s = 1
    while s < n:
        x = x + jnp.where(row >= s, pltpu.roll(x, s, 0), 0.0)
        s *= 2
    return x


def _row_from_col(col, eye):
    return jnp.sum(jnp.where(eye, col, 0.0), axis=0, keepdims=True)


def _tri_inv(n_mats, c):
    r = lax.broadcasted_iota(jnp.int32, (c, c), 0)
    col = lax.broadcasted_iota(jnp.int32, (c, c), 1)
    eye = jnp.where(r == col, 1.0, 0.0)
    pair = (r >> 1) == (col >> 1)
    xs = [eye - jnp.where(pair, n, 0.0) for n in n_mats]
    m, sh = 2, 1
    while m < c:
        lvl = ((r >> (sh + 1)) == (col >> (sh + 1))) & ((r >> sh) != (col >> sh))
        ts = [_mm(jnp.where(lvl, n, 0.0), x) for n, x in zip(n_mats, xs)]
        xs = [x - _mm(x, t) for x, t in zip(xs, ts)]
        m, sh = m * 2, sh + 1
    return xs


def _ada_kernel(c_ref, w_ref, b_ref, o_ref):
    cm = _silu(c_ref[...])
    o_ref[0] = _mm(cm, w_ref[0]) + b_ref[0]


def _ada(c_all, ada_w, ada_b):
    depth, d, n = ada_w.shape
    rows = c_all.shape[0]
    tn = 1024
    return pl.pallas_call(
        _ada_kernel,
        grid=(depth, n // tn),
        in_specs=[pl.BlockSpec((rows, d), lambda l, j: (0, 0)),
                  pl.BlockSpec((1, d, tn), lambda l, j: (l, 0, j)),
                  pl.BlockSpec((1, 1, tn), lambda l, j: (l, 0, j))],
        out_specs=pl.BlockSpec((1, rows, tn), lambda l, j: (l, 0, j)),
        out_shape=jax.ShapeDtypeStruct((depth, rows, n), F32),
        compiler_params=_cparams(("arbitrary", "arbitrary")),
        name="ada",
    )(c_all, ada_w, ada_b.reshape(depth, 1, n))


def _group_block(tm, n_prompt_tiles):
    gpt = tm // GROUP
    first_sample = 16 // gpt

    def idx(i):
        return jnp.where(i < n_prompt_tiles, 0, first_sample + i - n_prompt_tiles)
    return gpt, idx


def _modulate(y, sc, sh, gpt):
    tm, d = y.shape
    y3 = y.reshape(gpt, GROUP, d)
    return (y3 * (1.0 + sc[:, None, :]) + sh[:, None, :]).reshape(tm, d)


def _rms_rows(x, g):
    return x * lax.rsqrt(jnp.mean(x * x, axis=-1, keepdims=True) + NORM_EPS) * g


def _norm_kernel(*refs, gpt, with_moe, modulated):
    it = iter(refs)
    x_ref = next(it)
    if with_moe:
        m0_ref, m1_ref, gt_ref = next(it), next(it), next(it)
    g_ref = next(it)
    if modulated:
        sc_ref, sh_ref = next(it), next(it)
    x = x_ref[...]
    tm, d = x.shape
    if with_moe:
        moe = _rows_from_chunks(m0_ref, tm) + _rows_from_chunks(m1_ref, tm)
        gt = gt_ref[0]
        x = x + (moe.reshape(gpt, GROUP, d) * gt[:, None, :]).reshape(tm, d)
    y = _rms_rows(x, g_ref[...])
    if modulated:
        xo_ref, h_ref = next(it), next(it)
        xo_ref[...] = x
        h_ref[...] = _modulate(y, sc_ref[0], sh_ref[0], gpt).astype(BF16)
    else:
        y_ref = next(it)
        y_ref[...] = y


def _norm(x, moe_out, modg, l, g, n_prompt, *, final):
    t, d = x.shape
    tm = 256
    gpt, gidx = _group_block(tm, n_prompt // tm)
    with_moe = moe_out is not None
    row = pl.BlockSpec((tm, d), lambda i: (i, 0))

    def mod_spec(col, layer):
        return pl.BlockSpec((1, gpt, d), lambda i: (layer, gidx(i), col))
    args, specs = [x], [row]
    if with_moe:
        lm = l if final else l - 1
        args += [moe_out, moe_out, modg]
        specs += [pl.BlockSpec((tm * ROW_CHUNKS, LANES), lambda i: (i, 0)),
                  pl.BlockSpec((tm * ROW_CHUNKS, LANES), lambda i: (t // tm + i, 0)), mod_spec(5, lm)]
    args.append(g.reshape(1, d))
    specs.append(pl.BlockSpec((1, d), lambda i: (0, 0)))
    if not final:
        args += [modg, modg]
        specs += [mod_spec(1, l), mod_spec(0, l)]
        out_shape = [jax.ShapeDtypeStruct((t, d), F32), jax.ShapeDtypeStruct((t, d), BF16)]
        out_specs = [row, row]
    else:
        out_shape = jax.ShapeDtypeStruct((t, d), F32)
        out_specs = row
    return pl.pallas_call(
        functools.partial(_norm_kernel, gpt=gpt, with_moe=with_moe, modulated=not final),
        grid=(t // tm,), in_specs=specs, out_specs=out_specs, out_shape=out_shape,
        compiler_params=_cparams(("arbitrary",)), name="norm",
    )(*args)


def _proj_kernel(a_ref, w_ref, o_ref, *, sigmoid):
    acc = jnp.dot(a_ref[...], w_ref[0], preferred_element_type=F32)
    if sigmoid:
        acc = jax.nn.sigmoid(acc)
    o_ref[...] = acc.astype(o_ref.dtype)


def _proj(a, w, l, *, tn, sigmoid, out_dtype):
    t, k = a.shape
    n = w.shape[2]
    tm = t // 4
    return pl.pallas_call(
        functools.partial(_proj_kernel, sigmoid=sigmoid),
        grid=(n // tn, t // tm),
        in_specs=[pl.BlockSpec((tm, k), lambda j, i: (i, 0)),
                  pl.BlockSpec((1, k, tn), lambda j, i: (l, 0, j))],
        out_specs=pl.BlockSpec((tm, tn), lambda j, i: (i, j)),
        out_shape=jax.ShapeDtypeStruct((t, n), out_dtype),
        compiler_params=_cparams(("arbitrary", "arbitrary")), name="proj",
    )(a, w)


RET_CHUNK = 256


def _ret_tables(c):
    e = np.linspace(RET_EXP_LO, RET_EXP_HI, N_HEADS)
    lg = np.log1p(-np.exp2(-e))
    t = np.arange(c, dtype=np.float64)
    rel = t[:, None] - t[None, :]
    d_intra = np.where(rel >= 0, np.exp(lg[:, None, None] * np.where(rel >= 0, rel, 0.0)), 0.0)
    d_q = np.exp(lg[:, None] * (t + 1.0))[:, :, None] * np.ones((1, 1, HEAD_DIM))
    d_k = np.exp(lg[:, None] * (c - 1.0 - t))[:, :, None] * np.ones((1, 1, HEAD_DIM))
    d_s = np.exp(lg * c)[:, None, None] * np.ones((1, 8, HEAD_DIM))
    return tuple(jnp.asarray(a, F32) for a in (d_intra, d_q, d_k, d_s))


def _ret_kernel(*refs, c, nc, has_state):
    it = iter(refs)
    z_ref, cos_ref, sin_ref, di_ref, dq_ref, dk_ref, ds_ref = (next(it) for _ in range(7))
    s0_ref = next(it) if has_state else None
    y_ref, so_ref, s_scr = next(it), next(it), next(it)
    n = pl.program_id(1)

    @pl.when(n == 0)
    def _():
        s_scr[...] = s0_ref[0] if has_state else jnp.zeros(s_scr.shape, F32)

    cos, sin = cos_ref[...], sin_ref[...]
    half = HEAD_DIM // 2
    for h in range(N_HEADS):
        lo = h * HEAD_DIM
        q = z_ref[:, lo:lo + HEAD_DIM]
        k = z_ref[:, D_MIX + lo:D_MIX + lo + HEAD_DIM]
        v = z_ref[:, 2 * D_MIX + lo:2 * D_MIX + lo + HEAD_DIM]
        g = z_ref[:, 3 * D_MIX + lo:3 * D_MIX + lo + HEAD_DIM]
        q = q * cos + pltpu.roll(q, half, 1) * sin
        k = (k * cos + pltpu.roll(k, half, 1) * sin) * HEAD_DIM ** -0.5
        s = s_scr[h]
        att = _mm_nt(q, k) * di_ref[h]
        o = _mm(att, v) + _mm(q * dq_ref[h], s)
        s_scr[h] = s * ds_ref[h, 0:1, :] + _mm_tn(k * dk_ref[h], v)
        xc = o - jnp.mean(o, axis=-1, keepdims=True)
        on = xc * lax.rsqrt(jnp.mean(xc * xc, axis=-1, keepdims=True) + GN_EPS)
        y_ref[:, lo:lo + HEAD_DIM] = (on * _silu(g)).astype(y_ref.dtype)

    @pl.when(n == nc - 1)
    def _():
        so_ref[0] = s_scr[...]


def _rope_tables(length, pos0):
    half = HEAD_DIM // 2
    inv = ROPE_BASE ** (-jnp.arange(half, dtype=F32) / half)
    pos = jnp.arange(length, dtype=F32) + pos0
    ang = pos[:, None] * inv[None, :]
    cos, sin = jnp.cos(ang), jnp.sin(ang)
    return jnp.concatenate([cos, cos], axis=1), jnp.concatenate([-sin, sin], axis=1)


def _row_block(row0, c, nc):
    off = row0 // c
    return lambda b, n: off + b * nc + n


def _retention(z, row0, bsz, length, c, pos0, state):
    nc = length // c
    rb = _row_block(row0, c, nc)
    cos, sin = _rope_tables(length, pos0)
    di, dq, dk, ds = _ret_tables(c)
    has_state = state is not None
    const3 = lambda b, n: (0, 0, 0)
    args = [z, cos, sin, di, dq, dk, ds]
    specs = [pl.BlockSpec((c, MIX_W), lambda b, n: (rb(b, n), Z_RET // MIX_W)),
             pl.BlockSpec((c, HEAD_DIM), lambda b, n: (n, 0)),
             pl.BlockSpec((c, HEAD_DIM), lambda b, n: (n, 0)),
             pl.BlockSpec(di.shape, const3), pl.BlockSpec(dq.shape, const3),
             pl.BlockSpec(dk.shape, const3), pl.BlockSpec(ds.shape, const3)]
    sshape = (N_HEADS, HEAD_DIM, HEAD_DIM)
    if has_state:
        args.append(state)
        specs.append(pl.BlockSpec((1,) + sshape, lambda b, n: (b, 0, 0, 0)))
    return pl.pallas_call(
        functools.partial(_ret_kernel, c=c, nc=nc, has_state=has_state),
        grid=(bsz, nc), in_specs=specs,
        out_specs=[pl.BlockSpec((c, D_MIX), lambda b, n: (b * nc + n, 0)),
                   pl.BlockSpec((1,) + sshape, lambda b, n: (b, 0, 0, 0))],
        out_shape=[jax.ShapeDtypeStruct((bsz * length, D_MIX), BF16),
                   jax.ShapeDtypeStruct((bsz,) + sshape, F32)],
        scratch_shapes=[pltpu.VMEM(sshape, F32)],
        compiler_params=_cparams(("arbitrary", "arbitrary")), name="retention",
    )(*args)


SUB = 16


def _gla_head(q, k, v, g, st, c):
    big = _cumsum_rows(g)
    gl = big[c - 1:c, :]
    o = _mm_nt(q * jnp.exp(big), st)
    st_new = st * jnp.exp(gl) + _mm_tn(v, k * jnp.exp(gl - big))

    row = lax.broadcasted_iota(jnp.int32, (c, HEAD_DIM), 0)
    r2 = lax.broadcasted_iota(jnp.int32, (c, c), 0)
    c2 = lax.broadcasted_iota(jnp.int32, (c, c), 1)
    att = jnp.zeros((c, c), F32)
    m, sh = SUB, 4
    while m < c:
        anchor = jnp.concatenate(
            [jnp.broadcast_to(big[p * 2 * m + m - 1:p * 2 * m + m, :], (2 * m, HEAD_DIM))
             for p in range(c // (2 * m))], axis=0)
        right = ((row >> sh) & 1) == 1
        qt = jnp.where(right, q * jnp.exp(jnp.where(right, big - anchor, 0.0)), 0.0)
        kt = jnp.where(right, 0.0, k * jnp.exp(jnp.where(right, 0.0, anchor - big)))
        att = att + jnp.where((r2 >> (sh + 1)) == (c2 >> (sh + 1)), _mm_nt(qt, kt), 0.0)
        m, sh = m * 2, sh + 1
    o = o + _mm(att, v)

    f = jnp.exp(g)
    rsub = lax.broadcasted_iota(jnp.int32, (SUB, HEAD_DIM), 0)
    diag = []
    for blk in range(c // SUB):
        r0 = blk * SUB
        vb = v[r0:r0 + SUB, :]
        w = jnp.zeros((SUB, HEAD_DIM), F32)
        ob = jnp.zeros((SUB, HEAD_DIM), F32)
        for tl in range(SUB):
            t = r0 + tl
            if tl:
                w = w * f[t:t + 1, :]
            w = jnp.where(rsub == tl, k[t:t + 1, :], w)
            a_col = jnp.sum(w * q[t:t + 1, :], axis=1, keepdims=True)
            o_t = jnp.sum(a_col * vb, axis=0, keepdims=True)
            ob = jnp.where(rsub == tl, o_t, ob)
        diag.append(ob)
    return o + jnp.concatenate(diag, axis=0), st_new


def _hgrn_kernel(*refs, c, nc, has_state):
    it = iter(refs)
    z_ref, llb_ref, l1m_ref, oml_ref, gn_ref = (next(it) for _ in range(5))
    s0_ref = next(it) if has_state else None
    y_ref, so_ref, s_scr = next(it), next(it), next(it)
    n = pl.program_id(1)

    @pl.when(n == 0)
    def _():
        for h in range(N_HEADS):
            s_scr[h] = s0_ref[0, h].T if has_state else jnp.zeros((HEAD_DIM, HEAD_DIM), F32)

    for h in range(N_HEADS):
        lo = h * HEAD_DIM
        q = z_ref[:, lo:lo + HEAD_DIM]
        f = z_ref[:, D_MIX + lo:D_MIX + lo + HEAD_DIM]
        iv = z_ref[:, 2 * D_MIX + lo:2 * D_MIX + lo + HEAD_DIM]
        g = z_ref[:, 3 * D_MIX + lo:3 * D_MIX + lo + HEAD_DIM]
        ls = jnp.minimum(f, 0.0) - jnp.log1p(jnp.exp(-jnp.abs(f)))
        a = llb_ref[:, lo:lo + HEAD_DIM]
        b = l1m_ref[:, lo:lo + HEAD_DIM] + ls
        logf = jnp.maximum(a, b) + jnp.log1p(jnp.exp(-jnp.abs(a - b)))
        kf = oml_ref[:, lo:lo + HEAD_DIM] * jax.nn.sigmoid(-f)
        o, st = _gla_head(q, kf, iv, logf, s_scr[h], c)
        s_scr[h] = st
        y = _rms_rows(o, gn_ref[...]) * _silu(g)
        y_ref[:, lo:lo + HEAD_DIM] = y.astype(y_ref.dtype)

    @pl.when(n == nc - 1)
    def _():
        for h in range(N_HEADS):
            so_ref[0, h] = s_scr[h].T


def _hgrn(z, row0, bsz, length, c, lb, gnorm, state):
    nc = length // c
    rb = _row_block(row0, c, nc)
    has_state = state is not None
    vec = pl.BlockSpec((1, D_MIX), lambda b, n: (0, 0))
    lb = lb.reshape(1, D_MIX)
    args = [z, jnp.log(lb), jnp.log1p(-lb), 1.0 - lb, gnorm.reshape(1, HEAD_DIM)]
    specs = [pl.BlockSpec((c, MIX_W), lambda b, n: (rb(b, n), Z_HGRN // MIX_W)), vec, vec, vec,
             pl.BlockSpec((1, HEAD_DIM), lambda b, n: (0, 0))]
    sshape = (N_HEADS, HEAD_DIM, HEAD_DIM)
    if has_state:
        args.append(state)
        specs.append(pl.BlockSpec((1,) + sshape, lambda b, n: (b, 0, 0, 0)))
    return pl.pallas_call(
        functools.partial(_hgrn_kernel, c=c, nc=nc, has_state=has_state),
        grid=(bsz, nc), in_specs=specs,
        out_specs=[pl.BlockSpec((c, D_MIX), lambda b, n: (b * nc + n, 0)),
                   pl.BlockSpec((1,) + sshape, lambda b, n: (b, 0, 0, 0))],
        out_shape=[jax.ShapeDtypeStruct((bsz * length, D_MIX), BF16),
                   jax.ShapeDtypeStruct((bsz,) + sshape, F32)],
        scratch_shapes=[pltpu.VMEM(sshape, F32)],
        compiler_params=_cparams(("arbitrary", "arbitrary")), name="hgrn2",
    )(*args)


def _rwkv_kernel(*refs, c, nsub, nc, has_state):
    it = iter(refs)
    (z_ref, mu_ref, w0_ref, w2_ref, a0_ref, a2_ref, g2_ref, kk_ref, ka_ref, rk_ref,
     lnw_ref, lnb_ref) = (next(it) for _ in range(12))
    if has_state:
        sh0_ref, s0_ref = next(it), next(it)
    y_ref, so_ref, s_scr, prev_scr = next(it), next(it), next(it), next(it)
    n = pl.program_id(1)
    nh, hd = RWKV_HEADS, RWKV_N

    @pl.when(n == 0)
    def _():
        s_scr[...] = s0_ref[0] if has_state else jnp.zeros(s_scr.shape, F32)
        prev_scr[...] = (jnp.broadcast_to(sh0_ref[0], prev_scr.shape) if has_state
                         else jnp.zeros(prev_scr.shape, F32))

    z = z_ref[...]
    rows = nsub * c
    row = lax.broadcasted_iota(jnp.int32, z.shape, 0)
    prev = jnp.where(row == 0, prev_scr[0:1, :], pltpu.roll(z, 1, 0))
    prev_scr[...] = jnp.broadcast_to(z[rows - 1:rows, :], prev_scr.shape)
    xs = z + (prev - z) * mu_ref[...]
    o3 = 3 * D_MIX
    r, k, v = xs[:, :D_MIX], xs[:, D_MIX:2 * D_MIX], xs[:, 2 * D_MIX:o3]
    w_lo, a_lo, g_lo = xs[:, o3:o3 + 64], xs[:, o3 + 64:o3 + 128], xs[:, o3 + 128:o3 + 256]
    w = -_softplus(-(w0_ref[...] + _mm(jnp.tanh(w_lo), w2_ref[...]))) - 0.5
    ld = -jnp.exp(w)
    a = jax.nn.sigmoid(a0_ref[...] + _mm(a_lo, a2_ref[...]))
    g = _mm(jax.nn.sigmoid(g_lo), g2_ref[...])
    kkv = k * kk_ref[...]
    k = k * (1.0 + (a - 1.0) * ka_ref[...])
    bonus = r * k * rk_ref[...]

    r2 = lax.broadcasted_iota(jnp.int32, (2 * c, 2 * c), 0)
    c2 = lax.broadcasted_iota(jnp.int32, (2 * c, 2 * c), 1)
    tq = jnp.where(r2 >= c, r2 - c, r2)
    ts = jnp.where(c2 >= c, c2 - c, c2)
    keep = (tq > ts) | ((r2 >= c) & (tq == ts))
    heads, subs = range(nh), range(nsub)
    pairs = [(j, h) for j in subs for h in heads]
    sls = [slice(h * hd, (h + 1) * hd) for h in heads]
    lhs, rhs, vs, gam = {}, {}, {}, {}
    for j in subs:
        rs = slice(j * c, (j + 1) * c)
        ldj = ld[rs]
        lc = _cumsum_rows(ldj)
        e_in, e_in_neg, e_ex = jnp.exp(lc), jnp.exp(-lc), jnp.exp(lc - ldj)
        for h in heads:
            sl = sls[h]
            kkh = kkv[rs, sl]
            kap = kkh * lax.rsqrt(jnp.sum(kkh * kkh, axis=-1, keepdims=True) + 1e-6)
            lhs[j, h] = jnp.concatenate([kap * e_ex[:, sl], r[rs, sl] * e_in[:, sl]], axis=0)
            rhs[j, h] = jnp.concatenate([kap * a[rs, sl] * e_in_neg[:, sl], k[rs, sl] * e_in_neg[:, sl]], axis=0)
            vs[j, h] = v[rs, sl]
            gam[j, h] = e_in[c - 1:c, sl]
    pm = {p: jnp.where(keep, _mm_nt(lhs[p], rhs[p]), 0.0) for p in pairs}
    zv = jnp.zeros((c, hd), F32)
    mkv = {p: _mm(pm[p][:c, :], jnp.concatenate([zv, vs[p]], axis=0)) for p in pairs}
    t_inv = dict(zip(pairs, _tri_inv([pm[p][:c, :c] for p in pairs], c)))
    s_cur = [s_scr[h] for h in heads]
    ys = {}
    for j in subs:
        qs = [_mm_nt(lhs[j, h], s_cur[h]) for h in heads]
        us = [-_mm(t_inv[j, h], qs[h][:c, :] + mkv[j, h]) for h in heads]
        uv = [jnp.concatenate([us[h], vs[j, h]], axis=0) for h in heads]
        for h in heads:
            ys[j, h] = qs[h][c:, :] + _mm(pm[j, h][c:, :], uv[h])
        s_cur = [(s_cur[h] + _mm_tn(uv[h], rhs[j, h])) * gam[j, h] for h in heads]
    for h in heads:
        s_scr[h] = s_cur[h]
    for j, h in pairs:
        rs, sl = slice(j * c, (j + 1) * c), sls[h]
        y = ys[j, h]
        yc = y - jnp.mean(y, axis=-1, keepdims=True)
        yn = yc * lax.rsqrt(jnp.mean(yc * yc, axis=-1, keepdims=True) + RWKV_GN_EPS)
        yn = yn * lnw_ref[:, sl] + lnb_ref[:, sl]
        yn = yn + jnp.sum(bonus[rs, sl], axis=-1, keepdims=True) * vs[j, h]
        y_ref[rs, sl] = (yn * g[rs, sl]).astype(y_ref.dtype)

    @pl.when(n == nc - 1)
    def _():
        so_ref[0] = s_scr[...]


def _sub_chunks(length, c, most):
    nc = length // c
    while nc % most:
        most //= 2
    return most


def _rwkv(z, row0, bsz, length, c, p, shift, state):
    nsub = _sub_chunks(length, c, 2)
    cb = c * nsub
    nc = length // cb
    rb = _row_block(row0, cb, nc)
    has_state = state is not None
    pad = MIX_W - RWKV_COLS
    c2 = lambda b, n: (0, 0)
    args = [z, jnp.pad(p["mu"], (0, pad)).reshape(1, MIX_W),
            p["w0"].reshape(1, D_MIX), p["w2"], p["a0"].reshape(1, D_MIX), p["a2"], p["g2"],
            p["k_k"].reshape(1, D_MIX), p["k_a"].reshape(1, D_MIX), p["r_k"].reshape(1, D_MIX),
            p["ln_w"].reshape(1, D_MIX), p["ln_b"].reshape(1, D_MIX)]
    specs = [pl.BlockSpec((cb, MIX_W), lambda b, n: (rb(b, n), Z_RWKV // MIX_W))]
    specs += [pl.BlockSpec(a.shape, c2) for a in args[1:]]
    sshape = (RWKV_HEADS, RWKV_N, RWKV_N)
    if has_state:
        args += [jnp.pad(shift, ((0, 0), (0, pad))).reshape(bsz, 1, MIX_W), state]
        specs += [pl.BlockSpec((1, 1, MIX_W), lambda b, n: (b, 0, 0)),
                  pl.BlockSpec((1,) + sshape, lambda b, n: (b, 0, 0, 0))]
    return pl.pallas_call(
        functools.partial(_rwkv_kernel, c=c, nsub=nsub, nc=nc, has_state=has_state),
        grid=(bsz, nc), in_specs=specs,
        out_specs=[pl.BlockSpec((cb, D_MIX), lambda b, n: (b * nc + n, 0)),
                   pl.BlockSpec((1,) + sshape, lambda b, n: (b, 0, 0, 0))],
        out_shape=[jax.ShapeDtypeStruct((bsz * length, D_MIX), BF16),
                   jax.ShapeDtypeStruct((bsz,) + sshape, F32)],
        scratch_shapes=[pltpu.VMEM(sshape, F32), pltpu.VMEM((8, MIX_W), F32)],
        compiler_params=_cparams(("arbitrary", "arbitrary")), name="rwkv7",
    )(*args)


CONV_PAD = 8


def _gdn_kernel(*refs, c, nsub, nc, has_state):
    it = iter(refs)
    z_ref, ab_ref, cw_ref, alog_ref, dtb_ref, gn_ref = (next(it) for _ in range(6))
    if has_state:
        cv0_ref, s0_ref = next(it), next(it)
    y_ref, so_ref, s_scr, xp_scr = next(it), next(it), next(it), next(it)
    n = pl.program_id(1)
    qkv_w = 3 * D_MIX
    tail = GDN_CONV - 1
    rows = nsub * c

    @pl.when(n == 0)
    def _():
        s_scr[...] = s0_ref[0] if has_state else jnp.zeros(s_scr.shape, F32)
        xp_scr[0:CONV_PAD, :] = jnp.zeros((CONV_PAD, qkv_w), F32)
        if has_state:
            xp_scr[CONV_PAD - tail:CONV_PAD, :] = cv0_ref[0]

    xp_scr[CONV_PAD:CONV_PAD + rows, :] = z_ref[:, :qkv_w]
    conv = xp_scr[CONV_PAD - tail:CONV_PAD - tail + rows, :] * cw_ref[0:1, :]
    for j in range(1, GDN_CONV):
        conv = conv + xp_scr[CONV_PAD - tail + j:CONV_PAD - tail + j + rows, :] * cw_ref[j:j + 1, :]
    xp_scr[CONV_PAD - tail:CONV_PAD, :] = xp_scr[CONV_PAD + rows - tail:CONV_PAD + rows, :]
    act = _silu(conv)

    ab = ab_ref[...]
    gdec = -jnp.exp(alog_ref[...]) * _softplus(ab + dtb_ref[...])
    beta = jax.nn.sigmoid(ab)

    r2 = lax.broadcasted_iota(jnp.int32, (c, c), 0)
    c2 = lax.broadcasted_iota(jnp.int32, (c, c), 1)
    eye, causal, strict = r2 == c2, r2 >= c2, r2 > c2
    heads, subs = range(N_HEADS), range(nsub)
    pairs = [(j, h) for j in subs for h in heads]
    kq, ks, vs, gcols, bcols, decs = {}, {}, {}, {}, {}, {}
    for j in subs:
        rs = slice(j * c, (j + 1) * c)
        gcum = _cumsum_rows(gdec[rs])
        for h in heads:
            lo = h * HEAD_DIM
            q = act[rs, lo:lo + HEAD_DIM]
            k = act[rs, D_MIX + lo:D_MIX + lo + HEAD_DIM]
            q = q * lax.rsqrt(jnp.sum(q * q, axis=-1, keepdims=True) + 1e-6) * HEAD_DIM ** -0.5
            k = k * lax.rsqrt(jnp.sum(k * k, axis=-1, keepdims=True) + 1e-6)
            gcol = gcum[:, h:h + 1]
            rel = gcol - _row_from_col(gcol, eye)
            kq[j, h] = jnp.concatenate([k, q], axis=0)
            ks[j, h] = k
            vs[j, h] = act[rs, 2 * D_MIX + lo:2 * D_MIX + lo + HEAD_DIM]
            gcols[j, h] = gcol
            bcols[j, h] = beta[rs, N_HEADS + h:N_HEADS + h + 1]
            decs[j, h] = jnp.where(causal, jnp.exp(jnp.where(causal, rel, 0.0)), 0.0)
    kk_qk = {p: _mm_nt(kq[p], ks[p]) for p in pairs}
    t_inv = dict(zip(pairs, _tri_inv(
        [jnp.where(strict, bcols[p] * kk_qk[p][:c] * decs[p], 0.0) for p in pairs], c)))
    s_cur = [s_scr[h] for h in heads]
    os_ = {}
    for j in subs:
        ks_qs = [_mm(kq[j, h], s_cur[h]) for h in heads]
        egs = [jnp.exp(gcols[j, h]) for h in heads]
        us = [_mm(t_inv[j, h], bcols[j, h] * (vs[j, h] - egs[h] * ks_qs[h][:c])) for h in heads]
        for h in heads:
            os_[j, h] = egs[h] * ks_qs[h][c:] + _mm(kk_qk[j, h][c:] * decs[j, h], us[h])
        gls = [gcols[j, h][c - 1:c, :] for h in heads]
        s_cur = [jnp.exp(gls[h]) * s_cur[h] + _mm_tn(ks[j, h] * jnp.exp(gls[h] - gcols[j, h]), us[h])
                 for h in heads]
    for h in heads:
        s_scr[h] = s_cur[h]
    for j, h in pairs:
        rs, lo = slice(j * c, (j + 1) * c), h * HEAD_DIM
        zg = z_ref[rs, qkv_w + lo:qkv_w + lo + HEAD_DIM]
        y = _rms_rows(os_[j, h], gn_ref[...]) * _silu(zg)
        y_ref[rs, lo:lo + HEAD_DIM] = y.astype(y_ref.dtype)

    @pl.when(n == nc - 1)
    def _():
        so_ref[0] = s_scr[...]


def _gdn(z, row0, bsz, length, c, p, conv_state, state):
    nsub = _sub_chunks(length, c, 4)
    cb = c * nsub
    nc = length // cb
    rb = _row_block(row0, cb, nc)
    has_state = state is not None
    c2 = lambda b, n: (0, 0)
    lane_pad = lambda a: jnp.pad(a, (0, HEAD_DIM - a.shape[0])).reshape(1, HEAD_DIM)
    args = [z, z, p["conv_w"], lane_pad(p["A_log"]), lane_pad(p["dt_bias"]), p["norm_g"].reshape(1, HEAD_DIM)]
    specs = [pl.BlockSpec((cb, MIX_W), lambda b, n: (rb(b, n), Z_GDN // MIX_W)),
             pl.BlockSpec((cb, HEAD_DIM), lambda b, n: (rb(b, n), Z_AB // HEAD_DIM))]
    specs += [pl.BlockSpec(a.shape, c2) for a in args[2:]]
    sshape = (N_HEADS, HEAD_DIM, HEAD_DIM)
    if has_state:
        args += [conv_state, state]
        specs += [pl.BlockSpec((1, GDN_CONV - 1, 3 * D_MIX), lambda b, n: (b, 0, 0)),
                  pl.BlockSpec((1,) + sshape, lambda b, n: (b, 0, 0, 0))]
    return pl.pallas_call(
        functools.partial(_gdn_kernel, c=c, nsub=nsub, nc=nc, has_state=has_state),
        grid=(bsz, nc), in_specs=specs,
        out_specs=[pl.BlockSpec((cb, D_MIX), lambda b, n: (b * nc + n, 0)),
                   pl.BlockSpec((1,) + sshape, lambda b, n: (b, 0, 0, 0))],
        out_shape=[jax.ShapeDtypeStruct((bsz * length, D_MIX), BF16),
                   jax.ShapeDtypeStruct((bsz,) + sshape, F32)],
        scratch_shapes=[pltpu.VMEM(sshape, F32), pltpu.VMEM((CONV_PAD + cb, 3 * D_MIX), F32)],
        compiler_params=_cparams(("arbitrary", "arbitrary")), name="gdn",
    )(*args)


def _merge_kernel(y0_ref, y1_ref, y2_ref, y3_ref, gate_ref, x_ref, gt_ref, sc_ref, sh_ref, g_ref,
                  wb_ref, wo_ref, rw_ref, rb_ref, xo_ref, h_ref, lg_ref, *, gpt):
    tm, d = x_ref.shape
    merged = jnp.zeros((tm, d), F32)
    for nb, y_ref in enumerate((y0_ref, y1_ref, y2_ref, y3_ref)):
        br = jnp.dot(y_ref[...], wb_ref[0, nb], preferred_element_type=F32)
        merged = merged + gate_ref[:, nb * d:(nb + 1) * d].astype(F32) * br
    m = jnp.dot(merged.astype(BF16), wo_ref[0], preferred_element_type=F32)
    gt = gt_ref[0]
    x = x_ref[...] + (m.reshape(gpt, GROUP, d) * gt[:, None, :]).reshape(tm, d)
    xo_ref[...] = x
    h = _modulate(_rms_rows(x, g_ref[...]), sc_ref[0], sh_ref[0], gpt)
    _rows_to_chunks(h_ref, h)
    lg_ref[...] = _mm(h, rw_ref[0]) + rb_ref[0]


def _merge(ys, gates, x, modg, l, g2, wb, wo, rw, rb, n_prompt):
    t, d = x.shape
    tm = 256
    gpt, gidx = _group_block(tm, n_prompt // tm)
    row = lambda w: pl.BlockSpec((tm, w), lambda i: (i, 0))

    def mod_spec(col):
        return pl.BlockSpec((1, gpt, d), lambda i: (l, gidx(i), col))

    def const(shape):
        return pl.BlockSpec(shape, lambda i: (0,) * len(shape), pipeline_mode=pl.Buffered(1))

    def layer(arr):
        return pl.BlockSpec((1,) + arr.shape[1:], lambda i: (l,) + (0,) * (arr.ndim - 1),
                            pipeline_mode=pl.Buffered(1))
    nr = rw.shape[2]
    return pl.pallas_call(
        functools.partial(_merge_kernel, gpt=gpt),
        grid=(t // tm,),
        in_specs=[row(D_MIX)] * 4 + [row(N_BRANCH * d), row(d), mod_spec(2), mod_spec(4), mod_spec(3),
                                     const((1, d)), layer(wb), layer(wo), layer(rw), layer(rb)],
        out_specs=[row(d), pl.BlockSpec((tm * ROW_CHUNKS, LANES), lambda i: (i, 0)), row(nr)],
        out_shape=[jax.ShapeDtypeStruct((t, d), F32), jax.ShapeDtypeStruct((t * ROW_CHUNKS, LANES), F32),
                   jax.ShapeDtypeStruct((t, nr), F32)],
        compiler_params=_cparams(("arbitrary",)), name="merge",
    )(*ys, gates, x, modg, modg, modg, g2.reshape(1, d), wb, wo, rw, rb)


MOE_BLK = 256
MOE_UNROLL = 8


def _moe_kernel(be_ref, tok_ref, slot_ref, nused_ref, h_hbm, w_ref, wg_ref, wu_ref, wd_ref, o_hbm,
                x0, x1, y0, y1, wg_bf, wu_bf, wd_bf, gsem, ssem):
    i = pl.program_id(0)
    nused = nused_ref[0]

    rc = ROW_CHUNKS

    def gather(blk, xb, q, r):
        return pltpu.make_async_copy(h_hbm.at[pl.ds(tok_ref[blk * MOE_BLK + r] * rc, rc)],
                                     xb.at[pl.ds(r * rc, rc)], gsem.at[q])

    def scatter(blk, yb, q, r):
        return pltpu.make_async_copy(yb.at[pl.ds(r * rc, rc)],
                                     o_hbm.at[pl.ds(slot_ref[(blk + 1) * MOE_BLK + r] * rc, rc)], ssem.at[q])

    def for_rows(fn):
        def group(j, carry):
            for u in range(MOE_UNROLL):
                fn(j * MOE_UNROLL + u)
            return carry
        lax.fori_loop(0, MOE_BLK // MOE_UNROLL, group, 0)

    def step(q, xa, ya, xb, yb):
        @pl.when(i == 0)
        def _():
            yb[...] = jnp.zeros(yb.shape, F32)
            n_real = o_hbm.shape[0] - 2 * MOE_BLK * rc
            init = pltpu.make_async_copy(yb, o_hbm.at[pl.ds(n_real, MOE_BLK * rc)], ssem.at[q])
            init.start()
            init.wait()
            for_rows(lambda r: gather(i, xa, q, r).start())

        @pl.when(i > 0)
        def _():
            for_rows(lambda r: scatter(i - 2, ya, q, r).wait())
        for_rows(lambda r: gather(i, xa, q, r).wait())

        @pl.when((i == 0) | (be_ref[i] != be_ref[jnp.maximum(i - 1, 0)]))
        def _():
            wg_bf[...] = wg_ref[0, 0].astype(BF16)
            wu_bf[...] = wu_ref[0, 0].astype(BF16)
            wd_bf[...] = wd_ref[0, 0].astype(BF16)

        for r in range(MOE_BLK):
            gather(i + 1, xb, 1 - q, r).start()
        for r in range(MOE_BLK):
            scatter(i - 1, yb, 1 - q, r).start()
        x = _rows_from_chunks(xa, MOE_BLK).astype(BF16)
        hid = _silu(jnp.dot(x, wg_bf[...], preferred_element_type=F32)) * jnp.dot(
            x, wu_bf[...], preferred_element_type=F32)
        _rows_to_chunks(ya, jnp.dot(hid.astype(BF16), wd_bf[...], preferred_element_type=F32) * w_ref[...])

        @pl.when(i == nused - 1)
        def _():
            for_rows(lambda r: scatter(i, ya, q, r).start())
            for_rows(lambda r: scatter(i - 1, yb, 1 - q, r).wait())
            for_rows(lambda r: scatter(i, ya, q, r).wait())
            for_rows(lambda r: gather(i + 1, xb, 1 - q, r).wait())

    @pl.when((i < nused) & (i % 2 == 0))
    def _():
        step(0, x0, y0, x1, y1)

    @pl.when((i < nused) & (i % 2 == 1))
    def _():
        step(1, x1, y1, x0, y0)


def _route(logits, n_tok):
    lg = logits[:, :N_GROUPS]
    pg = jax.nn.softmax(lg, axis=-1)
    gsel = jnp.argmax(lg, axis=-1).astype(jnp.int32)
    gw = jnp.take_along_axis(pg, gsel[:, None], axis=-1)
    le = logits[:, N_GROUPS:N_GROUPS + N_EXPERTS].reshape(n_tok, N_GROUPS, EXPERTS_PER_GROUP)
    le = jnp.take_along_axis(le, gsel[:, None, None], axis=1)[:, 0]
    top_v, top_i = lax.top_k(jax.nn.softmax(le, axis=-1), TOPK)
    wts = top_v / jnp.sum(top_v, axis=-1, keepdims=True) * gw
    eid = (gsel[:, None] * EXPERTS_PER_GROUP + top_i).reshape(-1).astype(jnp.int32)
    a = n_tok * TOPK
    order = jnp.argsort(eid).astype(jnp.int32)
    counts = jnp.sum((eid[None, :] == jnp.arange(N_EXPERTS, dtype=jnp.int32)[:, None]).astype(jnp.int32), axis=1)
    padded = (counts + MOE_BLK - 1) // MOE_BLK * MOE_BLK
    pend = jnp.cumsum(padded)
    pstart = pend - padded
    cstart = jnp.cumsum(counts) - counts
    n_blocks = -(-a // MOE_BLK) + N_EXPERTS
    rows = n_blocks * MOE_BLK
    blk_row0 = jnp.arange(n_blocks, dtype=jnp.int32) * MOE_BLK
    block_e = jnp.minimum(jnp.sum((pend[None, :] <= blk_row0[:, None]).astype(jnp.int32), axis=1),
                          N_EXPERTS - 1).astype(jnp.int32)
    nused = (pend[-1:] // MOE_BLK).astype(jnp.int32)
    e_row = jnp.repeat(block_e, MOE_BLK)
    row_id = jnp.arange(rows, dtype=jnp.int32)
    j_row = row_id - pstart[e_row]
    valid = j_row < counts[e_row]
    asg = order[jnp.clip(cstart[e_row] + j_row, 0, a - 1)]
    tok = jnp.where(valid, asg // TOPK, 0)
    dummy = a + ((row_id // MOE_BLK) % 2) * MOE_BLK + row_id % MOE_BLK
    slot = jnp.where(valid, (asg % TOPK) * n_tok + asg // TOPK, dummy)
    wrow = jnp.where(valid, wts.reshape(-1)[asg], 0.0)
    tok = jnp.concatenate([tok, jnp.zeros((MOE_BLK,), jnp.int32)])
    slot = jnp.concatenate([a + MOE_BLK + jnp.arange(MOE_BLK, dtype=jnp.int32), slot])
    return block_e, tok, slot, nused, wrow.reshape(rows, 1), n_blocks


def _moe(h, logits, wg, wu, wd, l):
    t, d = h.shape[0] // ROW_CHUNKS, D_MODEL
    block_e, tok, slot, nused, wrow, n_blocks = _route(logits, t)
    grid_spec = pltpu.PrefetchScalarGridSpec(
        num_scalar_prefetch=4, grid=(n_blocks,),
        in_specs=[pl.BlockSpec(memory_space=pl.ANY),
                  pl.BlockSpec((MOE_BLK, 1), lambda i, be, *_: (i, 0)),
                  pl.BlockSpec((1, 1, d, D_EXPERT), lambda i, be, *_: (l, be[i], 0, 0)),
                  pl.BlockSpec((1, 1, d, D_EXPERT), lambda i, be, *_: (l, be[i], 0, 0)),
                  pl.BlockSpec((1, 1, D_EXPERT, d), lambda i, be, *_: (l, be[i], 0, 0))],
        out_specs=pl.BlockSpec(memory_space=pl.ANY),
        scratch_shapes=[pltpu.VMEM((MOE_BLK * ROW_CHUNKS, LANES), F32)] * 4 + [
                        pltpu.VMEM((d, D_EXPERT), BF16), pltpu.VMEM((d, D_EXPERT), BF16),
                        pltpu.VMEM((D_EXPERT, d), BF16),
                        pltpu.SemaphoreType.DMA((2,)), pltpu.SemaphoreType.DMA((2,))])
    out = pl.pallas_call(
        _moe_kernel, grid_spec=grid_spec,
        out_shape=jax.ShapeDtypeStruct(((t * TOPK + 2 * MOE_BLK) * ROW_CHUNKS, LANES), F32),
        compiler_params=_cparams(("arbitrary",)), name="moe",
    )(block_e, tok, slot, nused, h, wrow, wg, wu, wd)
    return out


def _pack_w_in(w_in):
    depth, d, _ = w_in.shape
    o_rwkv, o_gdn = 4096, 4096 + RWKV_COLS
    o_ab = o_gdn + 4 * D_MIX
    o_gate = o_ab + 2 * N_HEADS
    zeros = lambda n: jnp.zeros((depth, d, n), BF16)
    part = lambda lo, hi: w_in[:, :, lo:hi].astype(BF16)
    w_main = jnp.concatenate(
        [part(0, o_gdn), zeros(Z_GDN - o_gdn), part(o_gdn, o_ab), part(o_ab, o_gate),
         zeros(N_MAIN - Z_AB - 2 * N_HEADS)], axis=2)
    return w_main, part(o_gate, w_in.shape[2])


def _mixers(z, l, groups, params):
    outs = [[] for _ in range(N_BRANCH)]
    states = []
    for (row0, bsz, length, c, pos0, st) in groups:
        s_ret, s_hgrn, s_rwkv, s_shift, s_gdn, s_conv = st if st is not None else (None,) * 6
        y_a, n_ret = _retention(z, row0, bsz, length, min(length, RET_CHUNK), pos0, s_ret)
        y_b, n_hgrn = _hgrn(z, row0, bsz, length, c, params["hgrn_lb"][l], params["hgrn_norm_g"][l], s_hgrn)
        y_c, n_rwkv = _rwkv(z, row0, bsz, length, c, {k: v[l] for k, v in params["rwkv"].items()},
                            s_shift, s_rwkv)
        y_d, n_gdn = _gdn(z, row0, bsz, length, c, {k: v[l] for k, v in params["gdn"].items()},
                          s_conv, s_gdn)
        for lst, y in zip(outs, (y_a, y_b, y_c, y_d)):
            lst.append(y)
        def seq_row(j, col0, width):
            return lax.slice(z, (row0 + j, col0), (row0 + (bsz - 1) * length + j + 1, col0 + width),
                             (length, 1))
        n_shift = seq_row(length - 1, Z_RWKV, RWKV_COLS)
        n_conv = jnp.stack([seq_row(length - (GDN_CONV - 1) + j, Z_GDN, 3 * D_MIX)
                            for j in range(GDN_CONV - 1)], axis=1)
        states.append((n_ret, n_hgrn, n_rwkv, n_shift, n_gdn, n_conv))
    return [jnp.concatenate(lst, axis=0) for lst in outs], states


def kernel(x_prompt, x_sample, c_prompt, c_sample, state_ret, state_hgrn, state_rwkv, state_rwkv_shift,
           state_gdn, state_gdn_conv, ada_w, ada_b, norm1_g, norm2_g, w_in, hgrn_lb_logits, hgrn_norm_g,
           rwkv_mu, rwkv_w0, rwkv_w2, rwkv_a0, rwkv_a2, rwkv_g2, rwkv_k_k, rwkv_k_a, rwkv_r_k, rwkv_ln_w,
           rwkv_ln_b, gdn_conv_w, gdn_A_log, gdn_dt_bias, gdn_norm_g, w_branch, w_out, router_g, router_g_b,
           router_e, router_e_b, moe_w_gate, moe_w_up, moe_w_down, final_norm_g):
    depth = ada_w.shape[0]
    bp, lp, d = x_prompt.shape
    bs, ls, _ = x_sample.shape
    n_prompt = bp * lp
    assert bp == 1 and ls == GROUP and bs == 16

    lb_cum = jnp.cumsum(jax.nn.softmax(hgrn_lb_logits.astype(F32), axis=0), axis=0)
    params = dict(
        hgrn_lb=lb_cum - lb_cum[:1], hgrn_norm_g=hgrn_norm_g,
        rwkv=dict(mu=rwkv_mu, w0=rwkv_w0, w2=rwkv_w2, a0=rwkv_a0, a2=rwkv_a2, g2=rwkv_g2, k_k=rwkv_k_k,
                  k_a=rwkv_k_a, r_k=rwkv_r_k.reshape(depth, D_MIX), ln_w=rwkv_ln_w, ln_b=rwkv_ln_b),
        gdn=dict(conv_w=gdn_conv_w, A_log=gdn_A_log, dt_bias=gdn_dt_bias, norm_g=gdn_norm_g))

    c_all = jnp.concatenate([c_prompt, c_sample, jnp.zeros((24 - bp - bs, d), F32)], axis=0)
    mod = _ada(c_all, ada_w, ada_b)
    modg = jnp.concatenate([jnp.broadcast_to(mod[:, :1], (depth, 16, 6 * d)), mod[:, 1:1 + bs]], axis=1)

    w_main, w_gate = _pack_w_in(w_in)
    wb_bf, wo_bf = w_branch.astype(BF16), w_out.astype(BF16)
    n_r = 128
    r_w = jnp.concatenate([router_g, router_e, jnp.zeros((depth, d, n_r - N_GROUPS - N_EXPERTS), F32)], axis=2)
    r_b = jnp.concatenate([router_g_b, router_e_b, jnp.zeros((depth, n_r - N_GROUPS - N_EXPERTS), F32)],
                          axis=1).reshape(depth, 1, n_r)

    x = jnp.concatenate([x_prompt.reshape(n_prompt, d), x_sample.reshape(bs * ls, d)], axis=0)
    moe_out = None
    new_p, new_s = [], []
    for l in range(depth):
        x, h = _norm(x, moe_out, modg, l, norm1_g[l], n_prompt, final=False)
        z = _proj(h, w_main, l, tn=768, sigmoid=False, out_dtype=F32)
        gates = _proj(h, w_gate, l, tn=1024, sigmoid=True, out_dtype=BF16)
        groups = [(0, bp, lp, 64, 0.0, None),
                  (n_prompt, bs, ls, ls, float(PAST_LEN),
                   (state_ret[l], state_hgrn[l], state_rwkv[l], state_rwkv_shift[l], state_gdn[l],
                    state_gdn_conv[l]))]
        ys, (st_p, st_s) = _mixers(z, l, groups, params)
        new_p.append(st_p)
        new_s.append(st_s)
        x, h2, logits = _merge(ys, gates, x, modg, l, norm2_g[l], wb_bf, wo_bf, r_w, r_b, n_prompt)
        moe_out = _moe(h2, logits, moe_w_gate, moe_w_up, moe_w_down, l)
    y = _norm(x, moe_out, modg, depth - 1, final_norm_g, n_prompt, final=True)

    def stack(lst, i):
        return jnp.stack([s[i] for s in lst]).astype(F32)
    return ((y[:n_prompt].reshape(bp, lp, d), y[n_prompt:].reshape(bs, ls, d))
            + tuple(stack(new_p, i) for i in range(6)) + tuple(stack(new_s, i) for i in range(6)))
```

```python
import functools
import math

import numpy as np
import jax
import jax.numpy as jnp
from jax import lax
from jax.experimental import pallas as pl
from jax.experimental.pallas import tpu as pltpu

F32 = jnp.float32
BF16 = jnp.bfloat16

D_MODEL = 2048
D_MIX = 512
HEAD_DIM = 128
N_HEADS = 4
RWKV_N = 64
RWKV_HEADS = 8
RWKV_COLS = 1792
GDN_CONV = 4
N_BRANCH = 4
N_GROUPS = 4
EXPERTS_PER_GROUP = 8
N_EXPERTS = 32
TOPK = 2
D_EXPERT = 512
PAST_LEN = 4096
ROPE_BASE = 10000.0
NORM_EPS = 1e-6
GN_EPS = 1e-6
RWKV_GN_EPS = 64e-5
RET_EXP_LO, RET_EXP_HI = 5.0, 12.0

Z_RET, Z_HGRN, Z_RWKV = 0, 2048, 4096
Z_GDN = Z_RWKV + RWKV_COLS
Z_AB = Z_GDN + 4 * D_MIX
Z_GATE = Z_AB + 2 * N_HEADS
N_MAIN = 8192
MIX_W = 2048
GDN_W = 256
GROUP = 32

VMEM_LIMIT = 56 * 1024 * 1024


def _cparams(sem):
    return pltpu.CompilerParams(dimension_semantics=sem, vmem_limit_bytes=VMEM_LIMIT)


def _mm(a, b):
    return jnp.dot(a.astype(BF16), b.astype(BF16), preferred_element_type=F32)


def _mm_nt(a, b):
    return lax.dot_general(a.astype(BF16), b.astype(BF16), (((1,), (1,)), ((), ())),
                           preferred_element_type=F32)


def _mm_tn(a, b):
    return lax.dot_general(a.astype(BF16), b.astype(BF16), (((0,), (0,)), ((), ())),
                           preferred_element_type=F32)


def _silu(x):
    return x * jax.nn.sigmoid(x)


def _softplus(x):
    return jnp.maximum(x, 0.0) + jnp.log1p(jnp.exp(-jnp.abs(x)))


LANES = 128
ROW_CHUNKS = D_MODEL // LANES


def _rows_from_chunks(ref, n_rows):
    return jnp.concatenate([ref[pl.ds(c, n_rows, stride=ROW_CHUNKS), :] for c in range(ROW_CHUNKS)], axis=1)


def _rows_to_chunks(ref, val):
    n_rows = val.shape[0]
    for c in range(ROW_CHUNKS):
        ref[pl.ds(c, n_rows, stride=ROW_CHUNKS), :] = val[:, c * LANES:(c + 1) * LANES]


def _cumsum_rows(x):
    n = x.shape[0]
    row = lax.broadcasted_iota(jnp.int32, x.shape, 0)
    s = 1
    while s < n:
        x = x + jnp.where(row >= s, pltpu.roll(x, s, 0), 0.0)
        s *= 2
    return x


def _row_from_col(col, eye):
    return jnp.sum(jnp.where(eye, col, 0.0), axis=0, keepdims=True)


def _tri_inv(n_mats, c):
    r = lax.broadcasted_iota(jnp.int32, (c, c), 0)
    col = lax.broadcasted_iota(jnp.int32, (c, c), 1)
    eye = jnp.where(r == col, 1.0, 0.0)
    pair = (r >> 1) == (col >> 1)
    xs = [eye - jnp.where(pair, n, 0.0) for n in n_mats]
    m, sh = 2, 1
    while m < c:
        lvl = ((r >> (sh + 1)) == (col >> (sh + 1))) & ((r >> sh) != (col >> sh))
        ts = [_mm(jnp.where(lvl, n, 0.0), x) for n, x in zip(n_mats, xs)]
        xs = [x - _mm(x, t) for x, t in zip(xs, ts)]
        m, sh = m * 2, sh + 1
    return xs


def _ada_kernel(c_ref, w_ref, b_ref, o_ref):
    cm = _silu(c_ref[...])
    o_ref[0] = _mm(cm, w_ref[0]) + b_ref[0]


def _ada(c_all, ada_w, ada_b):
    depth, d, n = ada_w.shape
    rows = c_all.shape[0]
    tn = 1024
    return pl.pallas_call(
        _ada_kernel,
        grid=(depth, n // tn),
        in_specs=[pl.BlockSpec((rows, d), lambda l, j: (0, 0)),
                  pl.BlockSpec((1, d, tn), lambda l, j: (l, 0, j)),
                  pl.BlockSpec((1, 1, tn), lambda l, j: (l, 0, j))],
        out_specs=pl.BlockSpec((1, rows, tn), lambda l, j: (l, 0, j)),
        out_shape=jax.ShapeDtypeStruct((depth, rows, n), F32),
        compiler_params=_cparams(("arbitrary", "arbitrary")),
        name="ada",
    )(c_all, ada_w, ada_b.reshape(depth, 1, n))


def _group_block(tm, n_prompt_tiles):
    gpt = tm // GROUP
    first_sample = 16 // gpt

    def idx(i):
        return jnp.where(i < n_prompt_tiles, 0, first_sample + i - n_prompt_tiles)
    return gpt, idx


def _modulate(y, sc, sh, gpt):
    tm, d = y.shape
    y3 = y.reshape(gpt, GROUP, d)
    return (y3 * (1.0 + sc[:, None, :]) + sh[:, None, :]).reshape(tm, d)


def _rms_rows(x, g):
    return x * lax.rsqrt(jnp.mean(x * x, axis=-1, keepdims=True) + NORM_EPS) * g


def _norm_kernel(*refs, gpt, with_moe, modulated):
    it = iter(refs)
    x_ref = next(it)
    if with_moe:
        m0_ref, m1_ref, gt_ref = next(it), next(it), next(it)
    g_ref = next(it)
    if modulated:
        sc_ref, sh_ref = next(it), next(it)
    x = x_ref[...]
    tm, d = x.shape
    if with_moe:
        moe = _rows_from_chunks(m0_ref, tm) + _rows_from_chunks(m1_ref, tm)
        gt = gt_ref[0]
        x = x + (moe.reshape(gpt, GROUP, d) * gt[:, None, :]).reshape(tm, d)
    y = _rms_rows(x, g_ref[...])
    if modulated:
        xo_ref, h_ref = next(it), next(it)
        xo_ref[...] = x
        h_ref[...] = _modulate(y, sc_ref[0], sh_ref[0], gpt).astype(BF16)
    else:
        y_ref = next(it)
        y_ref[...] = y


def _norm(x, moe_out, modg, l, g, n_prompt, *, final):
    t, d = x.shape
    tm = 256
    gpt, gidx = _group_block(tm, n_prompt // tm)
    with_moe = moe_out is not None
    row = pl.BlockSpec((tm, d), lambda i: (i, 0))

    def mod_spec(col, layer):
        return pl.BlockSpec((1, gpt, d), lambda i: (layer, gidx(i), col))
    args, specs = [x], [row]
    if with_moe:
        lm = l if final else l - 1
        args += [moe_out, moe_out, modg]
        specs += [pl.BlockSpec((tm * ROW_CHUNKS, LANES), lambda i: (i, 0)),
                  pl.BlockSpec((tm * ROW_CHUNKS, LANES), lambda i: (t // tm + i, 0)), mod_spec(5, lm)]
    args.append(g.reshape(1, d))
    specs.append(pl.BlockSpec((1, d), lambda i: (0, 0)))
    if not final:
        args += [modg, modg]
        specs += [mod_spec(1, l), mod_spec(0, l)]
        out_shape = [jax.ShapeDtypeStruct((t, d), F32), jax.ShapeDtypeStruct((t, d), BF16)]
        out_specs = [row, row]
    else:
        out_shape = jax.ShapeDtypeStruct((t, d), F32)
        out_specs = row
    return pl.pallas_call(
        functools.partial(_norm_kernel, gpt=gpt, with_moe=with_moe, modulated=not final),
        grid=(t // tm,), in_specs=specs, out_specs=out_specs, out_shape=out_shape,
        compiler_params=_cparams(("arbitrary",)), name="norm",
    )(*args)


def _proj_kernel(a_ref, w_ref, o_ref, *, sigmoid):
    acc = jnp.dot(a_ref[...], w_ref[0], preferred_element_type=F32)
    if sigmoid:
        acc = jax.nn.sigmoid(acc)
    o_ref[...] = acc.astype(o_ref.dtype)


def _proj(a, w, l, *, tn, sigmoid, out_dtype):
    t, k = a.shape
    n = w.shape[2]
    tm = t // 4
    return pl.pallas_call(
        functools.partial(_proj_kernel, sigmoid=sigmoid),
        grid=(n // tn, t // tm),
        in_specs=[pl.BlockSpec((tm, k), lambda j, i: (i, 0)),
                  pl.BlockSpec((1, k, tn), lambda j, i: (l, 0, j))],
        out_specs=pl.BlockSpec((tm, tn), lambda j, i: (i, j)),
        out_shape=jax.ShapeDtypeStruct((t, n), out_dtype),
        compiler_params=_cparams(("arbitrary", "arbitrary")), name="proj",
    )(a, w)


RET_CHUNK = 256


def _ret_tables(c):
    e = np.linspace(RET_EXP_LO, RET_EXP_HI, N_HEADS)
    lg = np.log1p(-np.exp2(-e))
    t = np.arange(c, dtype=np.float64)
    rel = t[:, None] - t[None, :]
    d_intra = np.where(rel >= 0, np.exp(lg[:, None, None] * np.where(rel >= 0, rel, 0.0)), 0.0)
    d_q = np.exp(lg[:, None] * (t + 1.0))[:, :, None] * np.ones((1, 1, HEAD_DIM))
    d_k = np.exp(lg[:, None] * (c - 1.0 - t))[:, :, None] * np.ones((1, 1, HEAD_DIM))
    d_s = np.exp(lg * c)[:, None, None] * np.ones((1, 8, HEAD_DIM))
    return tuple(jnp.asarray(a, F32) for a in (d_intra, d_q, d_k, d_s))


def _ret_kernel(*refs, c, nc, has_state):
    it = iter(refs)
    z_ref, cos_ref, sin_ref, di_ref, dq_ref, dk_ref, ds_ref = (next(it) for _ in range(7))
    s0_ref = next(it) if has_state else None
    y_ref, so_ref, s_scr = next(it), next(it), next(it)
    n = pl.program_id(1)

    @pl.when(n == 0)
    def _():
        s_scr[...] = s0_ref[0] if has_state else jnp.zeros(s_scr.shape, F32)

    cos, sin = cos_ref[...], sin_ref[...]
    half = HEAD_DIM // 2
    for h in range(N_HEADS):
        lo = h * HEAD_DIM
        q = z_ref[:, lo:lo + HEAD_DIM]
        k = z_ref[:, D_MIX + lo:D_MIX + lo + HEAD_DIM]
        v = z_ref[:, 2 * D_MIX + lo:2 * D_MIX + lo + HEAD_DIM]
        g = z_ref[:, 3 * D_MIX + lo:3 * D_MIX + lo + HEAD_DIM]
        q = q * cos + pltpu.roll(q, half, 1) * sin
        k = (k * cos + pltpu.roll(k, half, 1) * sin) * HEAD_DIM ** -0.5
        s = s_scr[h]
        att = _mm_nt(q, k) * di_ref[h]
        o = _mm(att, v) + _mm(q * dq_ref[h], s)
        s_scr[h] = s * ds_ref[h, 0:1, :] + _mm_tn(k * dk_ref[h], v)
        xc = o - jnp.mean(o, axis=-1, keepdims=True)
        on = xc * lax.rsqrt(jnp.mean(xc * xc, axis=-1, keepdims=True) + GN_EPS)
        y_ref[:, lo:lo + HEAD_DIM] = (on * _silu(g)).astype(y_ref.dtype)

    @pl.when(n == nc - 1)
    def _():
        so_ref[0] = s_scr[...]


def _rope_tables(length, pos0):
    half = HEAD_DIM // 2
    inv = ROPE_BASE ** (-jnp.arange(half, dtype=F32) / half)
    pos = jnp.arange(length, dtype=F32) + pos0
    ang = pos[:, None] * inv[None, :]
    cos, sin = jnp.cos(ang), jnp.sin(ang)
    return jnp.concatenate([cos, cos], axis=1), jnp.concatenate([-sin, sin], axis=1)


def _row_block(row0, c, nc):
    off = row0 // c
    return lambda b, n: off + b * nc + n


def _retention(z, row0, bsz, length, c, pos0, state):
    nc = length // c
    rb = _row_block(row0, c, nc)
    cos, sin = _rope_tables(length, pos0)
    di, dq, dk, ds = _ret_tables(c)
    has_state = state is not None
    const3 = lambda b, n: (0, 0, 0)
    args = [z, cos, sin, di, dq, dk, ds]
    specs = [pl.BlockSpec((c, MIX_W), lambda b, n: (rb(b, n), Z_RET // MIX_W)),
             pl.BlockSpec((c, HEAD_DIM), lambda b, n: (n, 0)),
             pl.BlockSpec((c, HEAD_DIM), lambda b, n: (n, 0)),
             pl.BlockSpec(di.shape, const3), pl.BlockSpec(dq.shape, const3),
             pl.BlockSpec(dk.shape, const3), pl.BlockSpec(ds.shape, const3)]
    sshape = (N_HEADS, HEAD_DIM, HEAD_DIM)
    if has_state:
        args.append(state)
        specs.append(pl.BlockSpec((1,) + sshape, lambda b, n: (b, 0, 0, 0)))
    return pl.pallas_call(
        functools.partial(_ret_kernel, c=c, nc=nc, has_state=has_state),
        grid=(bsz, nc), in_specs=specs,
        out_specs=[pl.BlockSpec((c, D_MIX), lambda b, n: (b * nc + n, 0)),
                   pl.BlockSpec((1,) + sshape, lambda b, n: (b, 0, 0, 0))],
        out_shape=[jax.ShapeDtypeStruct((bsz * length, D_MIX), BF16),
                   jax.ShapeDtypeStruct((bsz,) + sshape, F32)],
        scratch_shapes=[pltpu.VMEM(sshape, F32)],
        compiler_params=_cparams(("arbitrary", "arbitrary")), name="retention",
    )(*args)


SUB = 16


def _gla_head(q, k, v, g, st, c):
    big = _cumsum_rows(g)
    gl = big[c - 1:c, :]
    o = _mm_nt(q * jnp.exp(big), st)
    st_new = st * jnp.exp(gl) + _mm_tn(v, k * jnp.exp(gl - big))

    row = lax.broadcasted_iota(jnp.int32, (c, HEAD_DIM), 0)
    r2 = lax.broadcasted_iota(jnp.int32, (c, c), 0)
    c2 = lax.broadcasted_iota(jnp.int32, (c, c), 1)
    att = jnp.zeros((c, c), F32)
    m, sh = SUB, SUB.bit_length() - 1
    while m < c:
        anchor = jnp.concatenate(
            [jnp.broadcast_to(big[p * 2 * m + m - 1:p * 2 * m + m, :], (2 * m, HEAD_DIM))
             for p in range(c // (2 * m))], axis=0)
        right = ((row >> sh) & 1) == 1
        qt = jnp.where(right, q * jnp.exp(jnp.where(right, big - anchor, 0.0)), 0.0)
        kt = jnp.where(right, 0.0, k * jnp.exp(jnp.where(right, 0.0, anchor - big)))
        att = att + jnp.where((r2 >> (sh + 1)) == (c2 >> (sh + 1)), _mm_nt(qt, kt), 0.0)
        m, sh = m * 2, sh + 1
    o = o + _mm(att, v)

    f = jnp.exp(g)
    rsub = lax.broadcasted_iota(jnp.int32, (SUB, HEAD_DIM), 0)
    diag = []
    for blk in range(c // SUB):
        r0 = blk * SUB
        vb = v[r0:r0 + SUB, :]
        w = jnp.zeros((SUB, HEAD_DIM), F32)
        ob = jnp.zeros((SUB, HEAD_DIM), F32)
        for tl in range(SUB):
            t = r0 + tl
            if tl:
                w = w * f[t:t + 1, :]
            w = jnp.where(rsub == tl, k[t:t + 1, :], w)
            a_col = jnp.sum(w * q[t:t + 1, :], axis=1, keepdims=True)
            o_t = jnp.sum(a_col * vb, axis=0, keepdims=True)
            ob = jnp.where(rsub == tl, o_t, ob)
        diag.append(ob)
    return o + jnp.concatenate(diag, axis=0), st_new


def _hgrn_kernel(*refs, c, nc, has_state):
    it = iter(refs)
    z_ref, llb_ref, l1m_ref, oml_ref, gn_ref = (next(it) for _ in range(5))
    s0_ref = next(it) if has_state else None
    y_ref, so_ref, s_scr = next(it), next(it), next(it)
    n = pl.program_id(1)

    @pl.when(n == 0)
    def _():
        for h in range(N_HEADS):
            s_scr[h] = s0_ref[0, h].T if has_state else jnp.zeros((HEAD_DIM, HEAD_DIM), F32)

    for h in range(N_HEADS):
        lo = h * HEAD_DIM
        q = z_ref[:, lo:lo + HEAD_DIM]
        f = z_ref[:, D_MIX + lo:D_MIX + lo + HEAD_DIM]
        iv = z_ref[:, 2 * D_MIX + lo:2 * D_MIX + lo + HEAD_DIM]
        g = z_ref[:, 3 * D_MIX + lo:3 * D_MIX + lo + HEAD_DIM]
        ls = jnp.minimum(f, 0.0) - jnp.log1p(jnp.exp(-jnp.abs(f)))
        a = llb_ref[:, lo:lo + HEAD_DIM]
        b = l1m_ref[:, lo:lo + HEAD_DIM] + ls
        logf = jnp.maximum(a, b) + jnp.log1p(jnp.exp(-jnp.abs(a - b)))
        kf = oml_ref[:, lo:lo + HEAD_DIM] * jax.nn.sigmoid(-f)
        o, st = _gla_head(q, kf, iv, logf, s_scr[h], c)
        s_scr[h] = st
        y = _rms_rows(o, gn_ref[...]) * _silu(g)
        y_ref[:, lo:lo + HEAD_DIM] = y.astype(y_ref.dtype)

    @pl.when(n == nc - 1)
    def _():
        for h in range(N_HEADS):
            so_ref[0, h] = s_scr[h].T


def _hgrn(z, row0, bsz, length, c, lb, gnorm, state):
    nc = length // c
    rb = _row_block(row0, c, nc)
    has_state = state is not None
    vec = pl.BlockSpec((1, D_MIX), lambda b, n: (0, 0))
    lb = lb.reshape(1, D_MIX)
    args = [z, jnp.log(lb), jnp.log1p(-lb), 1.0 - lb, gnorm.reshape(1, HEAD_DIM)]
    specs = [pl.BlockSpec((c, MIX_W), lambda b, n: (rb(b, n), Z_HGRN // MIX_W)), vec, vec, vec,
             pl.BlockSpec((1, HEAD_DIM), lambda b, n: (0, 0))]
    sshape = (N_HEADS, HEAD_DIM, HEAD_DIM)
    if has_state:
        args.append(state)
        specs.append(pl.BlockSpec((1,) + sshape, lambda b, n: (b, 0, 0, 0)))
    return pl.pallas_call(
        functools.partial(_hgrn_kernel, c=c, nc=nc, has_state=has_state),
        grid=(bsz, nc), in_specs=specs,
        out_specs=[pl.BlockSpec((c, D_MIX), lambda b, n: (b * nc + n, 0)),
                   pl.BlockSpec((1,) + sshape, lambda b, n: (b, 0, 0, 0))],
        out_shape=[jax.ShapeDtypeStruct((bsz * length, D_MIX), BF16),
                   jax.ShapeDtypeStruct((bsz,) + sshape, F32)],
        scratch_shapes=[pltpu.VMEM(sshape, F32)],
        compiler_params=_cparams(("arbitrary", "arbitrary")), name="hgrn2",
    )(*args)


def _rwkv_kernel(*refs, c, nsub, nc, has_state):
    it = iter(refs)
    (z_ref, mu_ref, w0_ref, w2_ref, a0_ref, a2_ref, g2_ref, kk_ref, ka_ref, rk_ref,
     lnw_ref, lnb_ref) = (next(it) for _ in range(12))
    if has_state:
        sh0_ref, s0_ref = next(it), next(it)
    y_ref, so_ref, s_scr, prev_scr = next(it), next(it), next(it), next(it)
    n = pl.program_id(1)
    nh, hd = RWKV_HEADS, RWKV_N

    @pl.when(n == 0)
    def _():
        s_scr[...] = s0_ref[0] if has_state else jnp.zeros(s_scr.shape, F32)
        prev_scr[...] = (jnp.broadcast_to(sh0_ref[0], prev_scr.shape) if has_state
                         else jnp.zeros(prev_scr.shape, F32))

    z = z_ref[...]
    rows = nsub * c
    row = lax.broadcasted_iota(jnp.int32, z.shape, 0)
    prev = jnp.where(row == 0, prev_scr[0:1, :], pltpu.roll(z, 1, 0))
    prev_scr[...] = jnp.broadcast_to(z[rows - 1:rows, :], prev_scr.shape)
    xs = z + (prev - z) * mu_ref[...]
    o3 = 3 * D_MIX
    r, k, v = xs[:, :D_MIX], xs[:, D_MIX:2 * D_MIX], xs[:, 2 * D_MIX:o3]
    w_lo, a_lo, g_lo = xs[:, o3:o3 + 64], xs[:, o3 + 64:o3 + 128], xs[:, o3 + 128:o3 + 256]
    w = -_softplus(-(w0_ref[...] + _mm(jnp.tanh(w_lo), w2_ref[...]))) - 0.5
    ld = -jnp.exp(w)
    a = jax.nn.sigmoid(a0_ref[...] + _mm(a_lo, a2_ref[...]))
    g = _mm(jax.nn.sigmoid(g_lo), g2_ref[...])
    kkv = k * kk_ref[...]
    k = k * (1.0 + (a - 1.0) * ka_ref[...])
    bonus = r * k * rk_ref[...]

    r2 = lax.broadcasted_iota(jnp.int32, (2 * c, 2 * c), 0)
    c2 = lax.broadcasted_iota(jnp.int32, (2 * c, 2 * c), 1)
    tq = jnp.where(r2 >= c, r2 - c, r2)
    ts = jnp.where(c2 >= c, c2 - c, c2)
    keep = (tq > ts) | ((r2 >= c) & (tq == ts))
    heads, subs = range(nh), range(nsub)
    pairs = [(j, h) for j in subs for h in heads]
    sls = [slice(h * hd, (h + 1) * hd) for h in heads]
    lhs, rhs, vs, gam = {}, {}, {}, {}
    for j in subs:
        rs = slice(j * c, (j + 1) * c)
        ldj = ld[rs]
        lc = _cumsum_rows(ldj)
        e_in, e_in_neg, e_ex = jnp.exp(lc), jnp.exp(-lc), jnp.exp(lc - ldj)
        for h in heads:
            sl = sls[h]
            kkh = kkv[rs, sl]
            kap = kkh * lax.rsqrt(jnp.sum(kkh * kkh, axis=-1, keepdims=True) + 1e-6)
            lhs[j, h] = jnp.concatenate([kap * e_ex[:, sl], r[rs, sl] * e_in[:, sl]], axis=0)
            rhs[j, h] = jnp.concatenate([kap * a[rs, sl] * e_in_neg[:, sl], k[rs, sl] * e_in_neg[:, sl]], axis=0)
            vs[j, h] = v[rs, sl]
            gam[j, h] = e_in[c - 1:c, sl]
    pm = {p: jnp.where(keep, _mm_nt(lhs[p], rhs[p]), 0.0) for p in pairs}
    zv = jnp.zeros((c, hd), F32)
    mkv = {p: _mm(pm[p][:c, :], jnp.concatenate([zv, vs[p]], axis=0)) for p in pairs}
    t_inv = dict(zip(pairs, _tri_inv([pm[p][:c, :c] for p in pairs], c)))
    s_cur = [s_scr[h] for h in heads]
    ys = {}
    for j in subs:
        qs = [_mm_nt(lhs[j, h], s_cur[h]) for h in heads]
        us = [-_mm(t_inv[j, h], qs[h][:c, :] + mkv[j, h]) for h in heads]
        uv = [jnp.concatenate([us[h], vs[j, h]], axis=0) for h in heads]
        for h in heads:
            ys[j, h] = qs[h][c:, :] + _mm(pm[j, h][c:, :], uv[h])
        s_cur = [(s_cur[h] + _mm_tn(uv[h], rhs[j, h])) * gam[j, h] for h in heads]
    for h in heads:
        s_scr[h] = s_cur[h]
    for j, h in pairs:
        rs, sl = slice(j * c, (j + 1) * c), sls[h]
        y = ys[j, h]
        yc = y - jnp.mean(y, axis=-1, keepdims=True)
        yn = yc * lax.rsqrt(jnp.mean(yc * yc, axis=-1, keepdims=True) + RWKV_GN_EPS)
        yn = yn * lnw_ref[:, sl] + lnb_ref[:, sl]
        yn = yn + jnp.sum(bonus[rs, sl], axis=-1, keepdims=True) * vs[j, h]
        y_ref[rs, sl] = (yn * g[rs, sl]).astype(y_ref.dtype)

    @pl.when(n == nc - 1)
    def _():
        so_ref[0] = s_scr[...]


def _sub_chunks(length, c, most):
    nc = length // c
    while nc % most:
        most //= 2
    return most


def _rwkv(z, row0, bsz, length, c, p, shift, state):
    nsub = _sub_chunks(length, c, 2)
    cb = c * nsub
    nc = length // cb
    rb = _row_block(row0, cb, nc)
    has_state = state is not None
    pad = MIX_W - RWKV_COLS
    c2 = lambda b, n: (0, 0)
    args = [z, jnp.pad(p["mu"], (0, pad)).reshape(1, MIX_W),
            p["w0"].reshape(1, D_MIX), p["w2"], p["a0"].reshape(1, D_MIX), p["a2"], p["g2"],
            p["k_k"].reshape(1, D_MIX), p["k_a"].reshape(1, D_MIX), p["r_k"].reshape(1, D_MIX),
            p["ln_w"].reshape(1, D_MIX), p["ln_b"].reshape(1, D_MIX)]
    specs = [pl.BlockSpec((cb, MIX_W), lambda b, n: (rb(b, n), Z_RWKV // MIX_W))]
    specs += [pl.BlockSpec(a.shape, c2) for a in args[1:]]
    sshape = (RWKV_HEADS, RWKV_N, RWKV_N)
    if has_state:
        args += [jnp.pad(shift, ((0, 0), (0, pad))).reshape(bsz, 1, MIX_W), state]
        specs += [pl.BlockSpec((1, 1, MIX_W), lambda b, n: (b, 0, 0)),
                  pl.BlockSpec((1,) + sshape, lambda b, n: (b, 0, 0, 0))]
    return pl.pallas_call(
        functools.partial(_rwkv_kernel, c=c, nsub=nsub, nc=nc, has_state=has_state),
        grid=(bsz, nc), in_specs=specs,
        out_specs=[pl.BlockSpec((cb, D_MIX), lambda b, n: (b * nc + n, 0)),
                   pl.BlockSpec((1,) + sshape, lambda b, n: (b, 0, 0, 0))],
        out_shape=[jax.ShapeDtypeStruct((bsz * length, D_MIX), BF16),
                   jax.ShapeDtypeStruct((bsz,) + sshape, F32)],
        scratch_shapes=[pltpu.VMEM(sshape, F32), pltpu.VMEM((8, MIX_W), F32)],
        compiler_params=_cparams(("arbitrary", "arbitrary")), name="rwkv7",
    )(*args)


CONV_PAD = 8


def _gdn_kernel(*refs, c, nsub, nc, has_state):
    it = iter(refs)
    n_win = 4 * D_MIX // GDN_W
    z_refs = [next(it) for _ in range(n_win)]
    ab_ref, cw_ref, alog_ref, dtb_ref, gn_ref = (next(it) for _ in range(5))
    if has_state:
        cv0_ref, s0_ref = next(it), next(it)
    y_ref, so_ref, s_scr, xp_scr = next(it), next(it), next(it), next(it)
    n = pl.program_id(1)
    qkv_w = 3 * D_MIX
    qkv_win = qkv_w // GDN_W
    tail = GDN_CONV - 1
    rows = nsub * c

    @pl.when(n == 0)
    def _():
        s_scr[...] = s0_ref[0] if has_state else jnp.zeros(s_scr.shape, F32)
        xp_scr[0:CONV_PAD, :] = jnp.zeros((CONV_PAD, qkv_w), F32)
        if has_state:
            xp_scr[CONV_PAD - tail:CONV_PAD, :] = cv0_ref[0]

    for j in range(qkv_win):
        xp_scr[CONV_PAD:CONV_PAD + rows, j * GDN_W:(j + 1) * GDN_W] = z_refs[j][...]
    conv = xp_scr[CONV_PAD - tail:CONV_PAD - tail + rows, :] * cw_ref[0:1, :]
    for j in range(1, GDN_CONV):
        conv = conv + xp_scr[CONV_PAD - tail + j:CONV_PAD - tail + j + rows, :] * cw_ref[j:j + 1, :]
    xp_scr[CONV_PAD - tail:CONV_PAD, :] = xp_scr[CONV_PAD + rows - tail:CONV_PAD + rows, :]
    act = _silu(conv)

    ab = ab_ref[...]
    gdec = -jnp.exp(alog_ref[...]) * _softplus(ab + dtb_ref[...])
    beta = jax.nn.sigmoid(ab)

    r2 = lax.broadcasted_iota(jnp.int32, (c, c), 0)
    c2 = lax.broadcasted_iota(jnp.int32, (c, c), 1)
    eye, causal, strict = r2 == c2, r2 >= c2, r2 > c2
    heads, subs = range(N_HEADS), range(nsub)
    pairs = [(j, h) for j in subs for h in heads]
    kq, ks, vs, gcols, bcols, decs = {}, {}, {}, {}, {}, {}
    for j in subs:
        rs = slice(j * c, (j + 1) * c)
        gcum = _cumsum_rows(gdec[rs])
        for h in heads:
            lo = h * HEAD_DIM
            q = act[rs, lo:lo + HEAD_DIM]
            k = act[rs, D_MIX + lo:D_MIX + lo + HEAD_DIM]
            q = q * lax.rsqrt(jnp.sum(q * q, axis=-1, keepdims=True) + 1e-6) * HEAD_DIM ** -0.5
            k = k * lax.rsqrt(jnp.sum(k * k, axis=-1, keepdims=True) + 1e-6)
            gcol = gcum[:, h:h + 1]
            rel = gcol - _row_from_col(gcol, eye)
            kq[j, h] = jnp.concatenate([k, q], axis=0)
            ks[j, h] = k
            vs[j, h] = act[rs, 2 * D_MIX + lo:2 * D_MIX + lo + HEAD_DIM]
            gcols[j, h] = gcol
            bcols[j, h] = beta[rs, N_HEADS + h:N_HEADS + h + 1]
            decs[j, h] = jnp.where(causal, jnp.exp(jnp.where(causal, rel, 0.0)), 0.0)
    kk_qk = {p: _mm_nt(kq[p], ks[p]) for p in pairs}
    t_inv = dict(zip(pairs, _tri_inv(
        [jnp.where(strict, bcols[p] * kk_qk[p][:c] * decs[p], 0.0) for p in pairs], c)))
    s_cur = [s_scr[h] for h in heads]
    os_ = {}
    for j in subs:
        ks_qs = [_mm(kq[j, h], s_cur[h]) for h in heads]
        egs = [jnp.exp(gcols[j, h]) for h in heads]
        us = [_mm(t_inv[j, h], bcols[j, h] * (vs[j, h] - egs[h] * ks_qs[h][:c])) for h in heads]
        for h in heads:
            os_[j, h] = egs[h] * ks_qs[h][c:] + _mm(kk_qk[j, h][c:] * decs[j, h], us[h])
        gls = [gcols[j, h][c - 1:c, :] for h in heads]
        s_cur = [jnp.exp(gls[h]) * s_cur[h] + _mm_tn(ks[j, h] * jnp.exp(gls[h] - gcols[j, h]), us[h])
                 for h in heads]
    for h in heads:
        s_scr[h] = s_cur[h]
    for j, h in pairs:
        rs, lo = slice(j * c, (j + 1) * c), h * HEAD_DIM
        zg = z_refs[qkv_win + lo // GDN_W][rs, lo % GDN_W:lo % GDN_W + HEAD_DIM]
        y = _rms_rows(os_[j, h], gn_ref[...]) * _silu(zg)
        y_ref[rs, lo:lo + HEAD_DIM] = y.astype(y_ref.dtype)

    @pl.when(n == nc - 1)
    def _():
        so_ref[0] = s_scr[...]


def _gdn(z, row0, bsz, length, c, p, conv_state, state):
    nsub = _sub_chunks(length, c, 4)
    cb = c * nsub
    nc = length // cb
    rb = _row_block(row0, cb, nc)
    has_state = state is not None
    c2 = lambda b, n: (0, 0)
    lane_pad = lambda a: jnp.pad(a, (0, HEAD_DIM - a.shape[0])).reshape(1, HEAD_DIM)
    n_win = 4 * D_MIX // GDN_W
    params = [p["conv_w"], lane_pad(p["A_log"]), lane_pad(p["dt_bias"]), p["norm_g"].reshape(1, HEAD_DIM)]
    args = [z] * (n_win + 1) + params
    specs = [pl.BlockSpec((cb, GDN_W), lambda b, n, j=j: (rb(b, n), Z_GDN // GDN_W + j)) for j in range(n_win)]
    specs.append(pl.BlockSpec((cb, HEAD_DIM), lambda b, n: (rb(b, n), Z_AB // HEAD_DIM)))
    specs += [pl.BlockSpec(a.shape, c2) for a in params]
    sshape = (N_HEADS, HEAD_DIM, HEAD_DIM)
    if has_state:
        args += [conv_state, state]
        specs += [pl.BlockSpec((1, GDN_CONV - 1, 3 * D_MIX), lambda b, n: (b, 0, 0)),
                  pl.BlockSpec((1,) + sshape, lambda b, n: (b, 0, 0, 0))]
    return pl.pallas_call(
        functools.partial(_gdn_kernel, c=c, nsub=nsub, nc=nc, has_state=has_state),
        grid=(bsz, nc), in_specs=specs,
        out_specs=[pl.BlockSpec((cb, D_MIX), lambda b, n: (b * nc + n, 0)),
                   pl.BlockSpec((1,) + sshape, lambda b, n: (b, 0, 0, 0))],
        out_shape=[jax.ShapeDtypeStruct((bsz * length, D_MIX), BF16),
                   jax.ShapeDtypeStruct((bsz,) + sshape, F32)],
        scratch_shapes=[pltpu.VMEM(sshape, F32), pltpu.VMEM((CONV_PAD + cb, 3 * D_MIX), F32)],
        compiler_params=_cparams(("arbitrary", "arbitrary")), name="gdn",
    )(*args)


def _merge_kernel(y0_ref, y1_ref, y2_ref, y3_ref, gate_ref, x_ref, gt_ref, sc_ref, sh_ref, g_ref,
                  wb_ref, wo_ref, rw_ref, rb_ref, xo_ref, h_ref, lg_ref, *, gpt):
    tm, d = x_ref.shape
    merged = jnp.zeros((tm, d), F32)
    for nb, y_ref in enumerate((y0_ref, y1_ref, y2_ref, y3_ref)):
        br = jnp.dot(y_ref[...], wb_ref[0, nb], preferred_element_type=F32)
        merged = merged + gate_ref[:, nb * d:(nb + 1) * d].astype(F32) * br
    m = jnp.dot(merged.astype(BF16), wo_ref[0], preferred_element_type=F32)
    gt = gt_ref[0]
    x = x_ref[...] + (m.reshape(gpt, GROUP, d) * gt[:, None, :]).reshape(tm, d)
    xo_ref[...] = x
    h = _modulate(_rms_rows(x, g_ref[...]), sc_ref[0], sh_ref[0], gpt)
    _rows_to_chunks(h_ref, h)
    lg_ref[...] = _mm(h, rw_ref[0]) + rb_ref[0]


def _merge(ys, gates, x, modg, l, g2, wb, wo, rw, rb, n_prompt):
    t, d = x.shape
    tm = 256
    gpt, gidx = _group_block(tm, n_prompt // tm)
    row = lambda w: pl.BlockSpec((tm, w), lambda i: (i, 0))

    def mod_spec(col):
        return pl.BlockSpec((1, gpt, d), lambda i: (l, gidx(i), col))

    def const(shape):
        return pl.BlockSpec(shape, lambda i: (0,) * len(shape), pipeline_mode=pl.Buffered(1))

    def layer(arr):
        return pl.BlockSpec((1,) + arr.shape[1:], lambda i: (l,) + (0,) * (arr.ndim - 1),
                            pipeline_mode=pl.Buffered(1))
    nr = rw.shape[2]
    return pl.pallas_call(
        functools.partial(_merge_kernel, gpt=gpt),
        grid=(t // tm,),
        in_specs=[row(D_MIX)] * 4 + [row(N_BRANCH * d), row(d), mod_spec(2), mod_spec(4), mod_spec(3),
                                     const((1, d)), layer(wb), layer(wo), layer(rw), layer(rb)],
        out_specs=[row(d), pl.BlockSpec((tm * ROW_CHUNKS, LANES), lambda i: (i, 0)), row(nr)],
        out_shape=[jax.ShapeDtypeStruct((t, d), F32), jax.ShapeDtypeStruct((t * ROW_CHUNKS, LANES), F32),
                   jax.ShapeDtypeStruct((t, nr), F32)],
        compiler_params=_cparams(("arbitrary",)), name="merge",
    )(*ys, gates, x, modg, modg, modg, g2.reshape(1, d), wb, wo, rw, rb)


MOE_BLK = 256
MOE_UNROLL = 8


def _moe_kernel(be_ref, tok_ref, slot_ref, nused_ref, h_hbm, w_ref, wg_ref, wu_ref, wd_ref, o_hbm,
                x0, x1, y0, y1, wg_bf, wu_bf, wd_bf, gsem, ssem):
    i = pl.program_id(0)
    nused = nused_ref[0]

    rc = ROW_CHUNKS

    def gather(blk, xb, q, r):
        return pltpu.make_async_copy(h_hbm.at[pl.ds(tok_ref[blk * MOE_BLK + r] * rc, rc)],
                                     xb.at[pl.ds(r * rc, rc)], gsem.at[q])

    def scatter(blk, yb, q, r):
        return pltpu.make_async_copy(yb.at[pl.ds(r * rc, rc)],
                                     o_hbm.at[pl.ds(slot_ref[(blk + 1) * MOE_BLK + r] * rc, rc)], ssem.at[q])

    def for_rows(fn):
        def group(j, carry):
            for u in range(MOE_UNROLL):
                fn(j * MOE_UNROLL + u)
            return carry
        lax.fori_loop(0, MOE_BLK // MOE_UNROLL, group, 0)

    def step(q, xa, ya, xb, yb):
        @pl.when(i == 0)
        def _():
            yb[...] = jnp.zeros(yb.shape, F32)
            n_real = o_hbm.shape[0] - 2 * MOE_BLK * rc
            init = pltpu.make_async_copy(yb, o_hbm.at[pl.ds(n_real, MOE_BLK * rc)], ssem.at[q])
            init.start()
            init.wait()
            for_rows(lambda r: gather(i, xa, q, r).start())

        @pl.when(i > 0)
        def _():
            for_rows(lambda r: scatter(i - 2, ya, q, r).wait())
        for_rows(lambda r: gather(i, xa, q, r).wait())

        @pl.when((i == 0) | (be_ref[i] != be_ref[jnp.maximum(i - 1, 0)]))
        def _():
            wg_bf[...] = wg_ref[0, 0].astype(BF16)
            wu_bf[...] = wu_ref[0, 0].astype(BF16)
            wd_bf[...] = wd_ref[0, 0].astype(BF16)

        for r in range(MOE_BLK):
            gather(i + 1, xb, 1 - q, r).start()
        for r in range(MOE_BLK):
            scatter(i - 1, yb, 1 - q, r).start()
        x = _rows_from_chunks(xa, MOE_BLK).astype(BF16)
        hid = _silu(jnp.dot(x, wg_bf[...], preferred_element_type=F32)) * jnp.dot(
            x, wu_bf[...], preferred_element_type=F32)
        _rows_to_chunks(ya, jnp.dot(hid.astype(BF16), wd_bf[...], preferred_element_type=F32) * w_ref[...])

        @pl.when(i == nused - 1)
        def _():
            for_rows(lambda r: scatter(i, ya, q, r).start())
            for_rows(lambda r: scatter(i - 1, yb, 1 - q, r).wait())
            for_rows(lambda r: scatter(i, ya, q, r).wait())
            for_rows(lambda r: gather(i + 1, xb, 1 - q, r).wait())

    @pl.when((i < nused) & (i % 2 == 0))
    def _():
        step(0, x0, y0, x1, y1)

    @pl.when((i < nused) & (i % 2 == 1))
    def _():
        step(1, x1, y1, x0, y0)


def _route(logits, n_tok):
    lt = logits[:, :N_GROUPS + N_EXPERTS].T
    lg = lt[:N_GROUPS]
    gsel = jnp.argmax(lg, axis=0).astype(jnp.int32)
    gw = jnp.max(jax.nn.softmax(lg, axis=0), axis=0)
    le_all = lt[N_GROUPS:].reshape(N_GROUPS, EXPERTS_PER_GROUP, n_tok)
    le = sum(jnp.where(gsel[None, :] == g, le_all[g], 0.0) for g in range(N_GROUPS))
    pe = jax.nn.softmax(le, axis=0)
    rows8 = lax.broadcasted_iota(jnp.int32, pe.shape, 0)
    i1 = jnp.argmax(pe, axis=0).astype(jnp.int32)
    v1 = jnp.max(pe, axis=0)
    pe_rest = jnp.where(rows8 == i1[None, :], -jnp.inf, pe)
    i2 = jnp.argmax(pe_rest, axis=0).astype(jnp.int32)
    v2 = jnp.max(pe_rest, axis=0)
    top_v = jnp.stack([v1, v2], axis=1)
    top_i = jnp.stack([i1, i2], axis=1)
    wts = top_v / jnp.sum(top_v, axis=-1, keepdims=True) * gw[:, None]
    eid = (gsel[:, None] * EXPERTS_PER_GROUP + top_i).reshape(-1).astype(jnp.int32)
    a = n_tok * TOPK
    order = jnp.argsort(eid).astype(jnp.int32)
    counts = jnp.sum((eid[None, :] == jnp.arange(N_EXPERTS, dtype=jnp.int32)[:, None]).astype(jnp.int32), axis=1)
    padded = (counts + MOE_BLK - 1) // MOE_BLK * MOE_BLK
    pend = jnp.cumsum(padded)
    pstart = pend - padded
    cstart = jnp.cumsum(counts) - counts
    n_blocks = -(-a // MOE_BLK) + N_EXPERTS
    rows = n_blocks * MOE_BLK
    blk_row0 = jnp.arange(n_blocks, dtype=jnp.int32) * MOE_BLK
    block_e = jnp.minimum(jnp.sum((pend[None, :] <= blk_row0[:, None]).astype(jnp.int32), axis=1),
                          N_EXPERTS - 1).astype(jnp.int32)
    nused = (pend[-1:] // MOE_BLK).astype(jnp.int32)
    e_row = jnp.repeat(block_e, MOE_BLK)
    row_id = jnp.arange(rows, dtype=jnp.int32)
    j_row = row_id - pstart[e_row]
    valid = j_row < counts[e_row]
    asg = order[jnp.clip(cstart[e_row] + j_row, 0, a - 1)]
    tok = jnp.where(valid, asg // TOPK, 0)
    dummy = a + ((row_id // MOE_BLK) % 2) * MOE_BLK + row_id % MOE_BLK
    slot = jnp.where(valid, (asg % TOPK) * n_tok + asg // TOPK, dummy)
    wrow = jnp.where(valid, wts.reshape(-1)[asg], 0.0)
    tok = jnp.concatenate([tok, jnp.zeros((MOE_BLK,), jnp.int32)])
    slot = jnp.concatenate([a + MOE_BLK + jnp.arange(MOE_BLK, dtype=jnp.int32), slot])
    return block_e, tok, slot, nused, wrow.reshape(rows, 1), n_blocks


def _moe(h, logits, wg, wu, wd, l):
    t, d = h.shape[0] // ROW_CHUNKS, D_MODEL
    block_e, tok, slot, nused, wrow, n_blocks = _route(logits, t)
    grid_spec = pltpu.PrefetchScalarGridSpec(
        num_scalar_prefetch=4, grid=(n_blocks,),
        in_specs=[pl.BlockSpec(memory_space=pl.ANY),
                  pl.BlockSpec((MOE_BLK, 1), lambda i, be, *_: (i, 0)),
                  pl.BlockSpec((1, 1, d, D_EXPERT), lambda i, be, *_: (l, be[i], 0, 0)),
                  pl.BlockSpec((1, 1, d, D_EXPERT), lambda i, be, *_: (l, be[i], 0, 0)),
                  pl.BlockSpec((1, 1, D_EXPERT, d), lambda i, be, *_: (l, be[i], 0, 0))],
        out_specs=pl.BlockSpec(memory_space=pl.ANY),
        scratch_shapes=[pltpu.VMEM((MOE_BLK * ROW_CHUNKS, LANES), F32)] * 4 + [
                        pltpu.VMEM((d, D_EXPERT), BF16), pltpu.VMEM((d, D_EXPERT), BF16),
                        pltpu.VMEM((D_EXPERT, d), BF16),
                        pltpu.SemaphoreType.DMA((2,)), pltpu.SemaphoreType.DMA((2,))])
    out = pl.pallas_call(
        _moe_kernel, grid_spec=grid_spec,
        out_shape=jax.ShapeDtypeStruct(((t * TOPK + 2 * MOE_BLK) * ROW_CHUNKS, LANES), F32),
        compiler_params=_cparams(("arbitrary",)), name="moe",
    )(block_e, tok, slot, nused, h, wrow, wg, wu, wd)
    return out


def _pack_w_in(w_in):
    w_main = jnp.pad(w_in[:, :, :Z_GATE].astype(BF16), ((0, 0), (0, 0), (0, N_MAIN - Z_GATE)))
    return w_main, w_in[:, :, Z_GATE:].astype(BF16)


def _mixers(z, l, groups, params):
    outs = [[] for _ in range(N_BRANCH)]
    states = []
    for (row0, bsz, length, c, pos0, st) in groups:
        s_ret, s_hgrn, s_rwkv, s_shift, s_gdn, s_conv = st if st is not None else (None,) * 6
        y_a, n_ret = _retention(z, row0, bsz, length, min(length, RET_CHUNK), pos0, s_ret)
        y_b, n_hgrn = _hgrn(z, row0, bsz, length, c, params["hgrn_lb"][l], params["hgrn_norm_g"][l], s_hgrn)
        y_c, n_rwkv = _rwkv(z, row0, bsz, length, c, {k: v[l] for k, v in params["rwkv"].items()},
                            s_shift, s_rwkv)
        y_d, n_gdn = _gdn(z, row0, bsz, length, c, {k: v[l] for k, v in params["gdn"].items()},
                          s_conv, s_gdn)
        for lst, y in zip(outs, (y_a, y_b, y_c, y_d)):
            lst.append(y)
        def seq_row(j, col0, width):
            return lax.slice(z, (row0 + j, col0), (row0 + (bsz - 1) * length + j + 1, col0 + width),
                             (length, 1))
        n_shift = seq_row(length - 1, Z_RWKV, RWKV_COLS)
        n_conv = jnp.stack([seq_row(length - (GDN_CONV - 1) + j, Z_GDN, 3 * D_MIX)
                            for j in range(GDN_CONV - 1)], axis=1)
        states.append((n_ret, n_hgrn, n_rwkv, n_shift, n_gdn, n_conv))
    return [jnp.concatenate(lst, axis=0) for lst in outs], states


def kernel(x_prompt, x_sample, c_prompt, c_sample, state_ret, state_hgrn, state_rwkv, state_rwkv_shift,
           state_gdn, state_gdn_conv, ada_w, ada_b, norm1_g, norm2_g, w_in, hgrn_lb_logits, hgrn_norm_g,
           rwkv_mu, rwkv_w0, rwkv_w2, rwkv_a0, rwkv_a2, rwkv_g2, rwkv_k_k, rwkv_k_a, rwkv_r_k, rwkv_ln_w,
           rwkv_ln_b, gdn_conv_w, gdn_A_log, gdn_dt_bias, gdn_norm_g, w_branch, w_out, router_g, router_g_b,
           router_e, router_e_b, moe_w_gate, moe_w_up, moe_w_down, final_norm_g):
    depth = ada_w.shape[0]
    bp, lp, d = x_prompt.shape
    bs, ls, _ = x_sample.shape
    n_prompt = bp * lp
    assert bp == 1 and ls == GROUP and bs == 16

    lb_cum = jnp.cumsum(jax.nn.softmax(hgrn_lb_logits.astype(F32), axis=0), axis=0)
    params = dict(
        hgrn_lb=lb_cum - lb_cum[:1], hgrn_norm_g=hgrn_norm_g,
        rwkv=dict(mu=rwkv_mu, w0=rwkv_w0, w2=rwkv_w2, a0=rwkv_a0, a2=rwkv_a2, g2=rwkv_g2, k_k=rwkv_k_k,
                  k_a=rwkv_k_a, r_k=rwkv_r_k.reshape(depth, D_MIX), ln_w=rwkv_ln_w, ln_b=rwkv_ln_b),
        gdn=dict(conv_w=gdn_conv_w, A_log=gdn_A_log, dt_bias=gdn_dt_bias, norm_g=gdn_norm_g))

    c_all = jnp.concatenate([c_prompt, c_sample, jnp.zeros((24 - bp - bs, d), F32)], axis=0)
    mod = _ada(c_all, ada_w, ada_b)
    modg = jnp.concatenate([jnp.broadcast_to(mod[:, :1], (depth, 16, 6 * d)), mod[:, 1:1 + bs]], axis=1)

    w_main, w_gate = _pack_w_in(w_in)
    wb_bf, wo_bf = w_branch.astype(BF16), w_out.astype(BF16)
    n_r = 128
    r_w = jnp.concatenate([router_g, router_e, jnp.zeros((depth, d, n_r - N_GROUPS - N_EXPERTS), F32)], axis=2)
    r_b = jnp.concatenate([router_g_b, router_e_b, jnp.zeros((depth, n_r - N_GROUPS - N_EXPERTS), F32)],
                          axis=1).reshape(depth, 1, n_r)

    x = jnp.concatenate([x_prompt.reshape(n_prompt, d), x_sample.reshape(bs * ls, d)], axis=0)
    moe_out = None
    new_p, new_s = [], []
    for l in range(depth):
        x, h = _norm(x, moe_out, modg, l, norm1_g[l], n_prompt, final=False)
        z = _proj(h, w_main, l, tn=1024, sigmoid=False, out_dtype=F32)
        gates = _proj(h, w_gate, l, tn=1024, sigmoid=True, out_dtype=BF16)
        groups = [(0, bp, lp, 64, 0.0, None),
                  (n_prompt, bs, ls, ls, float(PAST_LEN),
                   (state_ret[l], state_hgrn[l], state_rwkv[l], state_rwkv_shift[l], state_gdn[l],
                    state_gdn_conv[l]))]
        ys, (st_p, st_s) = _mixers(z, l, groups, params)
        new_p.append(st_p)
        new_s.append(st_s)
        x, h2, logits = _merge(ys, gates, x, modg, l, norm2_g[l], wb_bf, wo_bf, r_w, r_b, n_prompt)
        moe_out = _moe(h2, logits, moe_w_gate, moe_w_up, moe_w_down, l)
    y = _norm(x, moe_out, modg, depth - 1, final_norm_g, n_prompt, final=True)

    def stack(lst, i):
        return jnp.stack([s[i] for s in lst]).astype(F32)
    return ((y[:n_prompt].reshape(bp, lp, d), y[n_prompt:].reshape(bs, ls, d))
            + tuple(stack(new_p, i) for i in range(6)) + tuple(stack(new_s, i) for i in range(6)))
```

```python
import functools
import math

import numpy as np
import jax
import jax.numpy as jnp
from jax import lax
from jax.experimental import pallas as pl
from jax.experimental.pallas import tpu as pltpu

F32 = jnp.float32
BF16 = jnp.bfloat16

D_MODEL = 2048
D_MIX = 512
HEAD_DIM = 128
N_HEADS = 4
RWKV_N = 64
RWKV_HEADS = 8
RWKV_COLS = 1792
GDN_CONV = 4
N_BRANCH = 4
N_GROUPS = 4
EXPERTS_PER_GROUP = 8
N_EXPERTS = 32
TOPK = 2
D_EXPERT = 512
PAST_LEN = 4096
ROPE_BASE = 10000.0
NORM_EPS = 1e-6
GN_EPS = 1e-6
RWKV_GN_EPS = 64e-5
RET_EXP_LO, RET_EXP_HI = 5.0, 12.0

Z_RET, Z_HGRN, Z_RWKV = 0, 2048, 4096
Z_GDN = Z_RWKV + RWKV_COLS
Z_AB = Z_GDN + 4 * D_MIX
Z_GATE = Z_AB + 2 * N_HEADS
N_MAIN = 8192
MIX_W = 2048
GDN_W = 256
GROUP = 32

VMEM_LIMIT = 56 * 1024 * 1024


def _cparams(sem):
    return pltpu.CompilerParams(dimension_semantics=sem, vmem_limit_bytes=VMEM_LIMIT)


def _mm(a, b):
    return jnp.dot(a.astype(BF16), b.astype(BF16), preferred_element_type=F32)


def _mm_nt(a, b):
    return lax.dot_general(a.astype(BF16), b.astype(BF16), (((1,), (1,)), ((), ())),
                           preferred_element_type=F32)


def _mm_tn(a, b):
    return lax.dot_general(a.astype(BF16), b.astype(BF16), (((0,), (0,)), ((), ())),
                           preferred_element_type=F32)


def _silu(x):
    return x * jax.nn.sigmoid(x)


def _softplus(x):
    return jnp.maximum(x, 0.0) + jnp.log1p(jnp.exp(-jnp.abs(x)))


LANES = 128
ROW_CHUNKS = D_MODEL // LANES


def _rows_from_chunks(ref, n_rows):
    return jnp.concatenate([ref[pl.ds(c, n_rows, stride=ROW_CHUNKS), :] for c in range(ROW_CHUNKS)], axis=1)


def _rows_to_chunks(ref, val):
    n_rows = val.shape[0]
    for c in range(ROW_CHUNKS):
        ref[pl.ds(c, n_rows, stride=ROW_CHUNKS), :] = val[:, c * LANES:(c + 1) * LANES]


def _cumsum_rows(x):
    n = x.shape[0]
    row = lax.broadcasted_iota(jnp.int32, x.shape, 0)
    s = 1
    while s < n:
        x = x + jnp.where(row >= s, pltpu.roll(x, s, 0), 0.0)
        s *= 2
    return x


def _row_from_col(col, eye):
    return jnp.sum(jnp.where(eye, col, 0.0), axis=0, keepdims=True)


def _tri_inv(n_mats, c):
    r = lax.broadcasted_iota(jnp.int32, (c, c), 0)
    col = lax.broadcasted_iota(jnp.int32, (c, c), 1)
    eye = jnp.where(r == col, 1.0, 0.0)
    pair = (r >> 1) == (col >> 1)
    xs = [eye - jnp.where(pair, n, 0.0) for n in n_mats]
    m, sh = 2, 1
    while m < c:
        lvl = ((r >> (sh + 1)) == (col >> (sh + 1))) & ((r >> sh) != (col >> sh))
        ts = [_mm(jnp.where(lvl, n, 0.0), x) for n, x in zip(n_mats, xs)]
        xs = [x - _mm(x, t) for x, t in zip(xs, ts)]
        m, sh = m * 2, sh + 1
    return xs


def _ada_kernel(c_ref, w_ref, b_ref, o_ref):
    cm = _silu(c_ref[...])
    o_ref[0] = _mm(cm, w_ref[0]) + b_ref[0]


def _ada(c_all, ada_w, ada_b):
    depth, d, n = ada_w.shape
    rows = c_all.shape[0]
    tn = 1024
    return pl.pallas_call(
        _ada_kernel,
        grid=(depth, n // tn),
        in_specs=[pl.BlockSpec((rows, d), lambda l, j: (0, 0)),
                  pl.BlockSpec((1, d, tn), lambda l, j: (l, 0, j)),
                  pl.BlockSpec((1, 1, tn), lambda l, j: (l, 0, j))],
        out_specs=pl.BlockSpec((1, rows, tn), lambda l, j: (l, 0, j)),
        out_shape=jax.ShapeDtypeStruct((depth, rows, n), F32),
        compiler_params=_cparams(("arbitrary", "arbitrary")),
        name="ada",
    )(c_all, ada_w, ada_b.reshape(depth, 1, n))


def _group_block(tm, n_prompt_tiles):
    gpt = tm // GROUP
    first_sample = 16 // gpt

    def idx(i):
        return jnp.where(i < n_prompt_tiles, 0, first_sample + i - n_prompt_tiles)
    return gpt, idx


def _modulate(y, sc, sh, gpt):
    tm, d = y.shape
    y3 = y.reshape(gpt, GROUP, d)
    return (y3 * (1.0 + sc[:, None, :]) + sh[:, None, :]).reshape(tm, d)


def _rms_rows(x, g):
    return x * lax.rsqrt(jnp.mean(x * x, axis=-1, keepdims=True) + NORM_EPS) * g


def _norm_kernel(*refs, gpt, with_moe, modulated):
    it = iter(refs)
    x_ref = next(it)
    if with_moe:
        m0_ref, m1_ref, gt_ref = next(it), next(it), next(it)
    g_ref = next(it)
    if modulated:
        sc_ref, sh_ref = next(it), next(it)
    x = x_ref[...]
    tm, d = x.shape
    if with_moe:
        moe = _rows_from_chunks(m0_ref, tm) + _rows_from_chunks(m1_ref, tm)
        gt = gt_ref[0]
        x = x + (moe.reshape(gpt, GROUP, d) * gt[:, None, :]).reshape(tm, d)
    y = _rms_rows(x, g_ref[...])
    if modulated:
        xo_ref, h_ref = next(it), next(it)
        xo_ref[...] = x
        h_ref[...] = _modulate(y, sc_ref[0], sh_ref[0], gpt).astype(BF16)
    else:
        y_ref = next(it)
        y_ref[...] = y


def _norm(x, moe_out, modg, l, g, n_prompt, *, final):
    t, d = x.shape
    tm = 256
    gpt, gidx = _group_block(tm, n_prompt // tm)
    with_moe = moe_out is not None
    row = pl.BlockSpec((tm, d), lambda i: (i, 0))

    def mod_spec(col, layer):
        return pl.BlockSpec((1, gpt, d), lambda i: (layer, gidx(i), col))
    args, specs = [x], [row]
    if with_moe:
        lm = l if final else l - 1
        args += [moe_out, moe_out, modg]
        specs += [pl.BlockSpec((tm * ROW_CHUNKS, LANES), lambda i: (i, 0)),
                  pl.BlockSpec((tm * ROW_CHUNKS, LANES), lambda i: (t // tm + i, 0)), mod_spec(5, lm)]
    args.append(g.reshape(1, d))
    specs.append(pl.BlockSpec((1, d), lambda i: (0, 0)))
    if not final:
        args += [modg, modg]
        specs += [mod_spec(1, l), mod_spec(0, l)]
        out_shape = [jax.ShapeDtypeStruct((t, d), F32), jax.ShapeDtypeStruct((t, d), BF16)]
        out_specs = [row, row]
    else:
        out_shape = jax.ShapeDtypeStruct((t, d), F32)
        out_specs = row
    return pl.pallas_call(
        functools.partial(_norm_kernel, gpt=gpt, with_moe=with_moe, modulated=not final),
        grid=(t // tm,), in_specs=specs, out_specs=out_specs, out_shape=out_shape,
        compiler_params=_cparams(("arbitrary",)), name="norm",
    )(*args)


def _proj_kernel(a_ref, w_ref, o_ref, *, sigmoid):
    acc = jnp.dot(a_ref[...], w_ref[0], preferred_element_type=F32)
    if sigmoid:
        acc = jax.nn.sigmoid(acc)
    o_ref[...] = acc.astype(o_ref.dtype)


def _proj(a, w, l, *, tn, sigmoid, out_dtype):
    t, k = a.shape
    n = w.shape[2]
    tm = t // 4
    return pl.pallas_call(
        functools.partial(_proj_kernel, sigmoid=sigmoid),
        grid=(n // tn, t // tm),
        in_specs=[pl.BlockSpec((tm, k), lambda j, i: (i, 0)),
                  pl.BlockSpec((1, k, tn), lambda j, i: (l, 0, j))],
        out_specs=pl.BlockSpec((tm, tn), lambda j, i: (i, j)),
        out_shape=jax.ShapeDtypeStruct((t, n), out_dtype),
        compiler_params=_cparams(("arbitrary", "arbitrary")), name="proj",
    )(a, w)


RET_CHUNK = 256


def _ret_tables(c):
    e = np.linspace(RET_EXP_LO, RET_EXP_HI, N_HEADS)
    lg = np.log1p(-np.exp2(-e))
    t = np.arange(c, dtype=np.float64)
    rel = t[:, None] - t[None, :]
    d_intra = np.where(rel >= 0, np.exp(lg[:, None, None] * np.where(rel >= 0, rel, 0.0)), 0.0)
    d_q = np.exp(lg[:, None] * (t + 1.0))[:, :, None] * np.ones((1, 1, HEAD_DIM))
    d_k = np.exp(lg[:, None] * (c - 1.0 - t))[:, :, None] * np.ones((1, 1, HEAD_DIM))
    d_s = np.exp(lg * c)[:, None, None] * np.ones((1, 8, HEAD_DIM))
    return tuple(jnp.asarray(a, F32) for a in (d_intra, d_q, d_k, d_s))


def _ret_kernel(*refs, c, nc, has_state):
    it = iter(refs)
    z_ref, cos_ref, sin_ref, di_ref, dq_ref, dk_ref, ds_ref = (next(it) for _ in range(7))
    s0_ref = next(it) if has_state else None
    y_ref, so_ref, s_scr = next(it), next(it), next(it)
    n = pl.program_id(1)

    @pl.when(n == 0)
    def _():
        s_scr[...] = s0_ref[0] if has_state else jnp.zeros(s_scr.shape, F32)

    cos, sin = cos_ref[...], sin_ref[...]
    half = HEAD_DIM // 2
    for h in range(N_HEADS):
        lo = h * HEAD_DIM
        q = z_ref[:, lo:lo + HEAD_DIM]
        k = z_ref[:, D_MIX + lo:D_MIX + lo + HEAD_DIM]
        v = z_ref[:, 2 * D_MIX + lo:2 * D_MIX + lo + HEAD_DIM]
        g = z_ref[:, 3 * D_MIX + lo:3 * D_MIX + lo + HEAD_DIM]
        q = q * cos + pltpu.roll(q, half, 1) * sin
        k = (k * cos + pltpu.roll(k, half, 1) * sin) * HEAD_DIM ** -0.5
        s = s_scr[h]
        att = _mm_nt(q, k) * di_ref[h]
        o = _mm(att, v) + _mm(q * dq_ref[h], s)
        s_scr[h] = s * ds_ref[h, 0:1, :] + _mm_tn(k * dk_ref[h], v)
        xc = o - jnp.mean(o, axis=-1, keepdims=True)
        on = xc * lax.rsqrt(jnp.mean(xc * xc, axis=-1, keepdims=True) + GN_EPS)
        y_ref[:, lo:lo + HEAD_DIM] = (on * _silu(g)).astype(y_ref.dtype)

    @pl.when(n == nc - 1)
    def _():
        so_ref[0] = s_scr[...]


def _rope_tables(length, pos0):
    half = HEAD_DIM // 2
    inv = ROPE_BASE ** (-jnp.arange(half, dtype=F32) / half)
    pos = jnp.arange(length, dtype=F32) + pos0
    ang = pos[:, None] * inv[None, :]
    cos, sin = jnp.cos(ang), jnp.sin(ang)
    return jnp.concatenate([cos, cos], axis=1), jnp.concatenate([-sin, sin], axis=1)


def _row_block(row0, c, nc):
    off = row0 // c
    return lambda b, n: off + b * nc + n


def _retention(z, row0, bsz, length, c, pos0, state):
    nc = length // c
    rb = _row_block(row0, c, nc)
    cos, sin = _rope_tables(length, pos0)
    di, dq, dk, ds = _ret_tables(c)
    has_state = state is not None
    const3 = lambda b, n: (0, 0, 0)
    args = [z, cos, sin, di, dq, dk, ds]
    specs = [pl.BlockSpec((c, MIX_W), lambda b, n: (rb(b, n), Z_RET // MIX_W)),
             pl.BlockSpec((c, HEAD_DIM), lambda b, n: (n, 0)),
             pl.BlockSpec((c, HEAD_DIM), lambda b, n: (n, 0)),
             pl.BlockSpec(di.shape, const3), pl.BlockSpec(dq.shape, const3),
             pl.BlockSpec(dk.shape, const3), pl.BlockSpec(ds.shape, const3)]
    sshape = (N_HEADS, HEAD_DIM, HEAD_DIM)
    if has_state:
        args.append(state)
        specs.append(pl.BlockSpec((1,) + sshape, lambda b, n: (b, 0, 0, 0)))
    return pl.pallas_call(
        functools.partial(_ret_kernel, c=c, nc=nc, has_state=has_state),
        grid=(bsz, nc), in_specs=specs,
        out_specs=[pl.BlockSpec((c, D_MIX), lambda b, n: (b * nc + n, 0)),
                   pl.BlockSpec((1,) + sshape, lambda b, n: (b, 0, 0, 0))],
        out_shape=[jax.ShapeDtypeStruct((bsz * length, D_MIX), BF16),
                   jax.ShapeDtypeStruct((bsz,) + sshape, F32)],
        scratch_shapes=[pltpu.VMEM(sshape, F32)],
        compiler_params=_cparams(("arbitrary", "arbitrary")), name="retention",
    )(*args)


SUB = 16


def _gla_head(q, k, v, g, st, c):
    big = _cumsum_rows(g)
    gl = big[c - 1:c, :]
    o = _mm_nt(q * jnp.exp(big), st)
    st_new = st * jnp.exp(gl) + _mm_tn(v, k * jnp.exp(gl - big))

    row = lax.broadcasted_iota(jnp.int32, (c, HEAD_DIM), 0)
    r2 = lax.broadcasted_iota(jnp.int32, (c, c), 0)
    c2 = lax.broadcasted_iota(jnp.int32, (c, c), 1)
    att = jnp.zeros((c, c), F32)
    m, sh = SUB, SUB.bit_length() - 1
    while m < c:
        anchor = jnp.concatenate(
            [jnp.broadcast_to(big[p * 2 * m + m - 1:p * 2 * m + m, :], (2 * m, HEAD_DIM))
             for p in range(c // (2 * m))], axis=0)
        right = ((row >> sh) & 1) == 1
        qt = jnp.where(right, q * jnp.exp(jnp.where(right, big - anchor, 0.0)), 0.0)
        kt = jnp.where(right, 0.0, k * jnp.exp(jnp.where(right, 0.0, anchor - big)))
        att = att + jnp.where((r2 >> (sh + 1)) == (c2 >> (sh + 1)), _mm_nt(qt, kt), 0.0)
        m, sh = m * 2, sh + 1
    o = o + _mm(att, v)

    f = jnp.exp(g)
    rsub = lax.broadcasted_iota(jnp.int32, (SUB, HEAD_DIM), 0)
    diag = []
    for blk in range(c // SUB):
        r0 = blk * SUB
        vb = v[r0:r0 + SUB, :]
        w = jnp.zeros((SUB, HEAD_DIM), F32)
        ob = jnp.zeros((SUB, HEAD_DIM), F32)
        for tl in range(SUB):
            t = r0 + tl
            if tl:
                w = w * f[t:t + 1, :]
            w = jnp.where(rsub == tl, k[t:t + 1, :], w)
            a_col = jnp.sum(w * q[t:t + 1, :], axis=1, keepdims=True)
            o_t = jnp.sum(a_col * vb, axis=0, keepdims=True)
            ob = jnp.where(rsub == tl, o_t, ob)
        diag.append(ob)
    return o + jnp.concatenate(diag, axis=0), st_new


def _hgrn_kernel(*refs, c, nc, has_state):
    it = iter(refs)
    z_ref, llb_ref, l1m_ref, oml_ref, gn_ref = (next(it) for _ in range(5))
    s0_ref = next(it) if has_state else None
    y_ref, so_ref, s_scr = next(it), next(it), next(it)
    n = pl.program_id(1)

    @pl.when(n == 0)
    def _():
        for h in range(N_HEADS):
            s_scr[h] = s0_ref[0, h].T if has_state else jnp.zeros((HEAD_DIM, HEAD_DIM), F32)

    for h in range(N_HEADS):
        lo = h * HEAD_DIM
        q = z_ref[:, lo:lo + HEAD_DIM]
        f = z_ref[:, D_MIX + lo:D_MIX + lo + HEAD_DIM]
        iv = z_ref[:, 2 * D_MIX + lo:2 * D_MIX + lo + HEAD_DIM]
        g = z_ref[:, 3 * D_MIX + lo:3 * D_MIX + lo + HEAD_DIM]
        ls = jnp.minimum(f, 0.0) - jnp.log1p(jnp.exp(-jnp.abs(f)))
        a = llb_ref[:, lo:lo + HEAD_DIM]
        b = l1m_ref[:, lo:lo + HEAD_DIM] + ls
        logf = jnp.maximum(a, b) + jnp.log1p(jnp.exp(-jnp.abs(a - b)))
        kf = oml_ref[:, lo:lo + HEAD_DIM] * jax.nn.sigmoid(-f)
        o, st = _gla_head(q, kf, iv, logf, s_scr[h], c)
        s_scr[h] = st
        y = _rms_rows(o, gn_ref[...]) * _silu(g)
        y_ref[:, lo:lo + HEAD_DIM] = y.astype(y_ref.dtype)

    @pl.when(n == nc - 1)
    def _():
        for h in range(N_HEADS):
            so_ref[0, h] = s_scr[h].T


def _hgrn(z, row0, bsz, length, c, lb, gnorm, state):
    nc = length // c
    rb = _row_block(row0, c, nc)
    has_state = state is not None
    vec = pl.BlockSpec((1, D_MIX), lambda b, n: (0, 0))
    lb = lb.reshape(1, D_MIX)
    args = [z, jnp.log(lb), jnp.log1p(-lb), 1.0 - lb, gnorm.reshape(1, HEAD_DIM)]
    specs = [pl.BlockSpec((c, MIX_W), lambda b, n: (rb(b, n), Z_HGRN // MIX_W)), vec, vec, vec,
             pl.BlockSpec((1, HEAD_DIM), lambda b, n: (0, 0))]
    sshape = (N_HEADS, HEAD_DIM, HEAD_DIM)
    if has_state:
        args.append(state)
        specs.append(pl.BlockSpec((1,) + sshape, lambda b, n: (b, 0, 0, 0)))
    return pl.pallas_call(
        functools.partial(_hgrn_kernel, c=c, nc=nc, has_state=has_state),
        grid=(bsz, nc), in_specs=specs,
        out_specs=[pl.BlockSpec((c, D_MIX), lambda b, n: (b * nc + n, 0)),
                   pl.BlockSpec((1,) + sshape, lambda b, n: (b, 0, 0, 0))],
        out_shape=[jax.ShapeDtypeStruct((bsz * length, D_MIX), BF16),
                   jax.ShapeDtypeStruct((bsz,) + sshape, F32)],
        scratch_shapes=[pltpu.VMEM(sshape, F32)],
        compiler_params=_cparams(("arbitrary", "arbitrary")), name="hgrn2",
    )(*args)


def _rwkv_kernel(*refs, c, nsub, nc, has_state):
    it = iter(refs)
    (z_ref, mu_ref, w0_ref, w2_ref, a0_ref, a2_ref, g2_ref, kk_ref, ka_ref, rk_ref,
     lnw_ref, lnb_ref) = (next(it) for _ in range(12))
    if has_state:
        sh0_ref, s0_ref = next(it), next(it)
    y_ref, so_ref, s_scr, prev_scr = next(it), next(it), next(it), next(it)
    n = pl.program_id(1)
    nh, hd = RWKV_HEADS, RWKV_N

    @pl.when(n == 0)
    def _():
        s_scr[...] = s0_ref[0] if has_state else jnp.zeros(s_scr.shape, F32)
        prev_scr[...] = (jnp.broadcast_to(sh0_ref[0], prev_scr.shape) if has_state
                         else jnp.zeros(prev_scr.shape, F32))

    z = z_ref[...]
    rows = nsub * c
    row = lax.broadcasted_iota(jnp.int32, z.shape, 0)
    prev = jnp.where(row == 0, prev_scr[0:1, :], pltpu.roll(z, 1, 0))
    prev_scr[...] = jnp.broadcast_to(z[rows - 1:rows, :], prev_scr.shape)
    xs = z + (prev - z) * mu_ref[...]
    o3 = 3 * D_MIX
    r, k, v = xs[:, :D_MIX], xs[:, D_MIX:2 * D_MIX], xs[:, 2 * D_MIX:o3]
    w_lo, a_lo, g_lo = xs[:, o3:o3 + 64], xs[:, o3 + 64:o3 + 128], xs[:, o3 + 128:o3 + 256]
    w = -_softplus(-(w0_ref[...] + _mm(jnp.tanh(w_lo), w2_ref[...]))) - 0.5
    ld = -jnp.exp(w)
    a = jax.nn.sigmoid(a0_ref[...] + _mm(a_lo, a2_ref[...]))
    g = _mm(jax.nn.sigmoid(g_lo), g2_ref[...])
    kkv = k * kk_ref[...]
    k = k * (1.0 + (a - 1.0) * ka_ref[...])
    bonus = r * k * rk_ref[...]

    r2 = lax.broadcasted_iota(jnp.int32, (2 * c, 2 * c), 0)
    c2 = lax.broadcasted_iota(jnp.int32, (2 * c, 2 * c), 1)
    tq = jnp.where(r2 >= c, r2 - c, r2)
    ts = jnp.where(c2 >= c, c2 - c, c2)
    keep = (tq > ts) | ((r2 >= c) & (tq == ts))
    heads, subs = range(nh), range(nsub)
    pairs = [(j, h) for j in subs for h in heads]
    sls = [slice(h * hd, (h + 1) * hd) for h in heads]
    lhs, rhs, vs, gam = {}, {}, {}, {}
    for j in subs:
        rs = slice(j * c, (j + 1) * c)
        ldj = ld[rs]
        lc = _cumsum_rows(ldj)
        e_in, e_in_neg, e_ex = jnp.exp(lc), jnp.exp(-lc), jnp.exp(lc - ldj)
        for h in heads:
            sl = sls[h]
            kkh = kkv[rs, sl]
            kap = kkh * lax.rsqrt(jnp.sum(kkh * kkh, axis=-1, keepdims=True) + 1e-6)
            lhs[j, h] = jnp.concatenate([kap * e_ex[:, sl], r[rs, sl] * e_in[:, sl]], axis=0)
            rhs[j, h] = jnp.concatenate([kap * a[rs, sl] * e_in_neg[:, sl], k[rs, sl] * e_in_neg[:, sl]], axis=0)
            vs[j, h] = v[rs, sl]
            gam[j, h] = e_in[c - 1:c, sl]
    pm = {p: jnp.where(keep, _mm_nt(lhs[p], rhs[p]), 0.0) for p in pairs}
    zv = jnp.zeros((c, hd), F32)
    mkv = {p: _mm(pm[p][:c, :], jnp.concatenate([zv, vs[p]], axis=0)) for p in pairs}
    t_inv = dict(zip(pairs, _tri_inv([pm[p][:c, :c] for p in pairs], c)))
    s_cur = [s_scr[h] for h in heads]
    ys = {}
    for j in subs:
        qs = [_mm_nt(lhs[j, h], s_cur[h]) for h in heads]
        us = [-_mm(t_inv[j, h], qs[h][:c, :] + mkv[j, h]) for h in heads]
        uv = [jnp.concatenate([us[h], vs[j, h]], axis=0) for h in heads]
        for h in heads:
            ys[j, h] = qs[h][c:, :] + _mm(pm[j, h][c:, :], uv[h])
        s_cur = [(s_cur[h] + _mm_tn(uv[h], rhs[j, h])) * gam[j, h] for h in heads]
    for h in heads:
        s_scr[h] = s_cur[h]
    for j, h in pairs:
        rs, sl = slice(j * c, (j + 1) * c), sls[h]
        y = ys[j, h]
        yc = y - jnp.mean(y, axis=-1, keepdims=True)
        yn = yc * lax.rsqrt(jnp.mean(yc * yc, axis=-1, keepdims=True) + RWKV_GN_EPS)
        yn = yn * lnw_ref[:, sl] + lnb_ref[:, sl]
        yn = yn + jnp.sum(bonus[rs, sl], axis=-1, keepdims=True) * vs[j, h]
        y_ref[rs, sl] = (yn * g[rs, sl]).astype(y_ref.dtype)

    @pl.when(n == nc - 1)
    def _():
        so_ref[0] = s_scr[...]


def _sub_chunks(length, c, most):
    nc = length // c
    while nc % most:
        most //= 2
    return most


def _rwkv(z, row0, bsz, length, c, p, shift, state):
    nsub = _sub_chunks(length, c, 2)
    cb = c * nsub
    nc = length // cb
    rb = _row_block(row0, cb, nc)
    has_state = state is not None
    pad = MIX_W - RWKV_COLS
    c2 = lambda b, n: (0, 0)
    args = [z, jnp.pad(p["mu"], (0, pad)).reshape(1, MIX_W),
            p["w0"].reshape(1, D_MIX), p["w2"], p["a0"].reshape(1, D_MIX), p["a2"], p["g2"],
            p["k_k"].reshape(1, D_MIX), p["k_a"].reshape(1, D_MIX), p["r_k"].reshape(1, D_MIX),
            p["ln_w"].reshape(1, D_MIX), p["ln_b"].reshape(1, D_MIX)]
    specs = [pl.BlockSpec((cb, MIX_W), lambda b, n: (rb(b, n), Z_RWKV // MIX_W))]
    specs += [pl.BlockSpec(a.shape, c2) for a in args[1:]]
    sshape = (RWKV_HEADS, RWKV_N, RWKV_N)
    if has_state:
        args += [jnp.pad(shift, ((0, 0), (0, pad))).reshape(bsz, 1, MIX_W), state]
        specs += [pl.BlockSpec((1, 1, MIX_W), lambda b, n: (b, 0, 0)),
                  pl.BlockSpec((1,) + sshape, lambda b, n: (b, 0, 0, 0))]
    return pl.pallas_call(
        functools.partial(_rwkv_kernel, c=c, nsub=nsub, nc=nc, has_state=has_state),
        grid=(bsz, nc), in_specs=specs,
        out_specs=[pl.BlockSpec((cb, D_MIX), lambda b, n: (b * nc + n, 0)),
                   pl.BlockSpec((1,) + sshape, lambda b, n: (b, 0, 0, 0))],
        out_shape=[jax.ShapeDtypeStruct((bsz * length, D_MIX), BF16),
                   jax.ShapeDtypeStruct((bsz,) + sshape, F32)],
        scratch_shapes=[pltpu.VMEM(sshape, F32), pltpu.VMEM((8, MIX_W), F32)],
        compiler_params=_cparams(("arbitrary", "arbitrary")), name="rwkv7",
    )(*args)


CONV_PAD = 8


def _gdn_kernel(*refs, c, nsub, nc, has_state):
    it = iter(refs)
    n_win = 4 * D_MIX // GDN_W
    z_refs = [next(it) for _ in range(n_win)]
    ab_ref, cw_ref, alog_ref, dtb_ref, gn_ref = (next(it) for _ in range(5))
    if has_state:
        cv0_ref, s0_ref = next(it), next(it)
    y_ref, so_ref, s_scr, xp_scr = next(it), next(it), next(it), next(it)
    n = pl.program_id(1)
    qkv_w = 3 * D_MIX
    qkv_win = qkv_w // GDN_W
    tail = GDN_CONV - 1
    rows = nsub * c

    @pl.when(n == 0)
    def _():
        s_scr[...] = s0_ref[0] if has_state else jnp.zeros(s_scr.shape, F32)
        xp_scr[0:CONV_PAD, :] = jnp.zeros((CONV_PAD, qkv_w), F32)
        if has_state:
            xp_scr[CONV_PAD - tail:CONV_PAD, :] = cv0_ref[0]

    for j in range(qkv_win):
        xp_scr[CONV_PAD:CONV_PAD + rows, j * GDN_W:(j + 1) * GDN_W] = z_refs[j][...]
    conv = xp_scr[CONV_PAD - tail:CONV_PAD - tail + rows, :] * cw_ref[0:1, :]
    for j in range(1, GDN_CONV):
        conv = conv + xp_scr[CONV_PAD - tail + j:CONV_PAD - tail + j + rows, :] * cw_ref[j:j + 1, :]
    xp_scr[CONV_PAD - tail:CONV_PAD, :] = xp_scr[CONV_PAD + rows - tail:CONV_PAD + rows, :]
    act = _silu(conv)

    ab = ab_ref[...]
    gdec = -jnp.exp(alog_ref[...]) * _softplus(ab + dtb_ref[...])
    beta = jax.nn.sigmoid(ab)

    r2 = lax.broadcasted_iota(jnp.int32, (c, c), 0)
    c2 = lax.broadcasted_iota(jnp.int32, (c, c), 1)
    eye, causal, strict = r2 == c2, r2 >= c2, r2 > c2
    heads, subs = range(N_HEADS), range(nsub)
    pairs = [(j, h) for j in subs for h in heads]
    kq, ks, vs, gcols, bcols, decs = {}, {}, {}, {}, {}, {}
    for j in subs:
        rs = slice(j * c, (j + 1) * c)
        gcum = _cumsum_rows(gdec[rs])
        for h in heads:
            lo = h * HEAD_DIM
            q = act[rs, lo:lo + HEAD_DIM]
            k = act[rs, D_MIX + lo:D_MIX + lo + HEAD_DIM]
            q = q * lax.rsqrt(jnp.sum(q * q, axis=-1, keepdims=True) + 1e-6) * HEAD_DIM ** -0.5
            k = k * lax.rsqrt(jnp.sum(k * k, axis=-1, keepdims=True) + 1e-6)
            gcol = gcum[:, h:h + 1]
            rel = gcol - _row_from_col(gcol, eye)
            kq[j, h] = jnp.concatenate([k, q], axis=0)
            ks[j, h] = k
            vs[j, h] = act[rs, 2 * D_MIX + lo:2 * D_MIX + lo + HEAD_DIM]
            gcols[j, h] = gcol
            bcols[j, h] = beta[rs, N_HEADS + h:N_HEADS + h + 1]
            decs[j, h] = jnp.where(causal, jnp.exp(jnp.where(causal, rel, 0.0)), 0.0)
    kk_qk = {p: _mm_nt(kq[p], ks[p]) for p in pairs}
    t_inv = dict(zip(pairs, _tri_inv(
        [jnp.where(strict, bcols[p] * kk_qk[p][:c] * decs[p], 0.0) for p in pairs], c)))
    s_cur = [s_scr[h] for h in heads]
    os_ = {}
    for j in subs:
        ks_qs = [_mm(kq[j, h], s_cur[h]) for h in heads]
        egs = [jnp.exp(gcols[j, h]) for h in heads]
        us = [_mm(t_inv[j, h], bcols[j, h] * (vs[j, h] - egs[h] * ks_qs[h][:c])) for h in heads]
        for h in heads:
            os_[j, h] = egs[h] * ks_qs[h][c:] + _mm(kk_qk[j, h][c:] * decs[j, h], us[h])
        gls = [gcols[j, h][c - 1:c, :] for h in heads]
        s_cur = [jnp.exp(gls[h]) * s_cur[h] + _mm_tn(ks[j, h] * jnp.exp(gls[h] - gcols[j, h]), us[h])
                 for h in heads]
    for h in heads:
        s_scr[h] = s_cur[h]
    for j, h in pairs:
        rs, lo = slice(j * c, (j + 1) * c), h * HEAD_DIM
        zg = z_refs[qkv_win + lo // GDN_W][rs, lo % GDN_W:lo % GDN_W + HEAD_DIM]
        y = _rms_rows(os_[j, h], gn_ref[...]) * _silu(zg)
        y_ref[rs, lo:lo + HEAD_DIM] = y.astype(y_ref.dtype)

    @pl.when(n == nc - 1)
    def _():
        so_ref[0] = s_scr[...]


def _gdn(z, row0, bsz, length, c, p, conv_state, state):
    nsub = _sub_chunks(length, c, 4)
    cb = c * nsub
    nc = length // cb
    rb = _row_block(row0, cb, nc)
    has_state = state is not None
    c2 = lambda b, n: (0, 0)
    lane_pad = lambda a: jnp.pad(a, (0, HEAD_DIM - a.shape[0])).reshape(1, HEAD_DIM)
    n_win = 4 * D_MIX // GDN_W
    params = [p["conv_w"], lane_pad(p["A_log"]), lane_pad(p["dt_bias"]), p["norm_g"].reshape(1, HEAD_DIM)]
    args = [z] * (n_win + 1) + params
    specs = [pl.BlockSpec((cb, GDN_W), lambda b, n, j=j: (rb(b, n), Z_GDN // GDN_W + j)) for j in range(n_win)]
    specs.append(pl.BlockSpec((cb, HEAD_DIM), lambda b, n: (rb(b, n), Z_AB // HEAD_DIM)))
    specs += [pl.BlockSpec(a.shape, c2) for a in params]
    sshape = (N_HEADS, HEAD_DIM, HEAD_DIM)
    if has_state:
        args += [conv_state, state]
        specs += [pl.BlockSpec((1, GDN_CONV - 1, 3 * D_MIX), lambda b, n: (b, 0, 0)),
                  pl.BlockSpec((1,) + sshape, lambda b, n: (b, 0, 0, 0))]
    return pl.pallas_call(
        functools.partial(_gdn_kernel, c=c, nsub=nsub, nc=nc, has_state=has_state),
        grid=(bsz, nc), in_specs=specs,
        out_specs=[pl.BlockSpec((cb, D_MIX), lambda b, n: (b * nc + n, 0)),
                   pl.BlockSpec((1,) + sshape, lambda b, n: (b, 0, 0, 0))],
        out_shape=[jax.ShapeDtypeStruct((bsz * length, D_MIX), BF16),
                   jax.ShapeDtypeStruct((bsz,) + sshape, F32)],
        scratch_shapes=[pltpu.VMEM(sshape, F32), pltpu.VMEM((CONV_PAD + cb, 3 * D_MIX), F32)],
        compiler_params=_cparams(("arbitrary", "arbitrary")), name="gdn",
    )(*args)


def _merge_kernel(y0_ref, y1_ref, y2_ref, y3_ref, gate_ref, x_ref, gt_ref, sc_ref, sh_ref, g_ref,
                  wb_ref, wo_ref, rw_ref, rb_ref, xo_ref, h_ref, lg_ref, *, gpt):
    tm, d = x_ref.shape
    merged = jnp.zeros((tm, d), F32)
    for nb, y_ref in enumerate((y0_ref, y1_ref, y2_ref, y3_ref)):
        br = jnp.dot(y_ref[...], wb_ref[0, nb], preferred_element_type=F32)
        merged = merged + gate_ref[:, nb * d:(nb + 1) * d].astype(F32) * br
    m = jnp.dot(merged.astype(BF16), wo_ref[0], preferred_element_type=F32)
    gt = gt_ref[0]
    x = x_ref[...] + (m.reshape(gpt, GROUP, d) * gt[:, None, :]).reshape(tm, d)
    xo_ref[...] = x
    h = _modulate(_rms_rows(x, g_ref[...]), sc_ref[0], sh_ref[0], gpt)
    _rows_to_chunks(h_ref, h)
    lg_ref[...] = _mm(h, rw_ref[0]) + rb_ref[0]


def _merge(ys, gates, x, modg, l, g2, wb, wo, rw, rb, n_prompt):
    t, d = x.shape
    tm = 256
    gpt, gidx = _group_block(tm, n_prompt // tm)
    row = lambda w: pl.BlockSpec((tm, w), lambda i: (i, 0))

    def mod_spec(col):
        return pl.BlockSpec((1, gpt, d), lambda i: (l, gidx(i), col))

    def const(shape):
        return pl.BlockSpec(shape, lambda i: (0,) * len(shape), pipeline_mode=pl.Buffered(1))

    def layer(arr):
        return pl.BlockSpec((1,) + arr.shape[1:], lambda i: (l,) + (0,) * (arr.ndim - 1),
                            pipeline_mode=pl.Buffered(1))
    nr = rw.shape[2]
    return pl.pallas_call(
        functools.partial(_merge_kernel, gpt=gpt),
        grid=(t // tm,),
        in_specs=[row(D_MIX)] * 4 + [row(N_BRANCH * d), row(d), mod_spec(2), mod_spec(4), mod_spec(3),
                                     const((1, d)), layer(wb), layer(wo), layer(rw), layer(rb)],
        out_specs=[row(d), pl.BlockSpec((tm * ROW_CHUNKS, LANES), lambda i: (i, 0)), row(nr)],
        out_shape=[jax.ShapeDtypeStruct((t, d), F32), jax.ShapeDtypeStruct((t * ROW_CHUNKS, LANES), F32),
                   jax.ShapeDtypeStruct((t, nr), F32)],
        compiler_params=_cparams(("arbitrary",)), name="merge",
    )(*ys, gates, x, modg, modg, modg, g2.reshape(1, d), wb, wo, rw, rb)


MOE_BLK = 256
MOE_UNROLL = 8


def _moe_kernel(be_ref, tok_ref, slot_ref, nused_ref, h_hbm, w_ref, wg_ref, wu_ref, wd_ref, o_hbm,
                x0, x1, y0, y1, wg_bf, wu_bf, wd_bf, gsem, ssem):
    i = pl.program_id(0)
    nused = nused_ref[0]

    rc = ROW_CHUNKS

    def gather(blk, xb, q, r):
        return pltpu.make_async_copy(h_hbm.at[pl.ds(tok_ref[blk * MOE_BLK + r] * rc, rc)],
                                     xb.at[pl.ds(r * rc, rc)], gsem.at[q])

    def scatter(blk, yb, q, r):
        return pltpu.make_async_copy(yb.at[pl.ds(r * rc, rc)],
                                     o_hbm.at[pl.ds(slot_ref[(blk + 1) * MOE_BLK + r] * rc, rc)], ssem.at[q])

    def for_rows(fn):
        def group(j, carry):
            for u in range(MOE_UNROLL):
                fn(j * MOE_UNROLL + u)
            return carry
        lax.fori_loop(0, MOE_BLK // MOE_UNROLL, group, 0)

    def step(q, xa, ya, xb, yb):
        @pl.when(i == 0)
        def _():
            yb[...] = jnp.zeros(yb.shape, F32)
            n_real = o_hbm.shape[0] - 2 * MOE_BLK * rc
            init = pltpu.make_async_copy(yb, o_hbm.at[pl.ds(n_real, MOE_BLK * rc)], ssem.at[q])
            init.start()
            init.wait()
            for_rows(lambda r: gather(i, xa, q, r).start())

        @pl.when(i > 0)
        def _():
            for_rows(lambda r: scatter(i - 2, ya, q, r).wait())
        for_rows(lambda r: gather(i, xa, q, r).wait())

        @pl.when((i == 0) | (be_ref[i] != be_ref[jnp.maximum(i - 1, 0)]))
        def _():
            wg_bf[...] = wg_ref[0, 0].astype(BF16)
            wu_bf[...] = wu_ref[0, 0].astype(BF16)
            wd_bf[...] = wd_ref[0, 0].astype(BF16)

        for r in range(MOE_BLK):
            gather(i + 1, xb, 1 - q, r).start()
        for r in range(MOE_BLK):
            scatter(i - 1, yb, 1 - q, r).start()
        x = _rows_from_chunks(xa, MOE_BLK).astype(BF16)
        hid = _silu(jnp.dot(x, wg_bf[...], preferred_element_type=F32)) * jnp.dot(
            x, wu_bf[...], preferred_element_type=F32)
        _rows_to_chunks(ya, jnp.dot(hid.astype(BF16), wd_bf[...], preferred_element_type=F32) * w_ref[...])

        @pl.when(i == nused - 1)
        def _():
            for_rows(lambda r: scatter(i, ya, q, r).start())
            for_rows(lambda r: scatter(i - 1, yb, 1 - q, r).wait())
            for_rows(lambda r: scatter(i, ya, q, r).wait())
            for_rows(lambda r: gather(i + 1, xb, 1 - q, r).wait())

    @pl.when((i < nused) & (i % 2 == 0))
    def _():
        step(0, x0, y0, x1, y1)

    @pl.when((i < nused) & (i % 2 == 1))
    def _():
        step(1, x1, y1, x0, y0)


def _route(logits, n_tok):
    lt = logits[:, :N_GROUPS + N_EXPERTS].T
    lg = lt[:N_GROUPS]
    gsel = jnp.argmax(lg, axis=0).astype(jnp.int32)
    gw = jnp.max(jax.nn.softmax(lg, axis=0), axis=0)
    le_all = lt[N_GROUPS:].reshape(N_GROUPS, EXPERTS_PER_GROUP, n_tok)
    le = sum(jnp.where(gsel[None, :] == g, le_all[g], 0.0) for g in range(N_GROUPS))
    pe = jax.nn.softmax(le, axis=0)
    rows8 = lax.broadcasted_iota(jnp.int32, pe.shape, 0)
    i1 = jnp.argmax(pe, axis=0).astype(jnp.int32)
    v1 = jnp.max(pe, axis=0)
    pe_rest = jnp.where(rows8 == i1[None, :], -jnp.inf, pe)
    i2 = jnp.argmax(pe_rest, axis=0).astype(jnp.int32)
    v2 = jnp.max(pe_rest, axis=0)
    top_v = jnp.stack([v1, v2], axis=1)
    top_i = jnp.stack([i1, i2], axis=1)
    wts = top_v / jnp.sum(top_v, axis=-1, keepdims=True) * gw[:, None]
    eid = (gsel[:, None] * EXPERTS_PER_GROUP + top_i).reshape(-1).astype(jnp.int32)
    a = n_tok * TOPK
    order = jnp.argsort(eid).astype(jnp.int32)
    onehot = (eid[None, :] == jnp.arange(N_EXPERTS, dtype=jnp.int32)[:, None]).astype(BF16)
    counts = jnp.dot(onehot, jnp.ones((a, 1), BF16), preferred_element_type=F32)[:, 0].astype(jnp.int32)
    padded = (counts + MOE_BLK - 1) // MOE_BLK * MOE_BLK
    pend = jnp.cumsum(padded)
    pstart = pend - padded
    cstart = jnp.cumsum(counts) - counts
    n_blocks = -(-a // MOE_BLK) + N_EXPERTS
    rows = n_blocks * MOE_BLK
    blk_row0 = jnp.arange(n_blocks, dtype=jnp.int32) * MOE_BLK
    block_e = jnp.minimum(jnp.sum((pend[None, :] <= blk_row0[:, None]).astype(jnp.int32), axis=1),
                          N_EXPERTS - 1).astype(jnp.int32)
    nused = (pend[-1:] // MOE_BLK).astype(jnp.int32)
    e_row = jnp.repeat(block_e, MOE_BLK)
    row_id = jnp.arange(rows, dtype=jnp.int32)
    j_row = row_id - pstart[e_row]
    valid = j_row < counts[e_row]
    asg = order[jnp.clip(cstart[e_row] + j_row, 0, a - 1)]
    tok = jnp.where(valid, asg // TOPK, 0)
    dummy = a + ((row_id // MOE_BLK) % 2) * MOE_BLK + row_id % MOE_BLK
    slot = jnp.where(valid, (asg % TOPK) * n_tok + asg // TOPK, dummy)
    wrow = jnp.where(valid, wts.reshape(-1)[asg], 0.0)
    tok = jnp.concatenate([tok, jnp.zeros((MOE_BLK,), jnp.int32)])
    slot = jnp.concatenate([a + MOE_BLK + jnp.arange(MOE_BLK, dtype=jnp.int32), slot])
    return block_e, tok, slot, nused, wrow.reshape(rows, 1), n_blocks


def _moe(h, logits, wg, wu, wd, l):
    t, d = h.shape[0] // ROW_CHUNKS, D_MODEL
    block_e, tok, slot, nused, wrow, n_blocks = _route(logits, t)
    grid_spec = pltpu.PrefetchScalarGridSpec(
        num_scalar_prefetch=4, grid=(n_blocks,),
        in_specs=[pl.BlockSpec(memory_space=pl.ANY),
                  pl.BlockSpec((MOE_BLK, 1), lambda i, be, *_: (i, 0)),
                  pl.BlockSpec((1, 1, d, D_EXPERT), lambda i, be, *_: (l, be[i], 0, 0)),
                  pl.BlockSpec((1, 1, d, D_EXPERT), lambda i, be, *_: (l, be[i], 0, 0)),
                  pl.BlockSpec((1, 1, D_EXPERT, d), lambda i, be, *_: (l, be[i], 0, 0))],
        out_specs=pl.BlockSpec(memory_space=pl.ANY),
        scratch_shapes=[pltpu.VMEM((MOE_BLK * ROW_CHUNKS, LANES), F32)] * 4 + [
                        pltpu.VMEM((d, D_EXPERT), BF16), pltpu.VMEM((d, D_EXPERT), BF16),
                        pltpu.VMEM((D_EXPERT, d), BF16),
                        pltpu.SemaphoreType.DMA((2,)), pltpu.SemaphoreType.DMA((2,))])
    out = pl.pallas_call(
        _moe_kernel, grid_spec=grid_spec,
        out_shape=jax.ShapeDtypeStruct(((t * TOPK + 2 * MOE_BLK) * ROW_CHUNKS, LANES), F32),
        compiler_params=_cparams(("arbitrary",)), name="moe",
    )(block_e, tok, slot, nused, h, wrow, wg, wu, wd)
    return out


def _pack_w_in(w_in):
    cut = -(-Z_GATE // LANES) * LANES
    w_main = jnp.pad(w_in[:, :, :cut].astype(BF16), ((0, 0), (0, 0), (0, N_MAIN - cut)))
    return w_main, w_in[:, :, Z_GATE:].astype(BF16)


def _mixers(z, l, groups, params):
    outs = [[] for _ in range(N_BRANCH)]
    states = []
    for (row0, bsz, length, c, pos0, st) in groups:
        s_ret, s_hgrn, s_rwkv, s_shift, s_gdn, s_conv = st if st is not None else (None,) * 6
        y_a, n_ret = _retention(z, row0, bsz, length, min(length, RET_CHUNK), pos0, s_ret)
        y_b, n_hgrn = _hgrn(z, row0, bsz, length, c, params["hgrn_lb"][l], params["hgrn_norm_g"][l], s_hgrn)
        y_c, n_rwkv = _rwkv(z, row0, bsz, length, c, {k: v[l] for k, v in params["rwkv"].items()},
                            s_shift, s_rwkv)
        y_d, n_gdn = _gdn(z, row0, bsz, length, c, {k: v[l] for k, v in params["gdn"].items()},
                          s_conv, s_gdn)
        for lst, y in zip(outs, (y_a, y_b, y_c, y_d)):
            lst.append(y)
        def seq_row(j, col0, width):
            return lax.slice(z, (row0 + j, col0), (row0 + (bsz - 1) * length + j + 1, col0 + width),
                             (length, 1))
        n_shift = seq_row(length - 1, Z_RWKV, RWKV_COLS)
        n_conv = jnp.stack([seq_row(length - (GDN_CONV - 1) + j, Z_GDN, 3 * D_MIX)
                            for j in range(GDN_CONV - 1)], axis=1)
        states.append((n_ret, n_hgrn, n_rwkv, n_shift, n_gdn, n_conv))
    return [jnp.concatenate(lst, axis=0) for lst in outs], states


def kernel(x_prompt, x_sample, c_prompt, c_sample, state_ret, state_hgrn, state_rwkv, state_rwkv_shift,
           state_gdn, state_gdn_conv, ada_w, ada_b, norm1_g, norm2_g, w_in, hgrn_lb_logits, hgrn_norm_g,
           rwkv_mu, rwkv_w0, rwkv_w2, rwkv_a0, rwkv_a2, rwkv_g2, rwkv_k_k, rwkv_k_a, rwkv_r_k, rwkv_ln_w,
           rwkv_ln_b, gdn_conv_w, gdn_A_log, gdn_dt_bias, gdn_norm_g, w_branch, w_out, router_g, router_g_b,
           router_e, router_e_b, moe_w_gate, moe_w_up, moe_w_down, final_norm_g):
    depth = ada_w.shape[0]
    bp, lp, d = x_prompt.shape
    bs, ls, _ = x_sample.shape
    n_prompt = bp * lp
    assert bp == 1 and ls == GROUP and bs == 16

    lb_cum = jnp.cumsum(jax.nn.softmax(hgrn_lb_logits.astype(F32), axis=0), axis=0)
    params = dict(
        hgrn_lb=lb_cum - lb_cum[:1], hgrn_norm_g=hgrn_norm_g,
        rwkv=dict(mu=rwkv_mu, w0=rwkv_w0, w2=rwkv_w2, a0=rwkv_a0, a2=rwkv_a2, g2=rwkv_g2, k_k=rwkv_k_k,
                  k_a=rwkv_k_a, r_k=rwkv_r_k.reshape(depth, D_MIX), ln_w=rwkv_ln_w, ln_b=rwkv_ln_b),
        gdn=dict(conv_w=gdn_conv_w, A_log=gdn_A_log, dt_bias=gdn_dt_bias, norm_g=gdn_norm_g))

    c_all = jnp.concatenate([c_prompt, c_sample, jnp.zeros((24 - bp - bs, d), F32)], axis=0)
    mod = _ada(c_all, ada_w, ada_b)
    modg = jnp.concatenate([jnp.broadcast_to(mod[:, :1], (depth, 16, 6 * d)), mod[:, 1:1 + bs]], axis=1)

    w_main, w_gate = _pack_w_in(w_in)
    wb_bf, wo_bf = w_branch.astype(BF16), w_out.astype(BF16)
    n_r = 128
    r_w = jnp.concatenate([router_g, router_e, jnp.zeros((depth, d, n_r - N_GROUPS - N_EXPERTS), F32)], axis=2)
    r_b = jnp.concatenate([router_g_b, router_e_b, jnp.zeros((depth, n_r - N_GROUPS - N_EXPERTS), F32)],
                          axis=1).reshape(depth, 1, n_r)

    x = jnp.concatenate([x_prompt.reshape(n_prompt, d), x_sample.reshape(bs * ls, d)], axis=0)
    moe_out = None
    new_p, new_s = [], []
    for l in range(depth):
        x, h = _norm(x, moe_out, modg, l, norm1_g[l], n_prompt, final=False)
        z = _proj(h, w_main, l, tn=1024, sigmoid=False, out_dtype=F32)
        gates = _proj(h, w_gate, l, tn=1024, sigmoid=True, out_dtype=BF16)
        groups = [(0, bp, lp, 64, 0.0, None),
                  (n_prompt, bs, ls, ls, float(PAST_LEN),
                   (state_ret[l], state_hgrn[l], state_rwkv[l], state_rwkv_shift[l], state_gdn[l],
                    state_gdn_conv[l]))]
        ys, (st_p, st_s) = _mixers(z, l, groups, params)
        new_p.append(st_p)
        new_s.append(st_s)
        x, h2, logits = _merge(ys, gates, x, modg, l, norm2_g[l], wb_bf, wo_bf, r_w, r_b, n_prompt)
        moe_out = _moe(h2, logits, moe_w_gate, moe_w_up, moe_w_down, l)
    y = _norm(x, moe_out, modg, depth - 1, final_norm_g, n_prompt, final=True)

    def stack(lst, i):
        return jnp.stack([s[i] for s in lst]).astype(F32)
    return ((y[:n_prompt].reshape(bp, lp, d), y[n_prompt:].reshape(bs, ls, d))
            + tuple(stack(new_p, i) for i in range(6)) + tuple(stack(new_s, i) for i in range(6)))
```

```python
import functools
import math

import numpy as np
import jax
import jax.numpy as jnp
from jax import lax
from jax.experimental import pallas as pl
from jax.experimental.pallas import tpu as pltpu

F32 = jnp.float32
BF16 = jnp.bfloat16

D_MODEL = 2048
D_MIX = 512
HEAD_DIM = 128
N_HEADS = 4
RWKV_N = 64
RWKV_HEADS = 8
RWKV_COLS = 1792
GDN_CONV = 4
N_BRANCH = 4
N_GROUPS = 4
EXPERTS_PER_GROUP = 8
N_EXPERTS = 32
TOPK = 2
D_EXPERT = 512
PAST_LEN = 4096
ROPE_BASE = 10000.0
NORM_EPS = 1e-6
GN_EPS = 1e-6
RWKV_GN_EPS = 64e-5
RET_EXP_LO, RET_EXP_HI = 5.0, 12.0

Z_RET, Z_HGRN, Z_RWKV = 0, 2048, 4096
Z_GDN = Z_RWKV + RWKV_COLS
Z_AB = Z_GDN + 4 * D_MIX
Z_GATE = Z_AB + 2 * N_HEADS
N_MAIN = 8192
MIX_W = 2048
GDN_W = 256
GROUP = 32

VMEM_LIMIT = 56 * 1024 * 1024


def _cparams(sem):
    return pltpu.CompilerParams(dimension_semantics=sem, vmem_limit_bytes=VMEM_LIMIT)


def _mm(a, b):
    return jnp.dot(a.astype(BF16), b.astype(BF16), preferred_element_type=F32)


def _mm_nt(a, b):
    return lax.dot_general(a.astype(BF16), b.astype(BF16), (((1,), (1,)), ((), ())),
                           preferred_element_type=F32)


def _mm_tn(a, b):
    return lax.dot_general(a.astype(BF16), b.astype(BF16), (((0,), (0,)), ((), ())),
                           preferred_element_type=F32)


def _silu(x):
    return x * jax.nn.sigmoid(x)


def _softplus(x):
    return jnp.maximum(x, 0.0) + jnp.log1p(jnp.exp(-jnp.abs(x)))


LANES = 128
ROW_CHUNKS = D_MODEL // LANES


def _rows_from_chunks(ref, n_rows):
    return jnp.concatenate([ref[pl.ds(c, n_rows, stride=ROW_CHUNKS), :] for c in range(ROW_CHUNKS)], axis=1)


def _rows_to_chunks(ref, val):
    n_rows = val.shape[0]
    for c in range(ROW_CHUNKS):
        ref[pl.ds(c, n_rows, stride=ROW_CHUNKS), :] = val[:, c * LANES:(c + 1) * LANES]


def _cumsum_rows(x):
    n = x.shape[0]
    row = lax.broadcasted_iota(jnp.int32, x.shape, 0)
    s = 1
    while s < n:
        x = x + jnp.where(row >= s, pltpu.roll(x, s, 0), 0.0)
        s *= 2
    return x


def _row_from_col(col, eye):
    return jnp.sum(jnp.where(eye, col, 0.0), axis=0, keepdims=True)


def _tri_inv(n_mats, c):
    r = lax.broadcasted_iota(jnp.int32, (c, c), 0)
    col = lax.broadcasted_iota(jnp.int32, (c, c), 1)
    eye = jnp.where(r == col, 1.0, 0.0)
    pair = (r >> 1) == (col >> 1)
    xs = [eye - jnp.where(pair, n, 0.0) for n in n_mats]
    m, sh = 2, 1
    while m < c:
        lvl = ((r >> (sh + 1)) == (col >> (sh + 1))) & ((r >> sh) != (col >> sh))
        ts = [_mm(jnp.where(lvl, n, 0.0), x) for n, x in zip(n_mats, xs)]
        xs = [x - _mm(x, t) for x, t in zip(xs, ts)]
        m, sh = m * 2, sh + 1
    return xs


def _ada_kernel(c_ref, w_ref, b_ref, o_ref):
    cm = _silu(c_ref[...])
    o_ref[0] = _mm(cm, w_ref[0]) + b_ref[0]


def _ada(c_all, ada_w, ada_b):
    depth, d, n = ada_w.shape
    rows = c_all.shape[0]
    tn = 1024
    return pl.pallas_call(
        _ada_kernel,
        grid=(depth, n // tn),
        in_specs=[pl.BlockSpec((rows, d), lambda l, j: (0, 0)),
                  pl.BlockSpec((1, d, tn), lambda l, j: (l, 0, j)),
                  pl.BlockSpec((1, 1, tn), lambda l, j: (l, 0, j))],
        out_specs=pl.BlockSpec((1, rows, tn), lambda l, j: (l, 0, j)),
        out_shape=jax.ShapeDtypeStruct((depth, rows, n), F32),
        compiler_params=_cparams(("arbitrary", "arbitrary")),
        name="ada",
    )(c_all, ada_w, ada_b.reshape(depth, 1, n))


def _group_block(tm, n_prompt_tiles):
    gpt = tm // GROUP
    first_sample = 16 // gpt

    def idx(i):
        return jnp.where(i < n_prompt_tiles, 0, first_sample + i - n_prompt_tiles)
    return gpt, idx


def _modulate(y, sc, sh, gpt):
    tm, d = y.shape
    y3 = y.reshape(gpt, GROUP, d)
    return (y3 * (1.0 + sc[:, None, :]) + sh[:, None, :]).reshape(tm, d)


def _rms_rows(x, g):
    return x * lax.rsqrt(jnp.mean(x * x, axis=-1, keepdims=True) + NORM_EPS) * g


def _norm_kernel(*refs, gpt, with_moe, modulated):
    it = iter(refs)
    x_ref = next(it)
    if with_moe:
        m0_ref, m1_ref, gt_ref = next(it), next(it), next(it)
    g_ref = next(it)
    if modulated:
        sc_ref, sh_ref = next(it), next(it)
    x = x_ref[...]
    tm, d = x.shape
    if with_moe:
        moe = _rows_from_chunks(m0_ref, tm) + _rows_from_chunks(m1_ref, tm)
        gt = gt_ref[0]
        x = x + (moe.reshape(gpt, GROUP, d) * gt[:, None, :]).reshape(tm, d)
    y = _rms_rows(x, g_ref[...])
    if modulated:
        xo_ref, h_ref = next(it), next(it)
        xo_ref[...] = x
        h_ref[...] = _modulate(y, sc_ref[0], sh_ref[0], gpt).astype(BF16)
    else:
        y_ref = next(it)
        y_ref[...] = y


def _norm(x, moe_out, modg, l, g, n_prompt, *, final):
    t, d = x.shape
    tm = 256
    gpt, gidx = _group_block(tm, n_prompt // tm)
    with_moe = moe_out is not None
    row = pl.BlockSpec((tm, d), lambda i: (i, 0))

    def mod_spec(col, layer):
        return pl.BlockSpec((1, gpt, d), lambda i: (layer, gidx(i), col))
    args, specs = [x], [row]
    if with_moe:
        lm = l if final else l - 1
        args += [moe_out, moe_out, modg]
        specs += [pl.BlockSpec((tm * ROW_CHUNKS, LANES), lambda i: (i, 0)),
                  pl.BlockSpec((tm * ROW_CHUNKS, LANES), lambda i: (t // tm + i, 0)), mod_spec(5, lm)]
    args.append(g.reshape(1, d))
    specs.append(pl.BlockSpec((1, d), lambda i: (0, 0)))
    if not final:
        args += [modg, modg]
        specs += [mod_spec(1, l), mod_spec(0, l)]
        out_shape = [jax.ShapeDtypeStruct((t, d), F32), jax.ShapeDtypeStruct((t, d), BF16)]
        out_specs = [row, row]
    else:
        out_shape = jax.ShapeDtypeStruct((t, d), F32)
        out_specs = row
    return pl.pallas_call(
        functools.partial(_norm_kernel, gpt=gpt, with_moe=with_moe, modulated=not final),
        grid=(t // tm,), in_specs=specs, out_specs=out_specs, out_shape=out_shape,
        compiler_params=_cparams(("arbitrary",)), name="norm",
    )(*args)


def _proj_kernel(a_ref, w_ref, o_ref, *, sigmoid):
    acc = jnp.dot(a_ref[...], w_ref[0], preferred_element_type=F32)
    if sigmoid:
        acc = jax.nn.sigmoid(acc)
    o_ref[...] = acc.astype(o_ref.dtype)


def _proj(a, w, l, *, tn, sigmoid, out_dtype):
    t, k = a.shape
    n = w.shape[2]
    tm = t // 4
    return pl.pallas_call(
        functools.partial(_proj_kernel, sigmoid=sigmoid),
        grid=(n // tn, t // tm),
        in_specs=[pl.BlockSpec((tm, k), lambda j, i: (i, 0)),
                  pl.BlockSpec((1, k, tn), lambda j, i: (l, 0, j))],
        out_specs=pl.BlockSpec((tm, tn), lambda j, i: (i, j)),
        out_shape=jax.ShapeDtypeStruct((t, n), out_dtype),
        compiler_params=_cparams(("arbitrary", "arbitrary")), name="proj",
    )(a, w)


RET_CHUNK = 256


def _ret_tables(c):
    e = np.linspace(RET_EXP_LO, RET_EXP_HI, N_HEADS)
    lg = np.log1p(-np.exp2(-e))
    t = np.arange(c, dtype=np.float64)
    rel = t[:, None] - t[None, :]
    d_intra = np.where(rel >= 0, np.exp(lg[:, None, None] * np.where(rel >= 0, rel, 0.0)), 0.0)
    d_q = np.exp(lg[:, None] * (t + 1.0))[:, :, None] * np.ones((1, 1, HEAD_DIM))
    d_k = np.exp(lg[:, None] * (c - 1.0 - t))[:, :, None] * np.ones((1, 1, HEAD_DIM))
    d_s = np.exp(lg * c)[:, None, None] * np.ones((1, 8, HEAD_DIM))
    return tuple(jnp.asarray(a, F32) for a in (d_intra, d_q, d_k, d_s))


def _ret_kernel(*refs, c, nc, has_state):
    it = iter(refs)
    z_ref, cos_ref, sin_ref, di_ref, dq_ref, dk_ref, ds_ref = (next(it) for _ in range(7))
    s0_ref = next(it) if has_state else None
    y_ref, so_ref, s_scr = next(it), next(it), next(it)
    n = pl.program_id(1)

    @pl.when(n == 0)
    def _():
        s_scr[...] = s0_ref[0] if has_state else jnp.zeros(s_scr.shape, F32)

    cos, sin = cos_ref[...], sin_ref[...]
    half = HEAD_DIM // 2
    for h in range(N_HEADS):
        lo = h * HEAD_DIM
        q = z_ref[:, lo:lo + HEAD_DIM]
        k = z_ref[:, D_MIX + lo:D_MIX + lo + HEAD_DIM]
        v = z_ref[:, 2 * D_MIX + lo:2 * D_MIX + lo + HEAD_DIM]
        g = z_ref[:, 3 * D_MIX + lo:3 * D_MIX + lo + HEAD_DIM]
        q = q * cos + pltpu.roll(q, half, 1) * sin
        k = (k * cos + pltpu.roll(k, half, 1) * sin) * HEAD_DIM ** -0.5
        s = s_scr[h]
        att = _mm_nt(q, k) * di_ref[h]
        o = _mm(att, v) + _mm(q * dq_ref[h], s)
        s_scr[h] = s * ds_ref[h, 0:1, :] + _mm_tn(k * dk_ref[h], v)
        xc = o - jnp.mean(o, axis=-1, keepdims=True)
        on = xc * lax.rsqrt(jnp.mean(xc * xc, axis=-1, keepdims=True) + GN_EPS)
        y_ref[:, lo:lo + HEAD_DIM] = (on * _silu(g)).astype(y_ref.dtype)

    @pl.when(n == nc - 1)
    def _():
        so_ref[0] = s_scr[...]


def _rope_tables(length, pos0):
    half = HEAD_DIM // 2
    inv = ROPE_BASE ** (-jnp.arange(half, dtype=F32) / half)
    pos = jnp.arange(length, dtype=F32) + pos0
    ang = pos[:, None] * inv[None, :]
    cos, sin = jnp.cos(ang), jnp.sin(ang)
    return jnp.concatenate([cos, cos], axis=1), jnp.concatenate([-sin, sin], axis=1)


def _row_block(row0, c, nc):
    off = row0 // c
    return lambda b, n: off + b * nc + n


def _retention(z, row0, bsz, length, c, pos0, state):
    nc = length // c
    rb = _row_block(row0, c, nc)
    cos, sin = _rope_tables(length, pos0)
    di, dq, dk, ds = _ret_tables(c)
    has_state = state is not None
    const3 = lambda b, n: (0, 0, 0)
    args = [z, cos, sin, di, dq, dk, ds]
    specs = [pl.BlockSpec((c, MIX_W), lambda b, n: (rb(b, n), Z_RET // MIX_W)),
             pl.BlockSpec((c, HEAD_DIM), lambda b, n: (n, 0)),
             pl.BlockSpec((c, HEAD_DIM), lambda b, n: (n, 0)),
             pl.BlockSpec(di.shape, const3), pl.BlockSpec(dq.shape, const3),
             pl.BlockSpec(dk.shape, const3), pl.BlockSpec(ds.shape, const3)]
    sshape = (N_HEADS, HEAD_DIM, HEAD_DIM)
    if has_state:
        args.append(state)
        specs.append(pl.BlockSpec((1,) + sshape, lambda b, n: (b, 0, 0, 0)))
    return pl.pallas_call(
        functools.partial(_ret_kernel, c=c, nc=nc, has_state=has_state),
        grid=(bsz, nc), in_specs=specs,
        out_specs=[pl.BlockSpec((c, D_MIX), lambda b, n: (b * nc + n, 0)),
                   pl.BlockSpec((1,) + sshape, lambda b, n: (b, 0, 0, 0))],
        out_shape=[jax.ShapeDtypeStruct((bsz * length, D_MIX), BF16),
                   jax.ShapeDtypeStruct((bsz,) + sshape, F32)],
        scratch_shapes=[pltpu.VMEM(sshape, F32)],
        compiler_params=_cparams(("arbitrary", "arbitrary")), name="retention",
    )(*args)


SUB = 16


def _gla_head(q, k, v, g, st, c):
    big = _cumsum_rows(g)
    gl = big[c - 1:c, :]
    o = _mm_nt(q * jnp.exp(big), st)
    st_new = st * jnp.exp(gl) + _mm_tn(v, k * jnp.exp(gl - big))

    row = lax.broadcasted_iota(jnp.int32, (c, HEAD_DIM), 0)
    r2 = lax.broadcasted_iota(jnp.int32, (c, c), 0)
    c2 = lax.broadcasted_iota(jnp.int32, (c, c), 1)
    att = jnp.zeros((c, c), F32)
    m, sh = SUB, SUB.bit_length() - 1
    while m < c:
        anchor = jnp.concatenate(
            [jnp.broadcast_to(big[p * 2 * m + m - 1:p * 2 * m + m, :], (2 * m, HEAD_DIM))
             for p in range(c // (2 * m))], axis=0)
        right = ((row >> sh) & 1) == 1
        qt = jnp.where(right, q * jnp.exp(jnp.where(right, big - anchor, 0.0)), 0.0)
        kt = jnp.where(right, 0.0, k * jnp.exp(jnp.where(right, 0.0, anchor - big)))
        att = att + jnp.where((r2 >> (sh + 1)) == (c2 >> (sh + 1)), _mm_nt(qt, kt), 0.0)
        m, sh = m * 2, sh + 1
    o = o + _mm(att, v)

    f = jnp.exp(g)
    rsub = lax.broadcasted_iota(jnp.int32, (SUB, HEAD_DIM), 0)
    diag = []
    for blk in range(c // SUB):
        r0 = blk * SUB
        vb = v[r0:r0 + SUB, :]
        w = jnp.zeros((SUB, HEAD_DIM), F32)
        ob = jnp.zeros((SUB, HEAD_DIM), F32)
        for tl in range(SUB):
            t = r0 + tl
            if tl:
                w = w * f[t:t + 1, :]
            w = jnp.where(rsub == tl, k[t:t + 1, :], w)
            a_col = jnp.sum(w * q[t:t + 1, :], axis=1, keepdims=True)
            o_t = jnp.sum(a_col * vb, axis=0, keepdims=True)
            ob = jnp.where(rsub == tl, o_t, ob)
        diag.append(ob)
    return o + jnp.concatenate(diag, axis=0), st_new


def _hgrn_kernel(*refs, c, nc, has_state):
    it = iter(refs)
    z_ref, llb_ref, l1m_ref, oml_ref, gn_ref = (next(it) for _ in range(5))
    s0_ref = next(it) if has_state else None
    y_ref, so_ref, s_scr = next(it), next(it), next(it)
    n = pl.program_id(1)

    @pl.when(n == 0)
    def _():
        for h in range(N_HEADS):
            s_scr[h] = s0_ref[0, h].T if has_state else jnp.zeros((HEAD_DIM, HEAD_DIM), F32)

    for h in range(N_HEADS):
        lo = h * HEAD_DIM
        q = z_ref[:, lo:lo + HEAD_DIM]
        f = z_ref[:, D_MIX + lo:D_MIX + lo + HEAD_DIM]
        iv = z_ref[:, 2 * D_MIX + lo:2 * D_MIX + lo + HEAD_DIM]
        g = z_ref[:, 3 * D_MIX + lo:3 * D_MIX + lo + HEAD_DIM]
        ls = jnp.minimum(f, 0.0) - jnp.log1p(jnp.exp(-jnp.abs(f)))
        a = llb_ref[:, lo:lo + HEAD_DIM]
        b = l1m_ref[:, lo:lo + HEAD_DIM] + ls
        logf = jnp.maximum(a, b) + jnp.log1p(jnp.exp(-jnp.abs(a - b)))
        kf = oml_ref[:, lo:lo + HEAD_DIM] * jax.nn.sigmoid(-f)
        o, st = _gla_head(q, kf, iv, logf, s_scr[h], c)
        s_scr[h] = st
        y = _rms_rows(o, gn_ref[...]) * _silu(g)
        y_ref[:, lo:lo + HEAD_DIM] = y.astype(y_ref.dtype)

    @pl.when(n == nc - 1)
    def _():
        for h in range(N_HEADS):
            so_ref[0, h] = s_scr[h].T


def _hgrn(z, row0, bsz, length, c, lb, gnorm, state):
    nc = length // c
    rb = _row_block(row0, c, nc)
    has_state = state is not None
    vec = pl.BlockSpec((1, D_MIX), lambda b, n: (0, 0))
    lb = lb.reshape(1, D_MIX)
    args = [z, jnp.log(lb), jnp.log1p(-lb), 1.0 - lb, gnorm.reshape(1, HEAD_DIM)]
    specs = [pl.BlockSpec((c, MIX_W), lambda b, n: (rb(b, n), Z_HGRN // MIX_W)), vec, vec, vec,
             pl.BlockSpec((1, HEAD_DIM), lambda b, n: (0, 0))]
    sshape = (N_HEADS, HEAD_DIM, HEAD_DIM)
    if has_state:
        args.append(state)
        specs.append(pl.BlockSpec((1,) + sshape, lambda b, n: (b, 0, 0, 0)))
    return pl.pallas_call(
        functools.partial(_hgrn_kernel, c=c, nc=nc, has_state=has_state),
        grid=(bsz, nc), in_specs=specs,
        out_specs=[pl.BlockSpec((c, D_MIX), lambda b, n: (b * nc + n, 0)),
                   pl.BlockSpec((1,) + sshape, lambda b, n: (b, 0, 0, 0))],
        out_shape=[jax.ShapeDtypeStruct((bsz * length, D_MIX), BF16),
                   jax.ShapeDtypeStruct((bsz,) + sshape, F32)],
        scratch_shapes=[pltpu.VMEM(sshape, F32)],
        compiler_params=_cparams(("arbitrary", "arbitrary")), name="hgrn2",
    )(*args)


def _rwkv_kernel(*refs, c, nsub, nc, has_state):
    it = iter(refs)
    (z_ref, mu_ref, w0_ref, w2_ref, a0_ref, a2_ref, g2_ref, kk_ref, ka_ref, rk_ref,
     lnw_ref, lnb_ref) = (next(it) for _ in range(12))
    if has_state:
        sh0_ref, s0_ref = next(it), next(it)
    y_ref, so_ref, s_scr, prev_scr = next(it), next(it), next(it), next(it)
    n = pl.program_id(1)
    nh, hd = RWKV_HEADS, RWKV_N

    @pl.when(n == 0)
    def _():
        s_scr[...] = s0_ref[0] if has_state else jnp.zeros(s_scr.shape, F32)
        prev_scr[...] = (jnp.broadcast_to(sh0_ref[0], prev_scr.shape) if has_state
                         else jnp.zeros(prev_scr.shape, F32))

    z = z_ref[...]
    rows = nsub * c
    row = lax.broadcasted_iota(jnp.int32, z.shape, 0)
    prev = jnp.where(row == 0, prev_scr[0:1, :], pltpu.roll(z, 1, 0))
    prev_scr[...] = jnp.broadcast_to(z[rows - 1:rows, :], prev_scr.shape)
    xs = z + (prev - z) * mu_ref[...]
    o3 = 3 * D_MIX
    r, k, v = xs[:, :D_MIX], xs[:, D_MIX:2 * D_MIX], xs[:, 2 * D_MIX:o3]
    w_lo, a_lo, g_lo = xs[:, o3:o3 + 64], xs[:, o3 + 64:o3 + 128], xs[:, o3 + 128:o3 + 256]
    w = -_softplus(-(w0_ref[...] + _mm(jnp.tanh(w_lo), w2_ref[...]))) - 0.5
    ld = -jnp.exp(w)
    a = jax.nn.sigmoid(a0_ref[...] + _mm(a_lo, a2_ref[...]))
    g = _mm(jax.nn.sigmoid(g_lo), g2_ref[...])
    kkv = k * kk_ref[...]
    k = k * (1.0 + (a - 1.0) * ka_ref[...])
    bonus = r * k * rk_ref[...]

    r2 = lax.broadcasted_iota(jnp.int32, (2 * c, 2 * c), 0)
    c2 = lax.broadcasted_iota(jnp.int32, (2 * c, 2 * c), 1)
    tq = jnp.where(r2 >= c, r2 - c, r2)
    ts = jnp.where(c2 >= c, c2 - c, c2)
    keep = (tq > ts) | ((r2 >= c) & (tq == ts))
    heads, subs = range(nh), range(nsub)
    pairs = [(j, h) for j in subs for h in heads]
    sls = [slice(h * hd, (h + 1) * hd) for h in heads]
    lhs, rhs, vs, gam = {}, {}, {}, {}
    for j in subs:
        rs = slice(j * c, (j + 1) * c)
        ldj = ld[rs]
        lc = _cumsum_rows(ldj)
        e_in, e_in_neg, e_ex = jnp.exp(lc), jnp.exp(-lc), jnp.exp(lc - ldj)
        for h in heads:
            sl = sls[h]
            kkh = kkv[rs, sl]
            kap = kkh * lax.rsqrt(jnp.sum(kkh * kkh, axis=-1, keepdims=True) + 1e-6)
            lhs[j, h] = jnp.concatenate([kap * e_ex[:, sl], r[rs, sl] * e_in[:, sl]], axis=0)
            rhs[j, h] = jnp.concatenate([kap * a[rs, sl] * e_in_neg[:, sl], k[rs, sl] * e_in_neg[:, sl]], axis=0)
            vs[j, h] = v[rs, sl]
            gam[j, h] = e_in[c - 1:c, sl]
    pm = {p: jnp.where(keep, _mm_nt(lhs[p], rhs[p]), 0.0) for p in pairs}
    zv = jnp.zeros((c, hd), F32)
    mkv = {p: _mm(pm[p][:c, :], jnp.concatenate([zv, vs[p]], axis=0)) for p in pairs}
    t_inv = dict(zip(pairs, _tri_inv([pm[p][:c, :c] for p in pairs], c)))
    s_cur = [s_scr[h] for h in heads]
    ys = {}
    for j in subs:
        qs = [_mm_nt(lhs[j, h], s_cur[h]) for h in heads]
        us = [-_mm(t_inv[j, h], qs[h][:c, :] + mkv[j, h]) for h in heads]
        uv = [jnp.concatenate([us[h], vs[j, h]], axis=0) for h in heads]
        for h in heads:
            ys[j, h] = qs[h][c:, :] + _mm(pm[j, h][c:, :], uv[h])
        s_cur = [(s_cur[h] + _mm_tn(uv[h], rhs[j, h])) * gam[j, h] for h in heads]
    for h in heads:
        s_scr[h] = s_cur[h]
    for j, h in pairs:
        rs, sl = slice(j * c, (j + 1) * c), sls[h]
        y = ys[j, h]
        yc = y - jnp.mean(y, axis=-1, keepdims=True)
        yn = yc * lax.rsqrt(jnp.mean(yc * yc, axis=-1, keepdims=True) + RWKV_GN_EPS)
        yn = yn * lnw_ref[:, sl] + lnb_ref[:, sl]
        yn = yn + jnp.sum(bonus[rs, sl], axis=-1, keepdims=True) * vs[j, h]
        y_ref[rs, sl] = (yn * g[rs, sl]).astype(y_ref.dtype)

    @pl.when(n == nc - 1)
    def _():
        so_ref[0] = s_scr[...]


def _sub_chunks(length, c, most):
    nc = length // c
    while nc % most:
        most //= 2
    return most


def _rwkv(z, row0, bsz, length, c, p, shift, state):
    nsub = _sub_chunks(length, c, 2)
    cb = c * nsub
    nc = length // cb
    rb = _row_block(row0, cb, nc)
    has_state = state is not None
    pad = MIX_W - RWKV_COLS
    c2 = lambda b, n: (0, 0)
    args = [z, jnp.pad(p["mu"], (0, pad)).reshape(1, MIX_W),
            p["w0"].reshape(1, D_MIX), p["w2"], p["a0"].reshape(1, D_MIX), p["a2"], p["g2"],
            p["k_k"].reshape(1, D_MIX), p["k_a"].reshape(1, D_MIX), p["r_k"].reshape(1, D_MIX),
            p["ln_w"].reshape(1, D_MIX), p["ln_b"].reshape(1, D_MIX)]
    specs = [pl.BlockSpec((cb, MIX_W), lambda b, n: (rb(b, n), Z_RWKV // MIX_W))]
    specs += [pl.BlockSpec(a.shape, c2) for a in args[1:]]
    sshape = (RWKV_HEADS, RWKV_N, RWKV_N)
    if has_state:
        args += [jnp.pad(shift, ((0, 0), (0, pad))).reshape(bsz, 1, MIX_W), state]
        specs += [pl.BlockSpec((1, 1, MIX_W), lambda b, n: (b, 0, 0)),
                  pl.BlockSpec((1,) + sshape, lambda b, n: (b, 0, 0, 0))]
    return pl.pallas_call(
        functools.partial(_rwkv_kernel, c=c, nsub=nsub, nc=nc, has_state=has_state),
        grid=(bsz, nc), in_specs=specs,
        out_specs=[pl.BlockSpec((cb, D_MIX), lambda b, n: (b * nc + n, 0)),
                   pl.BlockSpec((1,) + sshape, lambda b, n: (b, 0, 0, 0))],
        out_shape=[jax.ShapeDtypeStruct((bsz * length, D_MIX), BF16),
                   jax.ShapeDtypeStruct((bsz,) + sshape, F32)],
        scratch_shapes=[pltpu.VMEM(sshape, F32), pltpu.VMEM((8, MIX_W), F32)],
        compiler_params=_cparams(("arbitrary", "arbitrary")), name="rwkv7",
    )(*args)


CONV_PAD = 8


def _gdn_kernel(*refs, c, nsub, nc, has_state):
    it = iter(refs)
    n_win = 4 * D_MIX // GDN_W
    z_refs = [next(it) for _ in range(n_win)]
    ab_ref, cw_ref, alog_ref, dtb_ref, gn_ref = (next(it) for _ in range(5))
    if has_state:
        cv0_ref, s0_ref = next(it), next(it)
    y_ref, so_ref, s_scr, xp_scr = next(it), next(it), next(it), next(it)
    n = pl.program_id(1)
    qkv_w = 3 * D_MIX
    qkv_win = qkv_w // GDN_W
    tail = GDN_CONV - 1
    rows = nsub * c

    @pl.when(n == 0)
    def _():
        s_scr[...] = s0_ref[0] if has_state else jnp.zeros(s_scr.shape, F32)
        xp_scr[0:CONV_PAD, :] = jnp.zeros((CONV_PAD, qkv_w), F32)
        if has_state:
            xp_scr[CONV_PAD - tail:CONV_PAD, :] = cv0_ref[0]

    for j in range(qkv_win):
        xp_scr[CONV_PAD:CONV_PAD + rows, j * GDN_W:(j + 1) * GDN_W] = z_refs[j][...]
    conv = xp_scr[CONV_PAD - tail:CONV_PAD - tail + rows, :] * cw_ref[0:1, :]
    for j in range(1, GDN_CONV):
        conv = conv + xp_scr[CONV_PAD - tail + j:CONV_PAD - tail + j + rows, :] * cw_ref[j:j + 1, :]
    xp_scr[CONV_PAD - tail:CONV_PAD, :] = xp_scr[CONV_PAD + rows - tail:CONV_PAD + rows, :]
    act = _silu(conv)

    ab = ab_ref[...]
    gdec = -jnp.exp(alog_ref[...]) * _softplus(ab + dtb_ref[...])
    beta = jax.nn.sigmoid(ab)

    r2 = lax.broadcasted_iota(jnp.int32, (c, c), 0)
    c2 = lax.broadcasted_iota(jnp.int32, (c, c), 1)
    eye, causal, strict = r2 == c2, r2 >= c2, r2 > c2
    heads, subs = range(N_HEADS), range(nsub)
    pairs = [(j, h) for j in subs for h in heads]
    kq, ks, vs, gcols, bcols, decs = {}, {}, {}, {}, {}, {}
    for j in subs:
        rs = slice(j * c, (j + 1) * c)
        gcum = _cumsum_rows(gdec[rs])
        for h in heads:
            lo = h * HEAD_DIM
            q = act[rs, lo:lo + HEAD_DIM]
            k = act[rs, D_MIX + lo:D_MIX + lo + HEAD_DIM]
            q = q * lax.rsqrt(jnp.sum(q * q, axis=-1, keepdims=True) + 1e-6) * HEAD_DIM ** -0.5
            k = k * lax.rsqrt(jnp.sum(k * k, axis=-1, keepdims=True) + 1e-6)
            gcol = gcum[:, h:h + 1]
            rel = gcol - _row_from_col(gcol, eye)
            kq[j, h] = jnp.concatenate([k, q], axis=0)
            ks[j, h] = k
            vs[j, h] = act[rs, 2 * D_MIX + lo:2 * D_MIX + lo + HEAD_DIM]
            gcols[j, h] = gcol
            bcols[j, h] = beta[rs, N_HEADS + h:N_HEADS + h + 1]
            decs[j, h] = jnp.where(causal, jnp.exp(jnp.where(causal, rel, 0.0)), 0.0)
    kk_qk = {p: _mm_nt(kq[p], ks[p]) for p in pairs}
    t_inv = dict(zip(pairs, _tri_inv(
        [jnp.where(strict, bcols[p] * kk_qk[p][:c] * decs[p], 0.0) for p in pairs], c)))
    s_cur = [s_scr[h] for h in heads]
    os_ = {}
    for j in subs:
        ks_qs = [_mm(kq[j, h], s_cur[h]) for h in heads]
        egs = [jnp.exp(gcols[j, h]) for h in heads]
        us = [_mm(t_inv[j, h], bcols[j, h] * (vs[j, h] - egs[h] * ks_qs[h][:c])) for h in heads]
        for h in heads:
            os_[j, h] = egs[h] * ks_qs[h][c:] + _mm(kk_qk[j, h][c:] * decs[j, h], us[h])
        gls = [gcols[j, h][c - 1:c, :] for h in heads]
        s_cur = [jnp.exp(gls[h]) * s_cur[h] + _mm_tn(ks[j, h] * jnp.exp(gls[h] - gcols[j, h]), us[h])
                 for h in heads]
    for h in heads:
        s_scr[h] = s_cur[h]
    for j, h in pairs:
        rs, lo = slice(j * c, (j + 1) * c), h * HEAD_DIM
        zg = z_refs[qkv_win + lo // GDN_W][rs, lo % GDN_W:lo % GDN_W + HEAD_DIM]
        y = _rms_rows(os_[j, h], gn_ref[...]) * _silu(zg)
        y_ref[rs, lo:lo + HEAD_DIM] = y.astype(y_ref.dtype)

    @pl.when(n == nc - 1)
    def _():
        so_ref[0] = s_scr[...]


def _gdn(z, row0, bsz, length, c, p, conv_state, state):
    nsub = _sub_chunks(length, c, 4)
    cb = c * nsub
    nc = length // cb
    rb = _row_block(row0, cb, nc)
    has_state = state is not None
    c2 = lambda b, n: (0, 0)
    lane_pad = lambda a: jnp.pad(a, (0, HEAD_DIM - a.shape[0])).reshape(1, HEAD_DIM)
    n_win = 4 * D_MIX // GDN_W
    params = [p["conv_w"], lane_pad(p["A_log"]), lane_pad(p["dt_bias"]), p["norm_g"].reshape(1, HEAD_DIM)]
    args = [z] * (n_win + 1) + params
    specs = [pl.BlockSpec((cb, GDN_W), lambda b, n, j=j: (rb(b, n), Z_GDN // GDN_W + j)) for j in range(n_win)]
    specs.append(pl.BlockSpec((cb, HEAD_DIM), lambda b, n: (rb(b, n), Z_AB // HEAD_DIM)))
    specs += [pl.BlockSpec(a.shape, c2) for a in params]
    sshape = (N_HEADS, HEAD_DIM, HEAD_DIM)
    if has_state:
        args += [conv_state, state]
        specs += [pl.BlockSpec((1, GDN_CONV - 1, 3 * D_MIX), lambda b, n: (b, 0, 0)),
                  pl.BlockSpec((1,) + sshape, lambda b, n: (b, 0, 0, 0))]
    return pl.pallas_call(
        functools.partial(_gdn_kernel, c=c, nsub=nsub, nc=nc, has_state=has_state),
        grid=(bsz, nc), in_specs=specs,
        out_specs=[pl.BlockSpec((cb, D_MIX), lambda b, n: (b * nc + n, 0)),
                   pl.BlockSpec((1,) + sshape, lambda b, n: (b, 0, 0, 0))],
        out_shape=[jax.ShapeDtypeStruct((bsz * length, D_MIX), BF16),
                   jax.ShapeDtypeStruct((bsz,) + sshape, F32)],
        scratch_shapes=[pltpu.VMEM(sshape, F32), pltpu.VMEM((CONV_PAD + cb, 3 * D_MIX), F32)],
        compiler_params=_cparams(("arbitrary", "arbitrary")), name="gdn",
    )(*args)


def _merge_kernel(y0_ref, y1_ref, y2_ref, y3_ref, gate_ref, x_ref, gt_ref, sc_ref, sh_ref, g_ref,
                  wb_ref, wo_ref, rw_ref, rb_ref, xo_ref, h_ref, lg_ref, *, gpt):
    tm, d = x_ref.shape
    merged = jnp.zeros((tm, d), F32)
    for nb, y_ref in enumerate((y0_ref, y1_ref, y2_ref, y3_ref)):
        br = jnp.dot(y_ref[...], wb_ref[0, nb], preferred_element_type=F32)
        merged = merged + gate_ref[:, nb * d:(nb + 1) * d].astype(F32) * br
    m = jnp.dot(merged.astype(BF16), wo_ref[0], preferred_element_type=F32)
    gt = gt_ref[0]
    x = x_ref[...] + (m.reshape(gpt, GROUP, d) * gt[:, None, :]).reshape(tm, d)
    xo_ref[...] = x
    h = _modulate(_rms_rows(x, g_ref[...]), sc_ref[0], sh_ref[0], gpt)
    _rows_to_chunks(h_ref, h)
    lg_ref[...] = _mm(h, rw_ref[0]) + rb_ref[0]


def _merge(ys, gates, x, modg, l, g2, wb, wo, rw, rb, n_prompt):
    t, d = x.shape
    tm = 256
    gpt, gidx = _group_block(tm, n_prompt // tm)
    row = lambda w: pl.BlockSpec((tm, w), lambda i: (i, 0))

    def mod_spec(col):
        return pl.BlockSpec((1, gpt, d), lambda i: (l, gidx(i), col))

    def const(shape):
        return pl.BlockSpec(shape, lambda i: (0,) * len(shape), pipeline_mode=pl.Buffered(1))

    def layer(arr):
        return pl.BlockSpec((1,) + arr.shape[1:], lambda i: (l,) + (0,) * (arr.ndim - 1),
                            pipeline_mode=pl.Buffered(1))
    nr = rw.shape[2]
    return pl.pallas_call(
        functools.partial(_merge_kernel, gpt=gpt),
        grid=(t // tm,),
        in_specs=[row(D_MIX)] * 4 + [row(N_BRANCH * d), row(d), mod_spec(2), mod_spec(4), mod_spec(3),
                                     const((1, d)), layer(wb), layer(wo), layer(rw), layer(rb)],
        out_specs=[row(d), pl.BlockSpec((tm * ROW_CHUNKS, LANES), lambda i: (i, 0)), row(nr)],
        out_shape=[jax.ShapeDtypeStruct((t, d), F32), jax.ShapeDtypeStruct((t * ROW_CHUNKS, LANES), F32),
                   jax.ShapeDtypeStruct((t, nr), F32)],
        compiler_params=_cparams(("arbitrary",)), name="merge",
    )(*ys, gates, x, modg, modg, modg, g2.reshape(1, d), wb, wo, rw, rb)


MOE_BLK = 128
MOE_UNROLL = 8


def _moe_kernel(be_ref, tok_ref, slot_ref, nused_ref, h_hbm, w_ref, wg_ref, wu_ref, wd_ref, o_hbm,
                x0, x1, y0, y1, wg_bf, wu_bf, wd_bf, gsem, ssem):
    i = pl.program_id(0)
    nused = nused_ref[0]

    rc = ROW_CHUNKS

    def gather(blk, xb, q, r):
        return pltpu.make_async_copy(h_hbm.at[pl.ds(tok_ref[blk * MOE_BLK + r] * rc, rc)],
                                     xb.at[pl.ds(r * rc, rc)], gsem.at[q])

    def scatter(blk, yb, q, r):
        return pltpu.make_async_copy(yb.at[pl.ds(r * rc, rc)],
                                     o_hbm.at[pl.ds(slot_ref[(blk + 1) * MOE_BLK + r] * rc, rc)], ssem.at[q])

    def for_rows(fn):
        def group(j, carry):
            for u in range(MOE_UNROLL):
                fn(j * MOE_UNROLL + u)
            return carry
        lax.fori_loop(0, MOE_BLK // MOE_UNROLL, group, 0)

    def step(q, xa, ya, xb, yb):
        @pl.when(i == 0)
        def _():
            yb[...] = jnp.zeros(yb.shape, F32)
            n_real = o_hbm.shape[0] - 2 * MOE_BLK * rc
            init = pltpu.make_async_copy(yb, o_hbm.at[pl.ds(n_real, MOE_BLK * rc)], ssem.at[q])
            init.start()
            init.wait()
            for_rows(lambda r: gather(i, xa, q, r).start())

        @pl.when(i > 0)
        def _():
            for_rows(lambda r: scatter(i - 2, ya, q, r).wait())
        for_rows(lambda r: gather(i, xa, q, r).wait())

        @pl.when((i == 0) | (be_ref[i] != be_ref[jnp.maximum(i - 1, 0)]))
        def _():
            wg_bf[...] = wg_ref[0, 0].astype(BF16)
            wu_bf[...] = wu_ref[0, 0].astype(BF16)
            wd_bf[...] = wd_ref[0, 0].astype(BF16)

        for r in range(MOE_BLK):
            gather(i + 1, xb, 1 - q, r).start()
        for r in range(MOE_BLK):
            scatter(i - 1, yb, 1 - q, r).start()
        x = _rows_from_chunks(xa, MOE_BLK).astype(BF16)
        hid = _silu(jnp.dot(x, wg_bf[...], preferred_element_type=F32)) * jnp.dot(
            x, wu_bf[...], preferred_element_type=F32)
        _rows_to_chunks(ya, jnp.dot(hid.astype(BF16), wd_bf[...], preferred_element_type=F32) * w_ref[...])

        @pl.when(i == nused - 1)
        def _():
            for_rows(lambda r: scatter(i, ya, q, r).start())
            for_rows(lambda r: scatter(i - 1, yb, 1 - q, r).wait())
            for_rows(lambda r: scatter(i, ya, q, r).wait())
            for_rows(lambda r: gather(i + 1, xb, 1 - q, r).wait())

    @pl.when((i < nused) & (i % 2 == 0))
    def _():
        step(0, x0, y0, x1, y1)

    @pl.when((i < nused) & (i % 2 == 1))
    def _():
        step(1, x1, y1, x0, y0)


def _route(logits, n_tok):
    lt = logits[:, :N_GROUPS + N_EXPERTS].T
    lg = lt[:N_GROUPS]
    gsel = jnp.argmax(lg, axis=0).astype(jnp.int32)
    gw = jnp.max(jax.nn.softmax(lg, axis=0), axis=0)
    le_all = lt[N_GROUPS:].reshape(N_GROUPS, EXPERTS_PER_GROUP, n_tok)
    le = sum(jnp.where(gsel[None, :] == g, le_all[g], 0.0) for g in range(N_GROUPS))
    pe = jax.nn.softmax(le, axis=0)
    rows8 = lax.broadcasted_iota(jnp.int32, pe.shape, 0)
    i1 = jnp.argmax(pe, axis=0).astype(jnp.int32)
    v1 = jnp.max(pe, axis=0)
    pe_rest = jnp.where(rows8 == i1[None, :], -jnp.inf, pe)
    i2 = jnp.argmax(pe_rest, axis=0).astype(jnp.int32)
    v2 = jnp.max(pe_rest, axis=0)
    top_v = jnp.stack([v1, v2], axis=1)
    top_i = jnp.stack([i1, i2], axis=1)
    wts = top_v / jnp.sum(top_v, axis=-1, keepdims=True) * gw[:, None]
    eid = (gsel[:, None] * EXPERTS_PER_GROUP + top_i).reshape(-1).astype(jnp.int32)
    a = n_tok * TOPK
    order = jnp.argsort(eid).astype(jnp.int32)
    onehot = (eid[None, :] == jnp.arange(N_EXPERTS, dtype=jnp.int32)[:, None]).astype(BF16)
    counts = jnp.dot(onehot, jnp.ones((a, 1), BF16), preferred_element_type=F32)[:, 0].astype(jnp.int32)
    padded = (counts + MOE_BLK - 1) // MOE_BLK * MOE_BLK
    pend = jnp.cumsum(padded)
    pstart = pend - padded
    cstart = jnp.cumsum(counts) - counts
    n_blocks = -(-a // MOE_BLK) + N_EXPERTS
    rows = n_blocks * MOE_BLK
    blk_row0 = jnp.arange(n_blocks, dtype=jnp.int32) * MOE_BLK
    block_e = jnp.minimum(jnp.sum((pend[None, :] <= blk_row0[:, None]).astype(jnp.int32), axis=1),
                          N_EXPERTS - 1).astype(jnp.int32)
    nused = (pend[-1:] // MOE_BLK).astype(jnp.int32)
    e_row = jnp.repeat(block_e, MOE_BLK)
    row_id = jnp.arange(rows, dtype=jnp.int32)
    j_row = row_id - pstart[e_row]
    valid = j_row < counts[e_row]
    asg = order[jnp.clip(cstart[e_row] + j_row, 0, a - 1)]
    tok = jnp.where(valid, asg // TOPK, 0)
    dummy = a + ((row_id // MOE_BLK) % 2) * MOE_BLK + row_id % MOE_BLK
    slot = jnp.where(valid, (asg % TOPK) * n_tok + asg // TOPK, dummy)
    wrow = jnp.where(valid, wts.reshape(-1)[asg], 0.0)
    tok = jnp.concatenate([tok, jnp.zeros((MOE_BLK,), jnp.int32)])
    slot = jnp.concatenate([a + MOE_BLK + jnp.arange(MOE_BLK, dtype=jnp.int32), slot])
    return block_e, tok, slot, nused, wrow.reshape(rows, 1), n_blocks


def _moe(h, logits, wg, wu, wd, l):
    t, d = h.shape[0] // ROW_CHUNKS, D_MODEL
    block_e, tok, slot, nused, wrow, n_blocks = _route(logits, t)
    grid_spec = pltpu.PrefetchScalarGridSpec(
        num_scalar_prefetch=4, grid=(n_blocks,),
        in_specs=[pl.BlockSpec(memory_space=pl.ANY),
                  pl.BlockSpec((MOE_BLK, 1), lambda i, be, *_: (i, 0)),
                  pl.BlockSpec((1, 1, d, D_EXPERT), lambda i, be, *_: (l, be[i], 0, 0)),
                  pl.BlockSpec((1, 1, d, D_EXPERT), lambda i, be, *_: (l, be[i], 0, 0)),
                  pl.BlockSpec((1, 1, D_EXPERT, d), lambda i, be, *_: (l, be[i], 0, 0))],
        out_specs=pl.BlockSpec(memory_space=pl.ANY),
        scratch_shapes=[pltpu.VMEM((MOE_BLK * ROW_CHUNKS, LANES), F32)] * 4 + [
                        pltpu.VMEM((d, D_EXPERT), BF16), pltpu.VMEM((d, D_EXPERT), BF16),
                        pltpu.VMEM((D_EXPERT, d), BF16),
                        pltpu.SemaphoreType.DMA((2,)), pltpu.SemaphoreType.DMA((2,))])
    out = pl.pallas_call(
        _moe_kernel, grid_spec=grid_spec,
        out_shape=jax.ShapeDtypeStruct(((t * TOPK + 2 * MOE_BLK) * ROW_CHUNKS, LANES), F32),
        compiler_params=_cparams(("arbitrary",)), name="moe",
    )(block_e, tok, slot, nused, h, wrow, wg, wu, wd)
    return out


def _pack_w_in(w_in):
    cut = -(-Z_GATE // LANES) * LANES
    w_main = jnp.pad(w_in[:, :, :cut].astype(BF16), ((0, 0), (0, 0), (0, N_MAIN - cut)))
    return w_main, w_in[:, :, Z_GATE:].astype(BF16)


def _mixers(z, l, groups, params):
    outs = [[] for _ in range(N_BRANCH)]
    states = []
    for (row0, bsz, length, c, pos0, st) in groups:
        s_ret, s_hgrn, s_rwkv, s_shift, s_gdn, s_conv = st if st is not None else (None,) * 6
        y_a, n_ret = _retention(z, row0, bsz, length, min(length, RET_CHUNK), pos0, s_ret)
        y_b, n_hgrn = _hgrn(z, row0, bsz, length, c, params["hgrn_lb"][l], params["hgrn_norm_g"][l], s_hgrn)
        y_c, n_rwkv = _rwkv(z, row0, bsz, length, c, {k: v[l] for k, v in params["rwkv"].items()},
                            s_shift, s_rwkv)
        y_d, n_gdn = _gdn(z, row0, bsz, length, c, {k: v[l] for k, v in params["gdn"].items()},
                          s_conv, s_gdn)
        for lst, y in zip(outs, (y_a, y_b, y_c, y_d)):
            lst.append(y)
        def seq_row(j, col0, width):
            return lax.slice(z, (row0 + j, col0), (row0 + (bsz - 1) * length + j + 1, col0 + width),
                             (length, 1))
        n_shift = seq_row(length - 1, Z_RWKV, RWKV_COLS)
        n_conv = jnp.stack([seq_row(length - (GDN_CONV - 1) + j, Z_GDN, 3 * D_MIX)
                            for j in range(GDN_CONV - 1)], axis=1)
        states.append((n_ret, n_hgrn, n_rwkv, n_shift, n_gdn, n_conv))
    return [jnp.concatenate(lst, axis=0) for lst in outs], states


def kernel(x_prompt, x_sample, c_prompt, c_sample, state_ret, state_hgrn, state_rwkv, state_rwkv_shift,
           state_gdn, state_gdn_conv, ada_w, ada_b, norm1_g, norm2_g, w_in, hgrn_lb_logits, hgrn_norm_g,
           rwkv_mu, rwkv_w0, rwkv_w2, rwkv_a0, rwkv_a2, rwkv_g2, rwkv_k_k, rwkv_k_a, rwkv_r_k, rwkv_ln_w,
           rwkv_ln_b, gdn_conv_w, gdn_A_log, gdn_dt_bias, gdn_norm_g, w_branch, w_out, router_g, router_g_b,
           router_e, router_e_b, moe_w_gate, moe_w_up, moe_w_down, final_norm_g):
    depth = ada_w.shape[0]
    bp, lp, d = x_prompt.shape
    bs, ls, _ = x_sample.shape
    n_prompt = bp * lp
    assert bp == 1 and ls == GROUP and bs == 16

    lb_cum = jnp.cumsum(jax.nn.softmax(hgrn_lb_logits.astype(F32), axis=0), axis=0)
    params = dict(
        hgrn_lb=lb_cum - lb_cum[:1], hgrn_norm_g=hgrn_norm_g,
        rwkv=dict(mu=rwkv_mu, w0=rwkv_w0, w2=rwkv_w2, a0=rwkv_a0, a2=rwkv_a2, g2=rwkv_g2, k_k=rwkv_k_k,
                  k_a=rwkv_k_a, r_k=rwkv_r_k.reshape(depth, D_MIX), ln_w=rwkv_ln_w, ln_b=rwkv_ln_b),
        gdn=dict(conv_w=gdn_conv_w, A_log=gdn_A_log, dt_bias=gdn_dt_bias, norm_g=gdn_norm_g))

    c_all = jnp.concatenate([c_prompt, c_sample, jnp.zeros((24 - bp - bs, d), F32)], axis=0)
    mod = _ada(c_all, ada_w, ada_b)
    modg = jnp.concatenate([jnp.broadcast_to(mod[:, :1], (depth, 16, 6 * d)), mod[:, 1:1 + bs]], axis=1)

    w_main, w_gate = _pack_w_in(w_in)
    wb_bf, wo_bf = w_branch.astype(BF16), w_out.astype(BF16)
    n_r = 128
    r_w = jnp.concatenate([router_g, router_e, jnp.zeros((depth, d, n_r - N_GROUPS - N_EXPERTS), F32)], axis=2)
    r_b = jnp.concatenate([router_g_b, router_e_b, jnp.zeros((depth, n_r - N_GROUPS - N_EXPERTS), F32)],
                          axis=1).reshape(depth, 1, n_r)

    x = jnp.concatenate([x_prompt.reshape(n_prompt, d), x_sample.reshape(bs * ls, d)], axis=0)
    moe_out = None
    new_p, new_s = [], []
    for l in range(depth):
        x, h = _norm(x, moe_out, modg, l, norm1_g[l], n_prompt, final=False)
        z = _proj(h, w_main, l, tn=1024, sigmoid=False, out_dtype=F32)
        gates = _proj(h, w_gate, l, tn=1024, sigmoid=True, out_dtype=BF16)
        groups = [(0, bp, lp, 64, 0.0, None),
                  (n_prompt, bs, ls, ls, float(PAST_LEN),
                   (state_ret[l], state_hgrn[l], state_rwkv[l], state_rwkv_shift[l], state_gdn[l],
                    state_gdn_conv[l]))]
        ys, (st_p, st_s) = _mixers(z, l, groups, params)
        new_p.append(st_p)
        new_s.append(st_s)
        x, h2, logits = _merge(ys, gates, x, modg, l, norm2_g[l], wb_bf, wo_bf, r_w, r_b, n_prompt)
        moe_out = _moe(h2, logits, moe_w_gate, moe_w_up, moe_w_down, l)
    y = _norm(x, moe_out, modg, depth - 1, final_norm_g, n_prompt, final=True)

    def stack(lst, i):
        return jnp.stack([s[i] for s in lst]).astype(F32)
    return ((y[:n_prompt].reshape(bp, lp, d), y[n_prompt:].reshape(bs, ls, d))
            + tuple(stack(new_p, i) for i in range(6)) + tuple(stack(new_s, i) for i in range(6)))
```

```python
import functools
import math

import numpy as np
import jax
import jax.numpy as jnp
from jax import lax
from jax.experimental import pallas as pl
from jax.experimental.pallas import tpu as pltpu

F32 = jnp.float32
BF16 = jnp.bfloat16

D_MODEL = 2048
D_MIX = 512
HEAD_DIM = 128
N_HEADS = 4
RWKV_N = 64
RWKV_HEADS = 8
RWKV_COLS = 1792
GDN_CONV = 4
N_BRANCH = 4
N_GROUPS = 4
EXPERTS_PER_GROUP = 8
N_EXPERTS = 32
TOPK = 2
D_EXPERT = 512
PAST_LEN = 4096
ROPE_BASE = 10000.0
NORM_EPS = 1e-6
GN_EPS = 1e-6
RWKV_GN_EPS = 64e-5
RET_EXP_LO, RET_EXP_HI = 5.0, 12.0

Z_RET, Z_HGRN, Z_RWKV = 0, 2048, 4096
Z_GDN = Z_RWKV + RWKV_COLS
Z_AB = Z_GDN + 4 * D_MIX
Z_GATE = Z_AB + 2 * N_HEADS
N_MAIN = 8192
MIX_W = 2048
GDN_W = 256
GROUP = 32

VMEM_LIMIT = 56 * 1024 * 1024


def _cparams(sem):
    return pltpu.CompilerParams(dimension_semantics=sem, vmem_limit_bytes=VMEM_LIMIT)


def _mm(a, b):
    return jnp.dot(a.astype(BF16), b.astype(BF16), preferred_element_type=F32)


def _mm_nt(a, b):
    return lax.dot_general(a.astype(BF16), b.astype(BF16), (((1,), (1,)), ((), ())),
                           preferred_element_type=F32)


def _mm_tn(a, b):
    return lax.dot_general(a.astype(BF16), b.astype(BF16), (((0,), (0,)), ((), ())),
                           preferred_element_type=F32)


def _silu(x):
    return x * jax.nn.sigmoid(x)


def _softplus(x):
    return jnp.maximum(x, 0.0) + jnp.log1p(jnp.exp(-jnp.abs(x)))


LANES = 128
ROW_CHUNKS = D_MODEL // LANES


def _rows_from_chunks(ref, n_rows):
    return jnp.concatenate([ref[pl.ds(c, n_rows, stride=ROW_CHUNKS), :] for c in range(ROW_CHUNKS)], axis=1)


def _rows_to_chunks(ref, val):
    n_rows = val.shape[0]
    for c in range(ROW_CHUNKS):
        ref[pl.ds(c, n_rows, stride=ROW_CHUNKS), :] = val[:, c * LANES:(c + 1) * LANES]


def _cumsum_rows(x):
    n = x.shape[0]
    row = lax.broadcasted_iota(jnp.int32, x.shape, 0)
    s = 1
    while s < n:
        x = x + jnp.where(row >= s, pltpu.roll(x, s, 0), 0.0)
        s *= 2
    return x


def _row_from_col(col, eye):
    return jnp.sum(jnp.where(eye, col, 0.0), axis=0, keepdims=True)


def _tri_inv(n_mats, c):
    r = lax.broadcasted_iota(jnp.int32, (c, c), 0)
    col = lax.broadcasted_iota(jnp.int32, (c, c), 1)
    eye = jnp.where(r == col, 1.0, 0.0)
    pair = (r >> 1) == (col >> 1)
    xs = [eye - jnp.where(pair, n, 0.0) for n in n_mats]
    m, sh = 2, 1
    while m < c:
        lvl = ((r >> (sh + 1)) == (col >> (sh + 1))) & ((r >> sh) != (col >> sh))
        ts = [_mm(jnp.where(lvl, n, 0.0), x) for n, x in zip(n_mats, xs)]
        xs = [x - _mm(x, t) for x, t in zip(xs, ts)]
        m, sh = m * 2, sh + 1
    return xs


def _ada_kernel(c_ref, w_ref, b_ref, o_ref):
    cm = _silu(c_ref[...])
    o_ref[0] = _mm(cm, w_ref[0]) + b_ref[0]


def _ada(c_all, ada_w, ada_b):
    depth, d, n = ada_w.shape
    rows = c_all.shape[0]
    tn = 1024
    return pl.pallas_call(
        _ada_kernel,
        grid=(depth, n // tn),
        in_specs=[pl.BlockSpec((rows, d), lambda l, j: (0, 0)),
                  pl.BlockSpec((1, d, tn), lambda l, j: (l, 0, j)),
                  pl.BlockSpec((1, 1, tn), lambda l, j: (l, 0, j))],
        out_specs=pl.BlockSpec((1, rows, tn), lambda l, j: (l, 0, j)),
        out_shape=jax.ShapeDtypeStruct((depth, rows, n), F32),
        compiler_params=_cparams(("arbitrary", "arbitrary")),
        name="ada",
    )(c_all, ada_w, ada_b.reshape(depth, 1, n))


def _group_block(tm, n_prompt_tiles):
    gpt = tm // GROUP
    first_sample = 16 // gpt

    def idx(i):
        return jnp.where(i < n_prompt_tiles, 0, first_sample + i - n_prompt_tiles)
    return gpt, idx


def _modulate(y, sc, sh, gpt):
    tm, d = y.shape
    y3 = y.reshape(gpt, GROUP, d)
    return (y3 * (1.0 + sc[:, None, :]) + sh[:, None, :]).reshape(tm, d)


def _rms_rows(x, g):
    return x * lax.rsqrt(jnp.mean(x * x, axis=-1, keepdims=True) + NORM_EPS) * g


def _norm_kernel(*refs, gpt, with_moe, modulated):
    it = iter(refs)
    x_ref = next(it)
    if with_moe:
        m0_ref, m1_ref, gt_ref = next(it), next(it), next(it)
    g_ref = next(it)
    if modulated:
        sc_ref, sh_ref = next(it), next(it)
    x = x_ref[...]
    tm, d = x.shape
    if with_moe:
        moe = _rows_from_chunks(m0_ref, tm) + _rows_from_chunks(m1_ref, tm)
        gt = gt_ref[0]
        x = x + (moe.reshape(gpt, GROUP, d) * gt[:, None, :]).reshape(tm, d)
    y = _rms_rows(x, g_ref[...])
    if modulated:
        xo_ref, h_ref = next(it), next(it)
        xo_ref[...] = x
        h_ref[...] = _modulate(y, sc_ref[0], sh_ref[0], gpt).astype(BF16)
    else:
        y_ref = next(it)
        y_ref[...] = y


def _norm(x, moe_out, modg, l, g, n_prompt, *, final):
    t, d = x.shape
    tm = 256
    gpt, gidx = _group_block(tm, n_prompt // tm)
    with_moe = moe_out is not None
    row = pl.BlockSpec((tm, d), lambda i: (i, 0))

    def mod_spec(col, layer):
        return pl.BlockSpec((1, gpt, d), lambda i: (layer, gidx(i), col))
    args, specs = [x], [row]
    if with_moe:
        lm = l if final else l - 1
        args += [moe_out, moe_out, modg]
        specs += [pl.BlockSpec((tm * ROW_CHUNKS, LANES), lambda i: (i, 0)),
                  pl.BlockSpec((tm * ROW_CHUNKS, LANES), lambda i: (t // tm + i, 0)), mod_spec(5, lm)]
    args.append(g.reshape(1, d))
    specs.append(pl.BlockSpec((1, d), lambda i: (0, 0)))
    if not final:
        args += [modg, modg]
        specs += [mod_spec(1, l), mod_spec(0, l)]
        out_shape = [jax.ShapeDtypeStruct((t, d), F32), jax.ShapeDtypeStruct((t, d), BF16)]
        out_specs = [row, row]
    else:
        out_shape = jax.ShapeDtypeStruct((t, d), F32)
        out_specs = row
    return pl.pallas_call(
        functools.partial(_norm_kernel, gpt=gpt, with_moe=with_moe, modulated=not final),
        grid=(t // tm,), in_specs=specs, out_specs=out_specs, out_shape=out_shape,
        compiler_params=_cparams(("arbitrary",)), name="norm",
    )(*args)


def _proj_kernel(a_ref, w_ref, o_ref, *, sigmoid):
    acc = jnp.dot(a_ref[...], w_ref[0], preferred_element_type=F32)
    if sigmoid:
        acc = jax.nn.sigmoid(acc)
    o_ref[...] = acc.astype(o_ref.dtype)


def _proj(a, w, l, *, tn, sigmoid, out_dtype):
    t, k = a.shape
    n = w.shape[2]
    tm = t // 4
    return pl.pallas_call(
        functools.partial(_proj_kernel, sigmoid=sigmoid),
        grid=(n // tn, t // tm),
        in_specs=[pl.BlockSpec((tm, k), lambda j, i: (i, 0)),
                  pl.BlockSpec((1, k, tn), lambda j, i: (l, 0, j))],
        out_specs=pl.BlockSpec((tm, tn), lambda j, i: (i, j)),
        out_shape=jax.ShapeDtypeStruct((t, n), out_dtype),
        compiler_params=_cparams(("arbitrary", "arbitrary")), name="proj",
    )(a, w)


RET_CHUNK = 256


def _ret_tables(c):
    e = np.linspace(RET_EXP_LO, RET_EXP_HI, N_HEADS)
    lg = np.log1p(-np.exp2(-e))
    t = np.arange(c, dtype=np.float64)
    rel = t[:, None] - t[None, :]
    d_intra = np.where(rel >= 0, np.exp(lg[:, None, None] * np.where(rel >= 0, rel, 0.0)), 0.0)
    d_q = np.exp(lg[:, None] * (t + 1.0))[:, :, None] * np.ones((1, 1, HEAD_DIM))
    d_k = np.exp(lg[:, None] * (c - 1.0 - t))[:, :, None] * np.ones((1, 1, HEAD_DIM))
    d_s = np.exp(lg * c)[:, None, None] * np.ones((1, 8, HEAD_DIM))
    return tuple(jnp.asarray(a, F32) for a in (d_intra, d_q, d_k, d_s))


def _ret_kernel(*refs, c, nc, has_state):
    it = iter(refs)
    z_ref, cos_ref, sin_ref, di_ref, dq_ref, dk_ref, ds_ref = (next(it) for _ in range(7))
    s0_ref = next(it) if has_state else None
    y_ref, so_ref, s_scr = next(it), next(it), next(it)
    n = pl.program_id(1)

    @pl.when(n == 0)
    def _():
        s_scr[...] = s0_ref[0] if has_state else jnp.zeros(s_scr.shape, F32)

    cos, sin = cos_ref[...], sin_ref[...]
    half = HEAD_DIM // 2
    for h in range(N_HEADS):
        lo = h * HEAD_DIM
        q = z_ref[:, lo:lo + HEAD_DIM]
        k = z_ref[:, D_MIX + lo:D_MIX + lo + HEAD_DIM]
        v = z_ref[:, 2 * D_MIX + lo:2 * D_MIX + lo + HEAD_DIM]
        g = z_ref[:, 3 * D_MIX + lo:3 * D_MIX + lo + HEAD_DIM]
        q = q * cos + pltpu.roll(q, half, 1) * sin
        k = (k * cos + pltpu.roll(k, half, 1) * sin) * HEAD_DIM ** -0.5
        s = s_scr[h]
        att = _mm_nt(q, k) * di_ref[h]
        o = _mm(att, v) + _mm(q * dq_ref[h], s)
        s_scr[h] = s * ds_ref[h, 0:1, :] + _mm_tn(k * dk_ref[h], v)
        xc = o - jnp.mean(o, axis=-1, keepdims=True)
        on = xc * lax.rsqrt(jnp.mean(xc * xc, axis=-1, keepdims=True) + GN_EPS)
        y_ref[:, lo:lo + HEAD_DIM] = (on * _silu(g)).astype(y_ref.dtype)

    @pl.when(n == nc - 1)
    def _():
        so_ref[0] = s_scr[...]


def _rope_tables(length, pos0):
    half = HEAD_DIM // 2
    inv = ROPE_BASE ** (-jnp.arange(half, dtype=F32) / half)
    pos = jnp.arange(length, dtype=F32) + pos0
    ang = pos[:, None] * inv[None, :]
    cos, sin = jnp.cos(ang), jnp.sin(ang)
    return jnp.concatenate([cos, cos], axis=1), jnp.concatenate([-sin, sin], axis=1)


def _row_block(row0, c, nc):
    off = row0 // c
    return lambda b, n: off + b * nc + n


def _retention(z, row0, bsz, length, c, pos0, state):
    nc = length // c
    rb = _row_block(row0, c, nc)
    cos, sin = _rope_tables(length, pos0)
    di, dq, dk, ds = _ret_tables(c)
    has_state = state is not None
    const3 = lambda b, n: (0, 0, 0)
    args = [z, cos, sin, di, dq, dk, ds]
    specs = [pl.BlockSpec((c, MIX_W), lambda b, n: (rb(b, n), Z_RET // MIX_W)),
             pl.BlockSpec((c, HEAD_DIM), lambda b, n: (n, 0)),
             pl.BlockSpec((c, HEAD_DIM), lambda b, n: (n, 0)),
             pl.BlockSpec(di.shape, const3), pl.BlockSpec(dq.shape, const3),
             pl.BlockSpec(dk.shape, const3), pl.BlockSpec(ds.shape, const3)]
    sshape = (N_HEADS, HEAD_DIM, HEAD_DIM)
    if has_state:
        args.append(state)
        specs.append(pl.BlockSpec((1,) + sshape, lambda b, n: (b, 0, 0, 0)))
    return pl.pallas_call(
        functools.partial(_ret_kernel, c=c, nc=nc, has_state=has_state),
        grid=(bsz, nc), in_specs=specs,
        out_specs=[pl.BlockSpec((c, D_MIX), lambda b, n: (b * nc + n, 0)),
                   pl.BlockSpec((1,) + sshape, lambda b, n: (b, 0, 0, 0))],
        out_shape=[jax.ShapeDtypeStruct((bsz * length, D_MIX), BF16),
                   jax.ShapeDtypeStruct((bsz,) + sshape, F32)],
        scratch_shapes=[pltpu.VMEM(sshape, F32)],
        compiler_params=_cparams(("arbitrary", "arbitrary")), name="retention",
    )(*args)


SUB = 16


def _gla_head(q, k, v, g, st, c):
    big = _cumsum_rows(g)
    gl = big[c - 1:c, :]
    o = _mm_nt(q * jnp.exp(big), st)
    st_new = st * jnp.exp(gl) + _mm_tn(v, k * jnp.exp(gl - big))

    row = lax.broadcasted_iota(jnp.int32, (c, HEAD_DIM), 0)
    r2 = lax.broadcasted_iota(jnp.int32, (c, c), 0)
    c2 = lax.broadcasted_iota(jnp.int32, (c, c), 1)
    att = jnp.zeros((c, c), F32)
    m, sh = SUB, SUB.bit_length() - 1
    while m < c:
        anchor = jnp.concatenate(
            [jnp.broadcast_to(big[p * 2 * m + m - 1:p * 2 * m + m, :], (2 * m, HEAD_DIM))
             for p in range(c // (2 * m))], axis=0)
        right = ((row >> sh) & 1) == 1
        qt = jnp.where(right, q * jnp.exp(jnp.where(right, big - anchor, 0.0)), 0.0)
        kt = jnp.where(right, 0.0, k * jnp.exp(jnp.where(right, 0.0, anchor - big)))
        att = att + jnp.where((r2 >> (sh + 1)) == (c2 >> (sh + 1)), _mm_nt(qt, kt), 0.0)
        m, sh = m * 2, sh + 1
    o = o + _mm(att, v)

    f = jnp.exp(g)
    rsub = lax.broadcasted_iota(jnp.int32, (SUB, HEAD_DIM), 0)
    diag = []
    for blk in range(c // SUB):
        r0 = blk * SUB
        vb = v[r0:r0 + SUB, :]
        w = jnp.zeros((SUB, HEAD_DIM), F32)
        ob = jnp.zeros((SUB, HEAD_DIM), F32)
        for tl in range(SUB):
            t = r0 + tl
            if tl:
                w = w * f[t:t + 1, :]
            w = jnp.where(rsub == tl, k[t:t + 1, :], w)
            a_col = jnp.sum(w * q[t:t + 1, :], axis=1, keepdims=True)
            o_t = jnp.sum(a_col * vb, axis=0, keepdims=True)
            ob = jnp.where(rsub == tl, o_t, ob)
        diag.append(ob)
    return o + jnp.concatenate(diag, axis=0), st_new


def _hgrn_kernel(*refs, c, nc, has_state):
    it = iter(refs)
    z_ref, llb_ref, l1m_ref, oml_ref, gn_ref = (next(it) for _ in range(5))
    s0_ref = next(it) if has_state else None
    y_ref, so_ref, s_scr = next(it), next(it), next(it)
    n = pl.program_id(1)

    @pl.when(n == 0)
    def _():
        for h in range(N_HEADS):
            s_scr[h] = s0_ref[0, h].T if has_state else jnp.zeros((HEAD_DIM, HEAD_DIM), F32)

    for h in range(N_HEADS):
        lo = h * HEAD_DIM
        q = z_ref[:, lo:lo + HEAD_DIM]
        f = z_ref[:, D_MIX + lo:D_MIX + lo + HEAD_DIM]
        iv = z_ref[:, 2 * D_MIX + lo:2 * D_MIX + lo + HEAD_DIM]
        g = z_ref[:, 3 * D_MIX + lo:3 * D_MIX + lo + HEAD_DIM]
        ls = jnp.minimum(f, 0.0) - jnp.log1p(jnp.exp(-jnp.abs(f)))
        a = llb_ref[:, lo:lo + HEAD_DIM]
        b = l1m_ref[:, lo:lo + HEAD_DIM] + ls
        logf = jnp.maximum(a, b) + jnp.log1p(jnp.exp(-jnp.abs(a - b)))
        kf = oml_ref[:, lo:lo + HEAD_DIM] * jax.nn.sigmoid(-f)
        o, st = _gla_head(q, kf, iv, logf, s_scr[h], c)
        s_scr[h] = st
        y = _rms_rows(o, gn_ref[...]) * _silu(g)
        y_ref[:, lo:lo + HEAD_DIM] = y.astype(y_ref.dtype)

    @pl.when(n == nc - 1)
    def _():
        for h in range(N_HEADS):
            so_ref[0, h] = s_scr[h].T


def _hgrn(z, row0, bsz, length, c, lb, gnorm, state):
    nc = length // c
    rb = _row_block(row0, c, nc)
    has_state = state is not None
    vec = pl.BlockSpec((1, D_MIX), lambda b, n: (0, 0))
    lb = lb.reshape(1, D_MIX)
    args = [z, jnp.log(lb), jnp.log1p(-lb), 1.0 - lb, gnorm.reshape(1, HEAD_DIM)]
    specs = [pl.BlockSpec((c, MIX_W), lambda b, n: (rb(b, n), Z_HGRN // MIX_W)), vec, vec, vec,
             pl.BlockSpec((1, HEAD_DIM), lambda b, n: (0, 0))]
    sshape = (N_HEADS, HEAD_DIM, HEAD_DIM)
    if has_state:
        args.append(state)
        specs.append(pl.BlockSpec((1,) + sshape, lambda b, n: (b, 0, 0, 0)))
    return pl.pallas_call(
        functools.partial(_hgrn_kernel, c=c, nc=nc, has_state=has_state),
        grid=(bsz, nc), in_specs=specs,
        out_specs=[pl.BlockSpec((c, D_MIX), lambda b, n: (b * nc + n, 0)),
                   pl.BlockSpec((1,) + sshape, lambda b, n: (b, 0, 0, 0))],
        out_shape=[jax.ShapeDtypeStruct((bsz * length, D_MIX), BF16),
                   jax.ShapeDtypeStruct((bsz,) + sshape, F32)],
        scratch_shapes=[pltpu.VMEM(sshape, F32)],
        compiler_params=_cparams(("arbitrary", "arbitrary")), name="hgrn2",
    )(*args)


def _rwkv_kernel(*refs, c, nsub, nc, has_state):
    it = iter(refs)
    (z_ref, mu_ref, w0_ref, w2_ref, a0_ref, a2_ref, g2_ref, kk_ref, ka_ref, rk_ref,
     lnw_ref, lnb_ref) = (next(it) for _ in range(12))
    if has_state:
        sh0_ref, s0_ref = next(it), next(it)
    y_ref, so_ref, s_scr, prev_scr = next(it), next(it), next(it), next(it)
    n = pl.program_id(1)
    nh, hd = RWKV_HEADS, RWKV_N

    @pl.when(n == 0)
    def _():
        s_scr[...] = s0_ref[0] if has_state else jnp.zeros(s_scr.shape, F32)
        prev_scr[...] = (jnp.broadcast_to(sh0_ref[0], prev_scr.shape) if has_state
                         else jnp.zeros(prev_scr.shape, F32))

    z = z_ref[...]
    rows = nsub * c
    row = lax.broadcasted_iota(jnp.int32, z.shape, 0)
    prev = jnp.where(row == 0, prev_scr[0:1, :], pltpu.roll(z, 1, 0))
    prev_scr[...] = jnp.broadcast_to(z[rows - 1:rows, :], prev_scr.shape)
    xs = z + (prev - z) * mu_ref[...]
    o3 = 3 * D_MIX
    r, k, v = xs[:, :D_MIX], xs[:, D_MIX:2 * D_MIX], xs[:, 2 * D_MIX:o3]
    w_lo, a_lo, g_lo = xs[:, o3:o3 + 64], xs[:, o3 + 64:o3 + 128], xs[:, o3 + 128:o3 + 256]
    w = -_softplus(-(w0_ref[...] + _mm(jnp.tanh(w_lo), w2_ref[...]))) - 0.5
    ld = -jnp.exp(w)
    a = jax.nn.sigmoid(a0_ref[...] + _mm(a_lo, a2_ref[...]))
    g = _mm(jax.nn.sigmoid(g_lo), g2_ref[...])
    kkv = k * kk_ref[...]
    k = k * (1.0 + (a - 1.0) * ka_ref[...])
    bonus = r * k * rk_ref[...]

    r2 = lax.broadcasted_iota(jnp.int32, (2 * c, 2 * c), 0)
    c2 = lax.broadcasted_iota(jnp.int32, (2 * c, 2 * c), 1)
    tq = jnp.where(r2 >= c, r2 - c, r2)
    ts = jnp.where(c2 >= c, c2 - c, c2)
    keep = (tq > ts) | ((r2 >= c) & (tq == ts))
    heads, subs = range(nh), range(nsub)
    pairs = [(j, h) for j in subs for h in heads]
    sls = [slice(h * hd, (h + 1) * hd) for h in heads]
    lhs, rhs, vs, gam = {}, {}, {}, {}
    for j in subs:
        rs = slice(j * c, (j + 1) * c)
        ldj = ld[rs]
        lc = _cumsum_rows(ldj)
        e_in, e_in_neg, e_ex = jnp.exp(lc), jnp.exp(-lc), jnp.exp(lc - ldj)
        for h in heads:
            sl = sls[h]
            kkh = kkv[rs, sl]
            kap = kkh * lax.rsqrt(jnp.sum(kkh * kkh, axis=-1, keepdims=True) + 1e-6)
            lhs[j, h] = jnp.concatenate([kap * e_ex[:, sl], r[rs, sl] * e_in[:, sl]], axis=0)
            rhs[j, h] = jnp.concatenate([kap * a[rs, sl] * e_in_neg[:, sl], k[rs, sl] * e_in_neg[:, sl]], axis=0)
            vs[j, h] = v[rs, sl]
            gam[j, h] = e_in[c - 1:c, sl]
    pm = {p: jnp.where(keep, _mm_nt(lhs[p], rhs[p]), 0.0) for p in pairs}
    zv = jnp.zeros((c, hd), F32)
    mkv = {p: _mm(pm[p][:c, :], jnp.concatenate([zv, vs[p]], axis=0)) for p in pairs}
    t_inv = dict(zip(pairs, _tri_inv([pm[p][:c, :c] for p in pairs], c)))
    s_cur = [s_scr[h] for h in heads]
    ys = {}
    for j in subs:
        qs = [_mm_nt(lhs[j, h], s_cur[h]) for h in heads]
        us = [-_mm(t_inv[j, h], qs[h][:c, :] + mkv[j, h]) for h in heads]
        uv = [jnp.concatenate([us[h], vs[j, h]], axis=0) for h in heads]
        for h in heads:
            ys[j, h] = qs[h][c:, :] + _mm(pm[j, h][c:, :], uv[h])
        s_cur = [(s_cur[h] + _mm_tn(uv[h], rhs[j, h])) * gam[j, h] for h in heads]
    for h in heads:
        s_scr[h] = s_cur[h]
    for j, h in pairs:
        rs, sl = slice(j * c, (j + 1) * c), sls[h]
        y = ys[j, h]
        yc = y - jnp.mean(y, axis=-1, keepdims=True)
        yn = yc * lax.rsqrt(jnp.mean(yc * yc, axis=-1, keepdims=True) + RWKV_GN_EPS)
        yn = yn * lnw_ref[:, sl] + lnb_ref[:, sl]
        yn = yn + jnp.sum(bonus[rs, sl], axis=-1, keepdims=True) * vs[j, h]
        y_ref[rs, sl] = (yn * g[rs, sl]).astype(y_ref.dtype)

    @pl.when(n == nc - 1)
    def _():
        so_ref[0] = s_scr[...]


def _sub_chunks(length, c, most):
    nc = length // c
    while nc % most:
        most //= 2
    return most


def _rwkv(z, row0, bsz, length, c, p, shift, state):
    nsub = _sub_chunks(length, c, 2)
    cb = c * nsub
    nc = length // cb
    rb = _row_block(row0, cb, nc)
    has_state = state is not None
    pad = MIX_W - RWKV_COLS
    c2 = lambda b, n: (0, 0)
    args = [z, jnp.pad(p["mu"], (0, pad)).reshape(1, MIX_W),
            p["w0"].reshape(1, D_MIX), p["w2"], p["a0"].reshape(1, D_MIX), p["a2"], p["g2"],
            p["k_k"].reshape(1, D_MIX), p["k_a"].reshape(1, D_MIX), p["r_k"].reshape(1, D_MIX),
            p["ln_w"].reshape(1, D_MIX), p["ln_b"].reshape(1, D_MIX)]
    specs = [pl.BlockSpec((cb, MIX_W), lambda b, n: (rb(b, n), Z_RWKV // MIX_W))]
    specs += [pl.BlockSpec(a.shape, c2) for a in args[1:]]
    sshape = (RWKV_HEADS, RWKV_N, RWKV_N)
    if has_state:
        args += [jnp.pad(shift, ((0, 0), (0, pad))).reshape(bsz, 1, MIX_W), state]
        specs += [pl.BlockSpec((1, 1, MIX_W), lambda b, n: (b, 0, 0)),
                  pl.BlockSpec((1,) + sshape, lambda b, n: (b, 0, 0, 0))]
    return pl.pallas_call(
        functools.partial(_rwkv_kernel, c=c, nsub=nsub, nc=nc, has_state=has_state),
        grid=(bsz, nc), in_specs=specs,
        out_specs=[pl.BlockSpec((cb, D_MIX), lambda b, n: (b * nc + n, 0)),
                   pl.BlockSpec((1,) + sshape, lambda b, n: (b, 0, 0, 0))],
        out_shape=[jax.ShapeDtypeStruct((bsz * length, D_MIX), BF16),
                   jax.ShapeDtypeStruct((bsz,) + sshape, F32)],
        scratch_shapes=[pltpu.VMEM(sshape, F32), pltpu.VMEM((8, MIX_W), F32)],
        compiler_params=_cparams(("arbitrary", "arbitrary")), name="rwkv7",
    )(*args)


CONV_PAD = 8


def _gdn_kernel(*refs, c, nsub, nc, has_state):
    it = iter(refs)
    n_win = 4 * D_MIX // GDN_W
    z_refs = [next(it) for _ in range(n_win)]
    ab_ref, cw_ref, alog_ref, dtb_ref, gn_ref = (next(it) for _ in range(5))
    if has_state:
        cv0_ref, s0_ref = next(it), next(it)
    y_ref, so_ref, s_scr, xp_scr = next(it), next(it), next(it), next(it)
    n = pl.program_id(1)
    qkv_w = 3 * D_MIX
    qkv_win = qkv_w // GDN_W
    tail = GDN_CONV - 1
    rows = nsub * c

    @pl.when(n == 0)
    def _():
        s_scr[...] = s0_ref[0] if has_state else jnp.zeros(s_scr.shape, F32)
        xp_scr[0:CONV_PAD, :] = jnp.zeros((CONV_PAD, qkv_w), F32)
        if has_state:
            xp_scr[CONV_PAD - tail:CONV_PAD, :] = cv0_ref[0]

    for j in range(qkv_win):
        xp_scr[CONV_PAD:CONV_PAD + rows, j * GDN_W:(j + 1) * GDN_W] = z_refs[j][...]
    conv = xp_scr[CONV_PAD - tail:CONV_PAD - tail + rows, :] * cw_ref[0:1, :]
    for j in range(1, GDN_CONV):
        conv = conv + xp_scr[CONV_PAD - tail + j:CONV_PAD - tail + j + rows, :] * cw_ref[j:j + 1, :]
    xp_scr[CONV_PAD - tail:CONV_PAD, :] = xp_scr[CONV_PAD + rows - tail:CONV_PAD + rows, :]
    act = _silu(conv)

    ab = ab_ref[...]
    gdec = -jnp.exp(alog_ref[...]) * _softplus(ab + dtb_ref[...])
    beta = jax.nn.sigmoid(ab)

    r2 = lax.broadcasted_iota(jnp.int32, (c, c), 0)
    c2 = lax.broadcasted_iota(jnp.int32, (c, c), 1)
    eye, causal, strict = r2 == c2, r2 >= c2, r2 > c2
    heads, subs = range(N_HEADS), range(nsub)
    pairs = [(j, h) for j in subs for h in heads]
    kq, ks, vs, gcols, bcols, decs = {}, {}, {}, {}, {}, {}
    for j in subs:
        rs = slice(j * c, (j + 1) * c)
        gcum = _cumsum_rows(gdec[rs])
        for h in heads:
            lo = h * HEAD_DIM
            q = act[rs, lo:lo + HEAD_DIM]
            k = act[rs, D_MIX + lo:D_MIX + lo + HEAD_DIM]
            q = q * lax.rsqrt(jnp.sum(q * q, axis=-1, keepdims=True) + 1e-6) * HEAD_DIM ** -0.5
            k = k * lax.rsqrt(jnp.sum(k * k, axis=-1, keepdims=True) + 1e-6)
            gcol = gcum[:, h:h + 1]
            rel = gcol - _row_from_col(gcol, eye)
            kq[j, h] = jnp.concatenate([k, q], axis=0)
            ks[j, h] = k
            vs[j, h] = act[rs, 2 * D_MIX + lo:2 * D_MIX + lo + HEAD_DIM]
            gcols[j, h] = gcol
            bcols[j, h] = beta[rs, N_HEADS + h:N_HEADS + h + 1]
            decs[j, h] = jnp.where(causal, jnp.exp(jnp.where(causal, rel, 0.0)), 0.0)
    kk_qk = {p: _mm_nt(kq[p], ks[p]) for p in pairs}
    t_inv = dict(zip(pairs, _tri_inv(
        [jnp.where(strict, bcols[p] * kk_qk[p][:c] * decs[p], 0.0) for p in pairs], c)))
    s_cur = [s_scr[h] for h in heads]
    os_ = {}
    for j in subs:
        ks_qs = [_mm(kq[j, h], s_cur[h]) for h in heads]
        egs = [jnp.exp(gcols[j, h]) for h in heads]
        us = [_mm(t_inv[j, h], bcols[j, h] * (vs[j, h] - egs[h] * ks_qs[h][:c])) for h in heads]
        for h in heads:
            os_[j, h] = egs[h] * ks_qs[h][c:] + _mm(kk_qk[j, h][c:] * decs[j, h], us[h])
        gls = [gcols[j, h][c - 1:c, :] for h in heads]
        s_cur = [jnp.exp(gls[h]) * s_cur[h] + _mm_tn(ks[j, h] * jnp.exp(gls[h] - gcols[j, h]), us[h])
                 for h in heads]
    for h in heads:
        s_scr[h] = s_cur[h]
    for j, h in pairs:
        rs, lo = slice(j * c, (j + 1) * c), h * HEAD_DIM
        zg = z_refs[qkv_win + lo // GDN_W][rs, lo % GDN_W:lo % GDN_W + HEAD_DIM]
        y = _rms_rows(os_[j, h], gn_ref[...]) * _silu(zg)
        y_ref[rs, lo:lo + HEAD_DIM] = y.astype(y_ref.dtype)

    @pl.when(n == nc - 1)
    def _():
        so_ref[0] = s_scr[...]


def _gdn(z, row0, bsz, length, c, p, conv_state, state):
    nsub = _sub_chunks(length, c, 4)
    cb = c * nsub
    nc = length // cb
    rb = _row_block(row0, cb, nc)
    has_state = state is not None
    c2 = lambda b, n: (0, 0)
    lane_pad = lambda a: jnp.pad(a, (0, HEAD_DIM - a.shape[0])).reshape(1, HEAD_DIM)
    n_win = 4 * D_MIX // GDN_W
    params = [p["conv_w"], lane_pad(p["A_log"]), lane_pad(p["dt_bias"]), p["norm_g"].reshape(1, HEAD_DIM)]
    args = [z] * (n_win + 1) + params
    specs = [pl.BlockSpec((cb, GDN_W), lambda b, n, j=j: (rb(b, n), Z_GDN // GDN_W + j)) for j in range(n_win)]
    specs.append(pl.BlockSpec((cb, HEAD_DIM), lambda b, n: (rb(b, n), Z_AB // HEAD_DIM)))
    specs += [pl.BlockSpec(a.shape, c2) for a in params]
    sshape = (N_HEADS, HEAD_DIM, HEAD_DIM)
    if has_state:
        args += [conv_state, state]
        specs += [pl.BlockSpec((1, GDN_CONV - 1, 3 * D_MIX), lambda b, n: (b, 0, 0)),
                  pl.BlockSpec((1,) + sshape, lambda b, n: (b, 0, 0, 0))]
    return pl.pallas_call(
        functools.partial(_gdn_kernel, c=c, nsub=nsub, nc=nc, has_state=has_state),
        grid=(bsz, nc), in_specs=specs,
        out_specs=[pl.BlockSpec((cb, D_MIX), lambda b, n: (b * nc + n, 0)),
                   pl.BlockSpec((1,) + sshape, lambda b, n: (b, 0, 0, 0))],
        out_shape=[jax.ShapeDtypeStruct((bsz * length, D_MIX), BF16),
                   jax.ShapeDtypeStruct((bsz,) + sshape, F32)],
        scratch_shapes=[pltpu.VMEM(sshape, F32), pltpu.VMEM((CONV_PAD + cb, 3 * D_MIX), F32)],
        compiler_params=_cparams(("arbitrary", "arbitrary")), name="gdn",
    )(*args)


def _merge_kernel(y0_ref, y1_ref, y2_ref, y3_ref, gate_ref, x_ref, gt_ref, sc_ref, sh_ref, g_ref,
                  wb_ref, wo_ref, rw_ref, rb_ref, xo_ref, h_ref, lg_ref, *, gpt):
    tm, d = x_ref.shape
    merged = jnp.zeros((tm, d), F32)
    for nb, y_ref in enumerate((y0_ref, y1_ref, y2_ref, y3_ref)):
        br = jnp.dot(y_ref[...], wb_ref[0, nb], preferred_element_type=F32)
        merged = merged + gate_ref[:, nb * d:(nb + 1) * d].astype(F32) * br
    m = jnp.dot(merged.astype(BF16), wo_ref[0], preferred_element_type=F32)
    gt = gt_ref[0]
    x = x_ref[...] + (m.reshape(gpt, GROUP, d) * gt[:, None, :]).reshape(tm, d)
    xo_ref[...] = x
    h = _modulate(_rms_rows(x, g_ref[...]), sc_ref[0], sh_ref[0], gpt)
    _rows_to_chunks(h_ref, h)
    lg_ref[...] = _mm(h, rw_ref[0]) + rb_ref[0]


def _merge(ys, gates, x, modg, l, g2, wb, wo, rw, rb, n_prompt):
    t, d = x.shape
    tm = 256
    gpt, gidx = _group_block(tm, n_prompt // tm)
    row = lambda w: pl.BlockSpec((tm, w), lambda i: (i, 0))

    def mod_spec(col):
        return pl.BlockSpec((1, gpt, d), lambda i: (l, gidx(i), col))

    def const(shape):
        return pl.BlockSpec(shape, lambda i: (0,) * len(shape), pipeline_mode=pl.Buffered(1))

    def layer(arr):
        return pl.BlockSpec((1,) + arr.shape[1:], lambda i: (l,) + (0,) * (arr.ndim - 1),
                            pipeline_mode=pl.Buffered(1))
    nr = rw.shape[2]
    return pl.pallas_call(
        functools.partial(_merge_kernel, gpt=gpt),
        grid=(t // tm,),
        in_specs=[row(D_MIX)] * 4 + [row(N_BRANCH * d), row(d), mod_spec(2), mod_spec(4), mod_spec(3),
                                     const((1, d)), layer(wb), layer(wo), layer(rw), layer(rb)],
        out_specs=[row(d), pl.BlockSpec((tm * ROW_CHUNKS, LANES), lambda i: (i, 0)), row(nr)],
        out_shape=[jax.ShapeDtypeStruct((t, d), F32), jax.ShapeDtypeStruct((t * ROW_CHUNKS, LANES), F32),
                   jax.ShapeDtypeStruct((t, nr), F32)],
        compiler_params=_cparams(("arbitrary",)), name="merge",
    )(*ys, gates, x, modg, modg, modg, g2.reshape(1, d), wb, wo, rw, rb)


MOE_BLK = 128
MOE_UNROLL = 8


def _moe_kernel(be_ref, tok_ref, slot_ref, nused_ref, h_hbm, w_ref, wg_ref, wu_ref, wd_ref, o_hbm,
                x0, x1, y0, y1, wg_bf, wu_bf, wd_bf, gsem, ssem):
    i = pl.program_id(0)
    nused = nused_ref[0]

    rc = ROW_CHUNKS

    def gather(blk, xb, q, r):
        return pltpu.make_async_copy(h_hbm.at[pl.ds(tok_ref[blk * MOE_BLK + r] * rc, rc)],
                                     xb.at[pl.ds(r * rc, rc)], gsem.at[q])

    def scatter(blk, yb, q, r):
        return pltpu.make_async_copy(yb.at[pl.ds(r * rc, rc)],
                                     o_hbm.at[pl.ds(slot_ref[(blk + 1) * MOE_BLK + r] * rc, rc)], ssem.at[q])

    def for_rows(fn):
        def group(j, carry):
            for u in range(MOE_UNROLL):
                fn(j * MOE_UNROLL + u)
            return carry
        lax.fori_loop(0, MOE_BLK // MOE_UNROLL, group, 0)

    def step(q, xa, ya, xb, yb):
        @pl.when(i == 0)
        def _():
            yb[...] = jnp.zeros(yb.shape, F32)
            n_real = o_hbm.shape[0] - 2 * MOE_BLK * rc
            init = pltpu.make_async_copy(yb, o_hbm.at[pl.ds(n_real, MOE_BLK * rc)], ssem.at[q])
            init.start()
            init.wait()
            for_rows(lambda r: gather(i, xa, q, r).start())

        @pl.when(i > 0)
        def _():
            for_rows(lambda r: scatter(i - 2, ya, q, r).wait())
        for_rows(lambda r: gather(i, xa, q, r).wait())

        @pl.when((i == 0) | (be_ref[i] != be_ref[jnp.maximum(i - 1, 0)]))
        def _():
            wg_bf[...] = wg_ref[0, 0].astype(BF16)
            wu_bf[...] = wu_ref[0, 0].astype(BF16)
            wd_bf[...] = wd_ref[0, 0].astype(BF16)

        for r in range(MOE_BLK):
            gather(i + 1, xb, 1 - q, r).start(priority=r % 2)
        for r in range(MOE_BLK):
            scatter(i - 1, yb, 1 - q, r).start(priority=r % 2)
        x = _rows_from_chunks(xa, MOE_BLK).astype(BF16)
        hid = _silu(jnp.dot(x, wg_bf[...], preferred_element_type=F32)) * jnp.dot(
            x, wu_bf[...], preferred_element_type=F32)
        _rows_to_chunks(ya, jnp.dot(hid.astype(BF16), wd_bf[...], preferred_element_type=F32) * w_ref[...])

        @pl.when(i == nused - 1)
        def _():
            for_rows(lambda r: scatter(i, ya, q, r).start())
            for_rows(lambda r: scatter(i - 1, yb, 1 - q, r).wait())
            for_rows(lambda r: scatter(i, ya, q, r).wait())
            for_rows(lambda r: gather(i + 1, xb, 1 - q, r).wait())

    @pl.when((i < nused) & (i % 2 == 0))
    def _():
        step(0, x0, y0, x1, y1)

    @pl.when((i < nused) & (i % 2 == 1))
    def _():
        step(1, x1, y1, x0, y0)


def _route(logits, n_tok):
    lt = logits[:, :N_GROUPS + N_EXPERTS].T
    lg = lt[:N_GROUPS]
    gsel = jnp.argmax(lg, axis=0).astype(jnp.int32)
    gw = jnp.max(jax.nn.softmax(lg, axis=0), axis=0)
    le_all = lt[N_GROUPS:].reshape(N_GROUPS, EXPERTS_PER_GROUP, n_tok)
    le = sum(jnp.where(gsel[None, :] == g, le_all[g], 0.0) for g in range(N_GROUPS))
    pe = jax.nn.softmax(le, axis=0)
    rows8 = lax.broadcasted_iota(jnp.int32, pe.shape, 0)
    i1 = jnp.argmax(pe, axis=0).astype(jnp.int32)
    v1 = jnp.max(pe, axis=0)
    pe_rest = jnp.where(rows8 == i1[None, :], -jnp.inf, pe)
    i2 = jnp.argmax(pe_rest, axis=0).astype(jnp.int32)
    v2 = jnp.max(pe_rest, axis=0)
    top_v = jnp.stack([v1, v2], axis=1)
    top_i = jnp.stack([i1, i2], axis=1)
    wts = top_v / jnp.sum(top_v, axis=-1, keepdims=True) * gw[:, None]
    eid = (gsel[:, None] * EXPERTS_PER_GROUP + top_i).reshape(-1).astype(jnp.int32)
    a = n_tok * TOPK
    order = jnp.argsort(eid).astype(jnp.int32)
    onehot = (eid[None, :] == jnp.arange(N_EXPERTS, dtype=jnp.int32)[:, None]).astype(BF16)
    counts = jnp.dot(onehot, jnp.ones((a, 1), BF16), preferred_element_type=F32)[:, 0].astype(jnp.int32)
    padded = (counts + MOE_BLK - 1) // MOE_BLK * MOE_BLK
    pend = jnp.cumsum(padded)
    pstart = pend - padded
    cstart = jnp.cumsum(counts) - counts
    n_blocks = -(-a // MOE_BLK) + N_EXPERTS
    rows = n_blocks * MOE_BLK
    blk_row0 = jnp.arange(n_blocks, dtype=jnp.int32) * MOE_BLK
    block_e = jnp.minimum(jnp.sum((pend[None, :] <= blk_row0[:, None]).astype(jnp.int32), axis=1),
                          N_EXPERTS - 1).astype(jnp.int32)
    nused = (pend[-1:] // MOE_BLK).astype(jnp.int32)
    e_row = jnp.repeat(block_e, MOE_BLK)
    row_id = jnp.arange(rows, dtype=jnp.int32)
    j_row = row_id - pstart[e_row]
    valid = j_row < counts[e_row]
    asg = order[jnp.clip(cstart[e_row] + j_row, 0, a - 1)]
    tok = jnp.where(valid, asg // TOPK, 0)
    dummy = a + ((row_id // MOE_BLK) % 2) * MOE_BLK + row_id % MOE_BLK
    slot = jnp.where(valid, (asg % TOPK) * n_tok + asg // TOPK, dummy)
    wrow = jnp.where(valid, wts.reshape(-1)[asg], 0.0)
    tok = jnp.concatenate([tok, jnp.zeros((MOE_BLK,), jnp.int32)])
    slot = jnp.concatenate([a + MOE_BLK + jnp.arange(MOE_BLK, dtype=jnp.int32), slot])
    return block_e, tok, slot, nused, wrow.reshape(rows, 1), n_blocks


def _moe(h, logits, wg, wu, wd, l):
    t, d = h.shape[0] // ROW_CHUNKS, D_MODEL
    block_e, tok, slot, nused, wrow, n_blocks = _route(logits, t)
    grid_spec = pltpu.PrefetchScalarGridSpec(
        num_scalar_prefetch=4, grid=(n_blocks,),
        in_specs=[pl.BlockSpec(memory_space=pl.ANY),
                  pl.BlockSpec((MOE_BLK, 1), lambda i, be, *_: (i, 0)),
                  pl.BlockSpec((1, 1, d, D_EXPERT), lambda i, be, *_: (l, be[i], 0, 0)),
                  pl.BlockSpec((1, 1, d, D_EXPERT), lambda i, be, *_: (l, be[i], 0, 0)),
                  pl.BlockSpec((1, 1, D_EXPERT, d), lambda i, be, *_: (l, be[i], 0, 0))],
        out_specs=pl.BlockSpec(memory_space=pl.ANY),
        scratch_shapes=[pltpu.VMEM((MOE_BLK * ROW_CHUNKS, LANES), F32)] * 4 + [
                        pltpu.VMEM((d, D_EXPERT), BF16), pltpu.VMEM((d, D_EXPERT), BF16),
                        pltpu.VMEM((D_EXPERT, d), BF16),
                        pltpu.SemaphoreType.DMA((2,)), pltpu.SemaphoreType.DMA((2,))])
    out = pl.pallas_call(
        _moe_kernel, grid_spec=grid_spec,
        out_shape=jax.ShapeDtypeStruct(((t * TOPK + 2 * MOE_BLK) * ROW_CHUNKS, LANES), F32),
        compiler_params=_cparams(("arbitrary",)), name="moe",
    )(block_e, tok, slot, nused, h, wrow, wg, wu, wd)
    return out


def _pack_w_in(w_in):
    cut = -(-Z_GATE // LANES) * LANES
    w_main = jnp.pad(w_in[:, :, :cut].astype(BF16), ((0, 0), (0, 0), (0, N_MAIN - cut)))
    return w_main, w_in[:, :, Z_GATE:].astype(BF16)


def _mixers(z, l, groups, params):
    outs = [[] for _ in range(N_BRANCH)]
    states = []
    for (row0, bsz, length, c, pos0, st) in groups:
        s_ret, s_hgrn, s_rwkv, s_shift, s_gdn, s_conv = st if st is not None else (None,) * 6
        y_a, n_ret = _retention(z, row0, bsz, length, min(length, RET_CHUNK), pos0, s_ret)
        y_b, n_hgrn = _hgrn(z, row0, bsz, length, c, params["hgrn_lb"][l], params["hgrn_norm_g"][l], s_hgrn)
        y_c, n_rwkv = _rwkv(z, row0, bsz, length, c, {k: v[l] for k, v in params["rwkv"].items()},
                            s_shift, s_rwkv)
        y_d, n_gdn = _gdn(z, row0, bsz, length, c, {k: v[l] for k, v in params["gdn"].items()},
                          s_conv, s_gdn)
        for lst, y in zip(outs, (y_a, y_b, y_c, y_d)):
            lst.append(y)
        def seq_row(j, col0, width):
            return lax.slice(z, (row0 + j, col0), (row0 + (bsz - 1) * length + j + 1, col0 + width),
                             (length, 1))
        n_shift = seq_row(length - 1, Z_RWKV, RWKV_COLS)
        n_conv = jnp.stack([seq_row(length - (GDN_CONV - 1) + j, Z_GDN, 3 * D_MIX)
                            for j in range(GDN_CONV - 1)], axis=1)
        states.append((n_ret, n_hgrn, n_rwkv, n_shift, n_gdn, n_conv))
    return [jnp.concatenate(lst, axis=0) for lst in outs], states


def kernel(x_prompt, x_sample, c_prompt, c_sample, state_ret, state_hgrn, state_rwkv, state_rwkv_shift,
           state_gdn, state_gdn_conv, ada_w, ada_b, norm1_g, norm2_g, w_in, hgrn_lb_logits, hgrn_norm_g,
           rwkv_mu, rwkv_w0, rwkv_w2, rwkv_a0, rwkv_a2, rwkv_g2, rwkv_k_k, rwkv_k_a, rwkv_r_k, rwkv_ln_w,
           rwkv_ln_b, gdn_conv_w, gdn_A_log, gdn_dt_bias, gdn_norm_g, w_branch, w_out, router_g, router_g_b,
           router_e, router_e_b, moe_w_gate, moe_w_up, moe_w_down, final_norm_g):
    depth = ada_w.shape[0]
    bp, lp, d = x_prompt.shape
    bs, ls, _ = x_sample.shape
    n_prompt = bp * lp
    assert bp == 1 and ls == GROUP and bs == 16

    lb_cum = jnp.cumsum(jax.nn.softmax(hgrn_lb_logits.astype(F32), axis=0), axis=0)
    params = dict(
        hgrn_lb=lb_cum - lb_cum[:1], hgrn_norm_g=hgrn_norm_g,
        rwkv=dict(mu=rwkv_mu, w0=rwkv_w0, w2=rwkv_w2, a0=rwkv_a0, a2=rwkv_a2, g2=rwkv_g2, k_k=rwkv_k_k,
                  k_a=rwkv_k_a, r_k=rwkv_r_k.reshape(depth, D_MIX), ln_w=rwkv_ln_w, ln_b=rwkv_ln_b),
        gdn=dict(conv_w=gdn_conv_w, A_log=gdn_A_log, dt_bias=gdn_dt_bias, norm_g=gdn_norm_g))

    c_all = jnp.concatenate([c_prompt, c_sample, jnp.zeros((24 - bp - bs, d), F32)], axis=0)
    mod = _ada(c_all, ada_w, ada_b)
    modg = jnp.concatenate([jnp.broadcast_to(mod[:, :1], (depth, 16, 6 * d)), mod[:, 1:1 + bs]], axis=1)

    w_main, w_gate = _pack_w_in(w_in)
    wb_bf, wo_bf = w_branch.astype(BF16), w_out.astype(BF16)
    n_r = 128
    r_w = jnp.concatenate([router_g, router_e, jnp.zeros((depth, d, n_r - N_GROUPS - N_EXPERTS), F32)], axis=2)
    r_b = jnp.concatenate([router_g_b, router_e_b, jnp.zeros((depth, n_r - N_GROUPS - N_EXPERTS), F32)],
                          axis=1).reshape(depth, 1, n_r)

    x = jnp.concatenate([x_prompt.reshape(n_prompt, d), x_sample.reshape(bs * ls, d)], axis=0)
    moe_out = None
    new_p, new_s = [], []
    for l in range(depth):
        x, h = _norm(x, moe_out, modg, l, norm1_g[l], n_prompt, final=False)
        z = _proj(h, w_main, l, tn=1024, sigmoid=False, out_dtype=F32)
        gates = _proj(h, w_gate, l, tn=1024, sigmoid=True, out_dtype=BF16)
        groups = [(0, bp, lp, 64, 0.0, None),
                  (n_prompt, bs, ls, ls, float(PAST_LEN),
                   (state_ret[l], state_hgrn[l], state_rwkv[l], state_rwkv_shift[l], state_gdn[l],
                    state_gdn_conv[l]))]
        ys, (st_p, st_s) = _mixers(z, l, groups, params)
        new_p.append(st_p)
        new_s.append(st_s)
        x, h2, logits = _merge(ys, gates, x, modg, l, norm2_g[l], wb_bf, wo_bf, r_w, r_b, n_prompt)
        moe_out = _moe(h2, logits, moe_w_gate, moe_w_up, moe_w_down, l)
    y = _norm(x, moe_out, modg, depth - 1, final_norm_g, n_prompt, final=True)

    def stack(lst, i):
        return jnp.stack([s[i] for s in lst]).astype(F32)
    return ((y[:n_prompt].reshape(bp, lp, d), y[n_prompt:].reshape(bs, ls, d))
            + tuple(stack(new_p, i) for i in range(6)) + tuple(stack(new_s, i) for i in range(6)))
```
